```python
import jax, jax.numpy as jnp
from jax import lax
import numpy as np

D_MODEL = 2048
BATCH = 4
SEQ = 8192
DEPTH = 1

GLA_HEADS = 4
GLA_KEY_DIM = D_MODEL // 2
GLA_VALUE_DIM = D_MODEL
GLA_HEAD_K = GLA_KEY_DIM // GLA_HEADS
GLA_HEAD_V = GLA_VALUE_DIM // GLA_HEADS
GLA_GATE_RANK = 16
GLA_GATE_NORMALIZER = 16.0
GLA_CHUNK = 64
POOL_WINDOWS = (2, 4, 8, 16)
POOL_GROUPS = 4
POOL_DIM = D_MODEL // 2
POOL_GROUP_DIM = POOL_DIM // POOL_GROUPS
IN_SPLITS = (GLA_KEY_DIM, GLA_KEY_DIM, GLA_VALUE_DIM, GLA_VALUE_DIM, GLA_GATE_RANK, POOL_DIM, D_MODEL, D_MODEL)
IN_DIM = 2 * GLA_KEY_DIM + 2 * GLA_VALUE_DIM + GLA_GATE_RANK + POOL_DIM + 2 * D_MODEL
N_GROUPS = 8
EXPERTS_PER_GROUP = 8
N_EXPERTS = N_GROUPS * EXPERTS_PER_GROUP
TOP_K = 2
EXPERT_HIDDEN = D_MODEL // 2
MOE_BLOCK = 128
NORM_EPS = 1e-6

kernel_name = "hybrid_gla_pool_hiermoe_adaln"


def rms_norm(x, gain):
    xf = x.astype(jnp.float32)
    y = xf * lax.rsqrt(jnp.mean(xf * xf, axis=-1, keepdims=True) + NORM_EPS)
    return (y * gain.astype(jnp.float32)).astype(x.dtype)


def modulate(x, shift, scale):
    return x * (1 + scale[:, None, :]) + shift[:, None, :]


def split_cols(z, sizes):
    outs, off = [], 0
    for s in sizes:
        outs.append(z[..., off:off + s])
        off += s
    return outs


def gla_chunked(q, k, v, log_a):
    B, H, S, dk = q.shape
    dv = v.shape[-1]
    C = GLA_CHUNK
    N = S // C
    q, k, log_a = (t.reshape(B, H, N, C, dk) for t in (q, k, log_a))
    v = v.reshape(B, H, N, C, dv)
    b = jnp.cumsum(log_a, axis=3)
    b_last = b[:, :, :, -1:, :]
    q_e = q * jnp.exp(b)
    k_e = k * jnp.exp(-b)
    k_dec = k * jnp.exp(b_last - b)
    causal = jnp.tril(jnp.ones((C, C), dtype=bool))
    att = jnp.einsum('bhnid,bhnjd->bhnij', q_e, k_e)
    att = jnp.where(causal, att, 0.0)
    o_intra = jnp.einsum('bhnij,bhnjv->bhniv', att, v)
    decay = jnp.exp(b_last[:, :, :, 0, :])

    def step(state, xs):
        qn, kn, vn, dn = xs
        o = jnp.einsum('bhid,bhdv->bhiv', qn, state)
        state = dn[..., None] * state + jnp.einsum('bhid,bhiv->bhdv', kn, vn)
        return state, o

    xs = tuple(jnp.moveaxis(t, 2, 0) for t in (q_e, k_dec, v, decay))
    _, o_inter = lax.scan(step, jnp.zeros((B, H, dk, dv), jnp.float32), xs)
    o = o_intra + jnp.moveaxis(o_inter, 0, 2)
    return o.reshape(B, H, S, dv)


def gla_branch(q, k, v, g, gk_lr, w_gk2, b_gk2, gla_norm):
    B, S, _ = q.shape
    heads = lambda t, d: t.reshape(B, S, GLA_HEADS, d).transpose(0, 2, 1, 3).astype(jnp.float32)
    qh = heads(q, GLA_HEAD_K) * (GLA_HEAD_K ** -0.5)
    kh = heads(k, GLA_HEAD_K)
    vh = heads(v, GLA_HEAD_V)
    log_a = jax.nn.log_sigmoid((gk_lr @ w_gk2 + b_gk2).astype(jnp.float32)) / GLA_GATE_NORMALIZER
    o = gla_chunked(qh, kh, vh, heads(log_a, GLA_HEAD_K))
    o = o.transpose(0, 2, 1, 3)
    o = rms_norm(o, gla_norm) * jax.nn.silu(g.reshape(B, S, GLA_HEADS, GLA_HEAD_V).astype(jnp.float32))
    return o.reshape(B, S, GLA_VALUE_DIM).astype(q.dtype)


def pool_branch(p, w_pool, pool_scale):
    B, S, _ = p.shape
    pf = p.astype(jnp.float32).reshape(B, S, POOL_GROUPS, POOL_GROUP_DIM)
    cs = jnp.cumsum(pf, axis=1)
    pos = jnp.arange(S)
    outs = []
    for gi, w in enumerate(POOL_WINDOWS):
        csg = cs[:, :, gi]
        lagged = jnp.pad(csg, ((0, 0), (w, 0), (0, 0)))[:, :S]
        count = jnp.minimum(pos + 1, w).astype(jnp.float32)[None, :, None]
        outs.append((csg - lagged) / count - pf[:, :, gi])
    m = jnp.stack(outs, axis=2)
    y = jnp.einsum('bsgc,gcd->bsgd', m, w_pool.astype(jnp.float32))
    return (y.reshape(B, S, POOL_DIM) * pool_scale.astype(jnp.float32)).astype(p.dtype)


def hier_moe(u, w_rg, b_rg, w_re, b_re, w_gate, w_up, w_down):
    B, S, D = u.shape
    T = B * S
    xt = u.reshape(T, D)
    glog = (xt @ w_rg).astype(jnp.float32) + b_rg.astype(jnp.float32)
    gprob = jax.nn.softmax(glog, axis=-1)
    gsel = jnp.argmax(glog, axis=-1).astype(jnp.int32)
    pg = jnp.take_along_axis(gprob, gsel[:, None], axis=1)[:, 0]
    elog = ((xt @ w_re).astype(jnp.float32) + b_re.astype(jnp.float32)).reshape(T, N_GROUPS, EXPERTS_PER_GROUP)
    elog = jnp.take_along_axis(elog, gsel[:, None, None], axis=1)[:, 0]
    eprob = jax.nn.softmax(elog, axis=-1)
    topv, topi = lax.top_k(eprob, TOP_K)
    topv = topv / jnp.sum(topv, axis=-1, keepdims=True)
    wts = (pg[:, None] * topv).reshape(-1)
    eid = (gsel[:, None] * EXPERTS_PER_GROUP + topi).reshape(-1).astype(jnp.int32)
    tok = jnp.repeat(jnp.arange(T, dtype=jnp.int32), TOP_K)
    A = T * TOP_K
    order = jnp.argsort(eid)
    e_sorted = eid[order]
    counts = jnp.bincount(eid, length=N_EXPERTS)
    starts = jnp.cumsum(counts) - counts
    padded = (counts + MOE_BLOCK - 1) // MOE_BLOCK * MOE_BLOCK
    pends = jnp.cumsum(padded)
    pstarts = pends - padded
    dest = pstarts[e_sorted] + jnp.arange(A, dtype=jnp.int32) - starts[e_sorted]
    P = (A + MOE_BLOCK - 1) // MOE_BLOCK * MOE_BLOCK + N_EXPERTS * MOE_BLOCK
    n_blocks = P // MOE_BLOCK
    buf_tok = jnp.full((P,), T, jnp.int32).at[dest].set(tok[order])
    buf_w = jnp.zeros((P,), jnp.float32).at[dest].set(wts[order])
    blk_e = jnp.clip(jnp.searchsorted(pends, jnp.arange(n_blocks, dtype=jnp.int32) * MOE_BLOCK, side='right'),
                     0, N_EXPERTS - 1)
    x_pad = jnp.concatenate([xt, jnp.zeros((1, D), xt.dtype)], axis=0)

    def expert_block(args):
        tb, wb, e = args
        xb = x_pad[tb]
        hdn = jax.nn.silu(xb @ w_gate[e]) * (xb @ w_up[e])
        return (hdn @ w_down[e]) * wb[:, None].astype(xb.dtype)

    yb = lax.map(expert_block, (buf_tok.reshape(n_blocks, MOE_BLOCK), buf_w.reshape(n_blocks, MOE_BLOCK), blk_e))
    y = jax.ops.segment_sum(yb.reshape(P, D), buf_tok, num_segments=T + 1)[:T]
    return y.reshape(B, S, D)


def setup_inputs(seed: int = 0) -> dict:
    key = jax.random.key(seed)
    ks = jax.random.split(key, 24)
    L, D = DEPTH, D_MODEL
    nrm = lambda k, shape, s: jax.random.normal(k, shape, jnp.float32) * s
    return {
        "x": nrm(ks[0], (BATCH, SEQ, D), 1.0),
        "c": nrm(ks[1], (BATCH, D), 1.0),
        "w_ada": nrm(ks[2], (L, D, 6 * D), 0.5 * D ** -0.5),
        "b_ada": nrm(ks[3], (L, 6 * D), 0.02),
        "norm_mix": 1.0 + nrm(ks[4], (L, D), 0.02),
        "w_in": nrm(ks[5], (L, D, IN_DIM), D ** -0.5),
        "w_gk2": nrm(ks[6], (L, GLA_GATE_RANK, GLA_KEY_DIM), GLA_GATE_RANK ** -0.5),
        "b_gk2": nrm(ks[7], (L, GLA_KEY_DIM), 0.1),
        "gla_norm": 1.0 + nrm(ks[8], (L, GLA_HEAD_V), 0.02),
        "w_a": nrm(ks[9], (L, GLA_VALUE_DIM, D), GLA_VALUE_DIM ** -0.5),
        "w_pool": nrm(ks[10], (L, POOL_GROUPS, POOL_GROUP_DIM, POOL_GROUP_DIM), POOL_GROUP_DIM ** -0.5),
        "pool_scale": 1.0 + nrm(ks[11], (L, POOL_DIM), 0.1),
        "w_b": nrm(ks[12], (L, POOL_DIM, D), POOL_DIM ** -0.5),
        "w_out": nrm(ks[13], (L, D, D), D ** -0.5),
        "norm_ffn": 1.0 + nrm(ks[14], (L, D), 0.02),
        "w_rg": nrm(ks[15], (L, D, N_GROUPS), D ** -0.5),
        "b_rg": nrm(ks[16], (L, N_GROUPS), 0.01),
        "w_re": nrm(ks[17], (L, D, N_EXPERTS), D ** -0.5),
        "b_re": nrm(ks[18], (L, N_EXPERTS), 0.01),
        "w_gate": nrm(ks[19], (L, N_EXPERTS, D, EXPERT_HIDDEN), D ** -0.5),
        "w_up": nrm(ks[20], (L, N_EXPERTS, D, EXPERT_HIDDEN), D ** -0.5),
        "w_down": nrm(ks[21], (L, N_EXPERTS, EXPERT_HIDDEN, D), EXPERT_HIDDEN ** -0.5),
        "norm_final": 1.0 + nrm(ks[22], (D,), 0.02),
    }


def reference(x, c, w_ada, b_ada, norm_mix, w_in, w_gk2, b_gk2, gla_norm, w_a, w_pool, pool_scale, w_b, w_out,
              norm_ffn, w_rg, b_rg, w_re, b_re, w_gate, w_up, w_down, norm_final):
    h = x
    for l in range(DEPTH):
        mod = jax.nn.silu(c) @ w_ada[l] + b_ada[l]
        shift1, scale1, gate1, shift2, scale2, gate2 = jnp.split(mod, 6, axis=-1)
        u = modulate(rms_norm(h, norm_mix[l]), shift1, scale1)
        z = u @ w_in[l]
        q, k, v, g, gk_lr, p, ga, gb = split_cols(z, IN_SPLITS)
        y_a = gla_branch(q, k, v, g, gk_lr, w_gk2[l], b_gk2[l], gla_norm[l]) @ w_a[l]
        y_b = pool_branch(p, w_pool[l], pool_scale[l]) @ w_b[l]
        merged = jax.nn.sigmoid(ga) * y_a + jax.nn.sigmoid(gb) * y_b
        h = h + gate1[:, None, :] * (merged @ w_out[l])
        u2 = modulate(rms_norm(h, norm_ffn[l]), shift2, scale2)
        h = h + gate2[:, None, :] * hier_moe(u2, w_rg[l], b_rg[l], w_re[l], b_re[l], w_gate[l], w_up[l], w_down[l])
    return rms_norm(h, norm_final)
```

```python
import functools

import jax
import jax.numpy as jnp
from jax import lax
from jax.experimental import pallas as pl
from jax.experimental.pallas import tpu as pltpu

F32 = jnp.float32
BF16 = jnp.bfloat16
I32 = jnp.int32

D = 2048
HEADS = 4
DK = 256
DV = 512
KEY_DIM = HEADS * DK
VAL_DIM = HEADS * DV
GATE_RANK = 16
GATE_NORMALIZER = 16.0
CHUNK = 64
POOL_WINDOWS = (2, 4, 8, 16)
POOL_DIM = 1024
POOL_GROUP_DIM = 256
POOL_HALO = 16
N_GROUPS = 8
EXPERTS_PER_GROUP = 8
N_EXPERTS = 64
EXPERT_HIDDEN = 1024
EPS = 1e-6
LANES = 128
ROW_SUB = D // LANES

Z_COLS = 2 * KEY_DIM + 2 * VAL_DIM + 2 * D + POOL_DIM

VMEM_LIMIT = 56 * 1024 * 1024

TM_IN = 512
TN_IN = 1024
TC_GLA = 512
TM_POOL = 256
TM_MIX = 256
TM_ROUTE = 512
TM_ROWS = 256
ROW_BLOCK = 256


def _sigmoid(x):
    return 1.0 / (1.0 + jnp.exp(-x))


def _params(sem, vmem=VMEM_LIMIT):
    return pltpu.CompilerParams(dimension_semantics=sem, vmem_limit_bytes=vmem)


def _ada_body(c_ref, w_ref, b_ref, o_ref):
    c = c_ref[...]
    s = (c * _sigmoid(c)).astype(BF16)
    o_ref[...] = jnp.dot(s, w_ref[...].astype(BF16), preferred_element_type=F32) + b_ref[...]


def _ada(c, w, b):
    bsz = c.shape[0]
    cp = jnp.zeros((8, D), F32).at[:bsz].set(c)
    n = w.shape[1]
    tn = 1024
    out = pl.pallas_call(
        _ada_body,
        grid=(n // tn,),
        in_specs=[pl.BlockSpec((8, D), lambda j: (0, 0)),
                  pl.BlockSpec((D, tn), lambda j: (0, j)),
                  pl.BlockSpec((1, tn), lambda j: (0, j))],
        out_specs=pl.BlockSpec((8, tn), lambda j: (0, j)),
        out_shape=jax.ShapeDtypeStruct((8, n), F32),
        compiler_params=_params(("arbitrary",)),
        name="ada",
    )(cp, w, b.reshape(1, n))
    return out[:bsz]


def _inproj_body(x_ref, g_ref, sc_ref, sh_ref, w_ref, wgk_ref, z_ref, gk_ref, u_scr):
    @pl.when(pl.program_id(1) == 0)
    def _():
        x = x_ref[...]
        ms = jnp.mean(x * x, axis=-1, keepdims=True)
        u = x * lax.rsqrt(ms + EPS) * g_ref[...]
        u = (u * (1.0 + sc_ref[0]) + sh_ref[0]).astype(BF16)
        u_scr[...] = u
        gk_ref[...] = jnp.dot(u, wgk_ref[...], preferred_element_type=F32)

    z_ref[...] = jnp.dot(u_scr[...], w_ref[...], preferred_element_type=F32).astype(BF16)


def _inproj(x2, gain, scale, shift, w_main, w_gk, seq):
    t = x2.shape[0]
    tm = min(TM_IN, seq)
    per_seq = seq // tm
    return pl.pallas_call(
        _inproj_body,
        grid=(t // tm, Z_COLS // TN_IN),
        in_specs=[pl.BlockSpec((tm, D), lambda i, j: (i, 0)),
                  pl.BlockSpec((1, D), lambda i, j: (0, 0)),
                  pl.BlockSpec((1, 1, D), lambda i, j: (i // per_seq, 0, 0)),
                  pl.BlockSpec((1, 1, D), lambda i, j: (i // per_seq, 0, 0)),
                  pl.BlockSpec((D, TN_IN), lambda i, j: (0, j)),
                  pl.BlockSpec((D, LANES), lambda i, j: (0, 0))],
        out_specs=[pl.BlockSpec((tm, TN_IN), lambda i, j: (i, j)),
                   pl.BlockSpec((tm, LANES), lambda i, j: (i, 0))],
        out_shape=[jax.ShapeDtypeStruct((t, Z_COLS), BF16),
                   jax.ShapeDtypeStruct((t, LANES), F32)],
        scratch_shapes=[pltpu.VMEM((tm, D), BF16)],
        compiler_params=_params(("arbitrary", "arbitrary")),
        name="inproj",
    )(x2, gain, scale, shift, w_main, w_gk)


def _gla_body(q_ref, k_ref, v_ref, g_ref, gk_ref, wgk_ref, bgk_ref, gn_ref, ll_ref, o_ref, st_ref, *, n_chunks):
    @pl.when(pl.program_id(1) == 0)
    def _():
        st_ref[...] = jnp.zeros_like(st_ref)

    row = lax.broadcasted_iota(I32, (CHUNK, CHUNK), 0)
    col = lax.broadcasted_iota(I32, (CHUNK, CHUNK), 1)
    causal = row >= col
    nt = (((1,), (1,)), ((), ()))
    tn = (((0,), (0,)), ((), ()))

    def chunk(ci, carry):
        r0 = pl.multiple_of(ci * CHUNK, CHUNK)
        rows = pl.ds(r0, CHUNK)
        gk = gk_ref[rows, :].astype(BF16)
        xg = jnp.dot(gk, wgk_ref[...], preferred_element_type=F32) + bgk_ref[...]
        la = (jnp.minimum(xg, 0.0) - jnp.log1p(jnp.exp(-jnp.abs(xg)))) * (1.0 / GATE_NORMALIZER)
        la_hi = la.astype(BF16)
        la_lo = (la - la_hi.astype(F32)).astype(BF16)
        ll = ll_ref[...]
        bb = (jnp.dot(ll, la_hi, preferred_element_type=F32)
              + jnp.dot(ll, la_lo, preferred_element_type=F32))
        b = bb[:CHUNK]
        bl = bb[CHUNK:]
        q = q_ref[rows, :].astype(F32)
        k = k_ref[rows, :].astype(F32)
        qe = (q * (DK ** -0.5) * jnp.exp(b)).astype(BF16)
        ke = (k * jnp.exp(-b)).astype(BF16)
        kd = (k * jnp.exp(bl - b)).astype(BF16)
        dec = jnp.exp(bl[0:1, :])
        for h in range(HEADS):
            sk = slice(h * DK, (h + 1) * DK)
            sv = slice(h * DV, (h + 1) * DV)
            att = lax.dot_general(qe[:, sk], ke[:, sk], nt, preferred_element_type=F32)
            att = jnp.where(causal, att, 0.0).astype(BF16)
            vc = v_ref[rows, sv]
            st = st_ref[h]
            o = (jnp.dot(att, vc, preferred_element_type=F32)
                 + lax.dot_general(qe[:, sk], st.astype(BF16), nt, preferred_element_type=F32))
            st_ref[h] = st * dec[:, sk] + lax.dot_general(vc, kd[:, sk], tn, preferred_element_type=F32)
            ms = jnp.mean(o * o, axis=-1, keepdims=True)
            on = o * lax.rsqrt(ms + EPS) * gn_ref[...]
            gg = g_ref[rows, sv].astype(F32)
            o_ref[rows, sv] = (on * (gg * _sigmoid(gg))).astype(BF16)
        return carry

    lax.fori_loop(0, n_chunks, chunk, 0)


def _gla(z, gk, w_gk2, b_gk2, gla_norm, bsz, seq):
    t = z.shape[0]
    tc = min(TC_GLA, seq)
    per_seq = seq // tc
    wgk = jnp.zeros((LANES, KEY_DIM), F32).at[:GATE_RANK].set(w_gk2).astype(BF16)
    r = jnp.arange(2 * CHUNK)[:, None]
    c = jnp.arange(CHUNK)[None, :]
    ll = jnp.where((r >= CHUNK) | (r >= c), 1.0, 0.0).astype(BF16)
    kb = KEY_DIM // KEY_DIM
    return pl.pallas_call(
        functools.partial(_gla_body, n_chunks=tc // CHUNK),
        grid=(bsz, per_seq),
        in_specs=[pl.BlockSpec((tc, KEY_DIM), lambda b, s: (b * per_seq + s, 0)),
                  pl.BlockSpec((tc, KEY_DIM), lambda b, s: (b * per_seq + s, kb)),
                  pl.BlockSpec((tc, VAL_DIM), lambda b, s: (b * per_seq + s, 1)),
                  pl.BlockSpec((tc, VAL_DIM), lambda b, s: (b * per_seq + s, 2)),
                  pl.BlockSpec((tc, LANES), lambda b, s: (b * per_seq + s, 0)),
                  pl.BlockSpec((LANES, KEY_DIM), lambda b, s: (0, 0)),
                  pl.BlockSpec((1, KEY_DIM), lambda b, s: (0, 0)),
                  pl.BlockSpec((1, DV), lambda b, s: (0, 0)),
                  pl.BlockSpec((2 * CHUNK, CHUNK), lambda b, s: (0, 0))],
        out_specs=pl.BlockSpec((tc, VAL_DIM), lambda b, s: (b * per_seq + s, 0)),
        out_shape=jax.ShapeDtypeStruct((t, VAL_DIM), BF16),
        scratch_shapes=[pltpu.VMEM((HEADS, DV, DK), F32)],
        compiler_params=_params(("arbitrary", "arbitrary")),
        name="gla",
    )(z, z, z, z, gk, wgk, b_gk2.reshape(1, KEY_DIM), gla_norm.reshape(1, DV), ll)


def _pool_body(p_ref, halo_ref, gb_ref, wp_ref, ps_ref, wb_ref, o_ref, *, tm, seq):
    base = (pl.program_id(0) * tm) % seq
    pf = p_ref[...].astype(F32)
    hal = halo_ref[...].astype(F32)
    hal = jnp.where(base == 0, 0.0, hal)
    ext = jnp.concatenate([hal, pf], axis=0)
    pos = base + lax.broadcasted_iota(I32, (tm, 1), 0)
    ys = []
    for gi, w in enumerate(POOL_WINDOWS):
        cols = slice(gi * POOL_GROUP_DIM, (gi + 1) * POOL_GROUP_DIM)
        s = ext[:, cols]
        sh = 1
        while sh < w:
            s = s + pltpu.roll(s, sh, 0)
            sh *= 2
        cnt = jnp.minimum(pos + 1, w).astype(F32)
        m = s[POOL_HALO:] / cnt - pf[:, cols]
        ys.append(jnp.dot(m.astype(BF16), wp_ref[gi], preferred_element_type=F32))
    y = jnp.concatenate(ys, axis=1) * ps_ref[...]
    yb = jnp.dot(y.astype(BF16), wb_ref[...], preferred_element_type=F32)
    o_ref[...] = (yb * _sigmoid(gb_ref[...].astype(F32))).astype(BF16)


def _pool(z, w_pool, pool_scale, w_b, seq):
    t = z.shape[0]
    tm = min(TM_POOL, seq)
    p_blk = (Z_COLS - POOL_DIM) // POOL_DIM
    gb_blk = (2 * KEY_DIM + 2 * VAL_DIM + D) // D
    hb = tm // POOL_HALO
    return pl.pallas_call(
        functools.partial(_pool_body, tm=tm, seq=seq),
        grid=(t // tm,),
        in_specs=[pl.BlockSpec((tm, POOL_DIM), lambda i: (i, p_blk)),
                  pl.BlockSpec((POOL_HALO, POOL_DIM), lambda i: (jnp.maximum(i * hb - 1, 0), p_blk)),
                  pl.BlockSpec((tm, D), lambda i: (i, gb_blk)),
                  pl.BlockSpec((len(POOL_WINDOWS), POOL_GROUP_DIM, POOL_GROUP_DIM), lambda i: (0, 0, 0)),
                  pl.BlockSpec((1, POOL_DIM), lambda i: (0, 0)),
                  pl.BlockSpec((POOL_DIM, D), lambda i: (0, 0))],
        out_specs=pl.BlockSpec((tm, D), lambda i: (i, 0)),
        out_shape=jax.ShapeDtypeStruct((t, D), BF16),
        compiler_params=_params(("arbitrary",)),
        name="pool",
    )(z, z, z, w_pool.astype(BF16), pool_scale.reshape(1, POOL_DIM), w_b.astype(BF16))


def _mix_body(o_ref, ga_ref, cb_ref, x_ref, g1_ref, nf_ref, sc_ref, sh_ref, wa_ref, wo_ref, wr_ref,
              h_ref, u_ref, lg_ref):
    ya = jnp.dot(o_ref[...], wa_ref[...], preferred_element_type=F32)
    merged = ya * _sigmoid(ga_ref[...].astype(F32)) + cb_ref[...].astype(F32)
    mo = jnp.dot(merged.astype(BF16), wo_ref[...], preferred_element_type=F32)
    h = x_ref[...] + g1_ref[0] * mo
    h_ref[...] = h
    ms = jnp.mean(h * h, axis=-1, keepdims=True)
    u = h * lax.rsqrt(ms + EPS) * nf_ref[...]
    u = u * (1.0 + sc_ref[0]) + sh_ref[0]
    ub = u.astype(BF16)
    u_ref[...] = ub.reshape(u_ref.shape)
    lg_ref[...] = jnp.dot(ub, wr_ref[...], preferred_element_type=F32)


def _mix(o_gla, z, cb, x2, gate1, norm_ffn, scale2, shift2, w_a, w_out, w_router, seq):
    t = x2.shape[0]
    tm = min(TM_MIX, seq)
    per_seq = seq // tm
    ga_blk = (2 * KEY_DIM + 2 * VAL_DIM) // D
    tok = lambda i: (i, 0)
    const = lambda i: (0, 0)
    per_b = lambda i: (i // per_seq, 0, 0)
    return pl.pallas_call(
        _mix_body,
        grid=(t // tm,),
        in_specs=[pl.BlockSpec((tm, VAL_DIM), tok),
                  pl.BlockSpec((tm, D), lambda i: (i, ga_blk)),
                  pl.BlockSpec((tm, D), tok),
                  pl.BlockSpec((tm, D), tok),
                  pl.BlockSpec((1, 1, D), per_b),
                  pl.BlockSpec((1, D), const),
                  pl.BlockSpec((1, 1, D), per_b),
                  pl.BlockSpec((1, 1, D), per_b),
                  pl.BlockSpec((VAL_DIM, D), const, pipeline_mode=pl.Buffered(1)),
                  pl.BlockSpec((D, D), const, pipeline_mode=pl.Buffered(1)),
                  pl.BlockSpec((D, LANES), const)],
        out_specs=[pl.BlockSpec((tm, D), tok),
                   pl.BlockSpec((tm, ROW_SUB, LANES), lambda i: (i, 0, 0)),
                   pl.BlockSpec((tm, LANES), tok)],
        out_shape=[jax.ShapeDtypeStruct((t, D), F32),
                   jax.ShapeDtypeStruct((t, ROW_SUB, LANES), BF16),
                   jax.ShapeDtypeStruct((t, LANES), F32)],
        compiler_params=_params(("arbitrary",)),
        name="mix",
    )(o_gla, z, cb, x2, gate1, norm_ffn, scale2, shift2, w_a, w_out, w_router)


def _route_topk(gl, tm):
    lane = lax.broadcasted_iota(I32, (tm, LANES), 1)
    lanef = lane.astype(F32)
    neg = float("-inf")
    big = float(LANES)
    gmask = lane < N_GROUPS
    gmax = jnp.max(jnp.where(gmask, gl, neg), axis=-1, keepdims=True)
    gsel = jnp.min(jnp.where(gmask & (gl == gmax), lanef, big), axis=-1, keepdims=True)
    gsum = jnp.sum(jnp.where(gmask, jnp.exp(gl - gmax), 0.0), axis=-1, keepdims=True)
    pg = 1.0 / gsum
    lo = N_GROUPS + gsel * EXPERTS_PER_GROUP
    emask = (lanef >= lo) & (lanef < lo + EXPERTS_PER_GROUP)
    emax = jnp.max(jnp.where(emask, gl, neg), axis=-1, keepdims=True)
    ee = jnp.where(emask, jnp.exp(gl - emax), 0.0)
    eprob = ee / jnp.sum(ee, axis=-1, keepdims=True)
    v1 = jnp.max(jnp.where(emask, eprob, -1.0), axis=-1, keepdims=True)
    i1 = jnp.min(jnp.where(emask & (eprob == v1), lanef, big), axis=-1, keepdims=True)
    m2 = emask & (lanef != i1)
    v2 = jnp.max(jnp.where(m2, eprob, -1.0), axis=-1, keepdims=True)
    i2 = jnp.min(jnp.where(m2 & (eprob == v2), lanef, big), axis=-1, keepdims=True)
    tot = v1 + v2
    w1 = pg * (v1 / tot)
    w2 = pg * (v2 / tot)
    hit1 = lanef == i1 - N_GROUPS
    hit2 = lanef == i2 - N_GROUPS
    return hit1, hit2, w1, w2


def _count_body(lg_ref, bias_ref, cnt_ref, *, tm):
    @pl.when(pl.program_id(0) == 0)
    def _():
        cnt_ref[...] = jnp.zeros_like(cnt_ref)

    hit1, hit2, _, _ = _route_topk(lg_ref[...] + bias_ref[...], tm)
    oh = jnp.where(hit1 | hit2, 1.0, 0.0)
    cnt_ref[...] += jnp.sum(oh, axis=0, keepdims=True)


def _assign_body(lg_ref, bias_ref, cnt_ref, ls_ref, ut_ref, pos_ref, wt_ref, carry_ref, pst_ref, *, tm):
    lane = lax.broadcasted_iota(I32, (tm, LANES), 1)

    @pl.when(pl.program_id(0) == 0)
    def _():
        cnt = cnt_ref[0:1, :]
        nblk = jnp.floor((cnt + (ROW_BLOCK - 1)) * (1.0 / ROW_BLOCK))
        hi = jnp.floor(nblk * (1.0 / 256.0))
        lo = nblk - 256.0 * hi
        ut = ut_ref[...]
        hi8 = jnp.broadcast_to(hi, (8, LANES)).astype(BF16)
        lo8 = jnp.broadcast_to(lo, (8, LANES)).astype(BF16)
        pre = (256.0 * jnp.dot(hi8, ut, preferred_element_type=F32)
               + jnp.dot(lo8, ut, preferred_element_type=F32))
        pst_ref[...] = pre[0:1] * float(ROW_BLOCK)
        carry_ref[...] = jnp.zeros_like(carry_ref)

    hit1, hit2, w1, w2 = _route_topk(lg_ref[...] + bias_ref[...], tm)
    oh = jnp.where(hit1 | hit2, 1.0, 0.0)
    cum = jnp.dot(ls_ref[...], oh.astype(BF16), preferred_element_type=F32)
    slot = cum + carry_ref[...] + pst_ref[...]
    carry_ref[...] += jnp.sum(oh, axis=0, keepdims=True)
    p1 = jnp.sum(jnp.where(hit1, slot, 0.0), axis=-1, keepdims=True)
    p2 = jnp.sum(jnp.where(hit2, slot, 0.0), axis=-1, keepdims=True)
    pos_ref[...] = jnp.where(lane == 0, p1, jnp.where(lane == 1, p2, 0.0)).astype(I32)
    wt_ref[...] = jnp.where(lane == 0, w1, jnp.where(lane == 1, w2, 0.0))


def _route(logits, bias):
    t = logits.shape[0]
    tm = min(TM_ROUTE, t)
    r = jnp.arange(tm)
    ls = jnp.where(r[:, None] > r[None, :], 1.0, 0.0).astype(BF16)
    e = jnp.arange(LANES)
    ut = jnp.where(e[:, None] < e[None, :], 1.0, 0.0).astype(BF16)
    tok = lambda i: (i, 0)
    const = lambda i: (0, 0)
    cnt = pl.pallas_call(
        functools.partial(_count_body, tm=tm),
        grid=(t // tm,),
        in_specs=[pl.BlockSpec((tm, LANES), tok), pl.BlockSpec((1, LANES), const)],
        out_specs=pl.BlockSpec((8, LANES), const),
        out_shape=jax.ShapeDtypeStruct((8, LANES), F32),
        compiler_params=_params(("arbitrary",)),
        name="route_count",
    )(logits, bias)
    pos, wts = pl.pallas_call(
        functools.partial(_assign_body, tm=tm),
        grid=(t // tm,),
        in_specs=[pl.BlockSpec((tm, LANES), tok),
                  pl.BlockSpec((1, LANES), const),
                  pl.BlockSpec((8, LANES), const),
                  pl.BlockSpec((tm, tm), const),
                  pl.BlockSpec((LANES, LANES), const)],
        out_specs=[pl.BlockSpec((tm, LANES), tok), pl.BlockSpec((tm, LANES), tok)],
        out_shape=[jax.ShapeDtypeStruct((t, LANES), I32), jax.ShapeDtypeStruct((t, LANES), F32)],
        scratch_shapes=[pltpu.VMEM((1, LANES), F32), pltpu.VMEM((1, LANES), F32)],
        compiler_params=_params(("arbitrary",)),
        name="route_assign",
    )(logits, bias, cnt, ls, ut)
    return pos, wts, cnt


def _row_copy(src, s, dst, d, sem):
    return pltpu.make_async_copy(src.at[s], dst.at[d], sem)


def _dispatch_body(pst_ref, pad_ref, nu_ref, pos_ref, u_ref, zero_ref, xs_ref, sem, zsem, *, tm, n_blocks):
    i = pl.program_id(0)

    @pl.when(i == 0)
    def _():
        def zero_block(start):
            cp = pltpu.make_async_copy(zero_ref, xs_ref.at[pl.ds(start, ROW_BLOCK)], zsem)
            cp.start()
            cp.wait()

        def zero_tail(e, carry):
            @pl.when(pad_ref[e] > 0)
            def _():
                zero_block(pst_ref[e] + pad_ref[e] - ROW_BLOCK)
            return carry
        lax.fori_loop(0, N_EXPERTS, zero_tail, 0)

        def zero_unused(b, carry):
            zero_block(b * ROW_BLOCK)
            return carry
        lax.fori_loop(nu_ref[0], n_blocks, zero_unused, 0)

    def issue(r, carry):
        for k in range(2):
            _row_copy(u_ref, i * tm + r, xs_ref, pos_ref[0, 0, 2 * r + k], sem).start()
        return carry
    lax.fori_loop(0, tm, issue, 0)

    def drain(r, carry):
        for k in range(2):
            _row_copy(u_ref, 0, xs_ref, 0, sem).wait()
        return carry
    lax.fori_loop(0, tm, drain, 0)


def _dispatch(u2, pos2, pstart, padded, n_used, n_rows):
    t = u2.shape[0]
    tm = min(TM_ROWS, t)
    zero = jnp.zeros((ROW_BLOCK, ROW_SUB, LANES), BF16)
    grid_spec = pltpu.PrefetchScalarGridSpec(
        num_scalar_prefetch=3,
        grid=(t // tm,),
        in_specs=[pl.BlockSpec((1, 1, 2 * tm), lambda i, a, b, c: (i, 0, 0), memory_space=pltpu.SMEM),
                  pl.BlockSpec(memory_space=pl.ANY),
                  pl.BlockSpec(memory_space=pl.ANY)],
        out_specs=pl.BlockSpec(memory_space=pl.ANY),
        scratch_shapes=[pltpu.SemaphoreType.DMA(()), pltpu.SemaphoreType.DMA(())],
    )
    return pl.pallas_call(
        functools.partial(_dispatch_body, tm=tm, n_blocks=n_rows // ROW_BLOCK),
        grid_spec=grid_spec,
        out_shape=jax.ShapeDtypeStruct((n_rows, ROW_SUB, LANES), BF16),
        compiler_params=_params(("arbitrary",)),
        name="dispatch",
    )(pstart, padded, n_used, pos2.reshape(t // tm, 1, 2 * tm), u2, zero)


def _expert_body(be_ref, nu_ref, x_ref, wg_ref, wu_ref, wd_ref, y_ref):
    @pl.when(pl.program_id(0) < nu_ref[0])
    def _():
        x = x_ref[...].reshape(ROW_BLOCK, D)
        a = jnp.dot(x, wg_ref[0], preferred_element_type=F32)
        b = jnp.dot(x, wu_ref[0], preferred_element_type=F32)
        hdn = (a * _sigmoid(a) * b).astype(BF16)
        y = jnp.dot(hdn, wd_ref[0], preferred_element_type=F32)
        y_ref[...] = y.astype(BF16).reshape(y_ref.shape)

    @pl.when(pl.program_id(0) >= nu_ref[0])
    def _():
        y_ref[...] = jnp.zeros_like(y_ref)


def _experts(xs, blk_e, n_used, w_gate, w_up, w_down):
    n_rows = xs.shape[0]
    nb = n_rows // ROW_BLOCK
    rows = lambda i, be, nu: (i, 0, 0)
    wsel = lambda i, be, nu: (be[i], 0, 0)
    grid_spec = pltpu.PrefetchScalarGridSpec(
        num_scalar_prefetch=2,
        grid=(nb,),
        in_specs=[pl.BlockSpec((ROW_BLOCK, ROW_SUB, LANES), rows),
                  pl.BlockSpec((1, D, EXPERT_HIDDEN), wsel),
                  pl.BlockSpec((1, D, EXPERT_HIDDEN), wsel),
                  pl.BlockSpec((1, EXPERT_HIDDEN, D), wsel)],
        out_specs=pl.BlockSpec((ROW_BLOCK, ROW_SUB, LANES), rows),
    )
    return pl.pallas_call(
        _expert_body,
        grid_spec=grid_spec,
        out_shape=jax.ShapeDtypeStruct((n_rows, ROW_SUB, LANES), BF16),
        compiler_params=_params(("arbitrary",)),
        name="experts",
    )(blk_e, n_used, xs, w_gate, w_up, w_down)


def _combine_body(pos_ref, yb_ref, wt_ref, h_ref, g2_ref, nf_ref, o_ref, ga_ref, gb_ref, sem, *, tm):
    def issue(r, carry):
        _row_copy(yb_ref, pos_ref[0, 0, 2 * r], ga_ref, r, sem).start()
        _row_copy(yb_ref, pos_ref[0, 0, 2 * r + 1], gb_ref, r, sem).start()
        return carry
    lax.fori_loop(0, tm, issue, 0)

    def drain(r, carry):
        _row_copy(yb_ref, 0, ga_ref, 0, sem).wait()
        _row_copy(yb_ref, 0, gb_ref, 0, sem).wait()
        return carry
    lax.fori_loop(0, tm, drain, 0)

    wt = wt_ref[...]
    ya = ga_ref[...].reshape(tm, D).astype(F32)
    yb = gb_ref[...].reshape(tm, D).astype(F32)
    y = wt[:, 0:1] * ya + wt[:, 1:2] * yb
    h = h_ref[...] + g2_ref[0] * y
    ms = jnp.mean(h * h, axis=-1, keepdims=True)
    o_ref[...] = h * lax.rsqrt(ms + EPS) * nf_ref[...]


def _combine(pos2, yb, wts, h1, gate2, norm_final, seq):
    t = h1.shape[0]
    tm = min(TM_ROWS, seq)
    per_seq = seq // tm
    tok = lambda i: (i, 0)
    return pl.pallas_call(
        functools.partial(_combine_body, tm=tm),
        grid=(t // tm,),
        in_specs=[pl.BlockSpec((1, 1, 2 * tm), lambda i: (i, 0, 0), memory_space=pltpu.SMEM),
                  pl.BlockSpec(memory_space=pl.ANY),
                  pl.BlockSpec((tm, LANES), tok),
                  pl.BlockSpec((tm, D), tok),
                  pl.BlockSpec((1, 1, D), lambda i: (i // per_seq, 0, 0)),
                  pl.BlockSpec((1, D), lambda i: (0, 0))],
        out_specs=pl.BlockSpec((tm, D), tok),
        out_shape=jax.ShapeDtypeStruct((t, D), F32),
        scratch_shapes=[pltpu.VMEM((tm, ROW_SUB, LANES), BF16), pltpu.VMEM((tm, ROW_SUB, LANES), BF16),
                        pltpu.SemaphoreType.DMA(())],
        compiler_params=_params(("arbitrary",)),
        name="combine",
    )(pos2.reshape(t // tm, 1, 2 * tm), yb, wts, h1, gate2, norm_final)


def _layer(h2d, c, bsz, seq, w_ada, b_ada, norm_mix, w_in, w_gk2, b_gk2, gla_norm, w_a, w_pool, pool_scale, w_b,
           w_out, norm_ffn, w_rg, b_rg, w_re, b_re, w_gate, w_up, w_down):
    t = bsz * seq
    mod = _ada(c, w_ada, b_ada)
    shift1, scale1, gate1, shift2, scale2, gate2 = [m.reshape(bsz, 1, D) for m in jnp.split(mod, 6, axis=-1)]

    c_gk = 2 * KEY_DIM + 2 * VAL_DIM
    c_p = c_gk + GATE_RANK
    c_ga = c_p + POOL_DIM
    w_main = jnp.concatenate([w_in[:, :c_gk], w_in[:, c_ga:], w_in[:, c_p:c_ga]], axis=1).astype(BF16)
    w_gk = jnp.zeros((D, LANES), F32).at[:, :GATE_RANK].set(w_in[:, c_gk:c_p]).astype(BF16)

    z, gk = _inproj(h2d, norm_mix.reshape(1, D), scale1, shift1, w_main, w_gk, seq)
    o_gla = _gla(z, gk, w_gk2, b_gk2, gla_norm, bsz, seq)
    cb = _pool(z, w_pool, pool_scale, w_b, seq)

    w_router = jnp.zeros((D, LANES), F32).at[:, :N_GROUPS].set(w_rg)
    w_router = w_router.at[:, N_GROUPS:N_GROUPS + N_EXPERTS].set(w_re).astype(BF16)
    h1, u2, logits = _mix(o_gla, z, cb, h2d, gate1, norm_ffn.reshape(1, D), scale2, shift2,
                          w_a.astype(BF16), w_out.astype(BF16), w_router, seq)

    bias = jnp.zeros((1, LANES), F32).at[0, :N_GROUPS].set(b_rg).at[0, N_GROUPS:N_GROUPS + N_EXPERTS].set(b_re)
    pos, wts, cnt = _route(logits, bias)
    pos2 = pos[:, :2]

    counts = cnt[0, :N_EXPERTS].astype(I32)
    nblk = (counts + ROW_BLOCK - 1) // ROW_BLOCK
    bends = jnp.cumsum(nblk)
    pstart = ((bends - nblk) * ROW_BLOCK).astype(I32)
    padded = (nblk * ROW_BLOCK).astype(I32)
    n_rows = 2 * t + N_EXPERTS * ROW_BLOCK
    n_blocks = n_rows // ROW_BLOCK
    blk_e = jnp.clip(jnp.searchsorted(bends, jnp.arange(n_blocks, dtype=I32), side="right"),
                     0, N_EXPERTS - 1).astype(I32)
    n_used = bends[-1:].astype(I32)

    xs = _dispatch(u2, pos2, pstart, padded, n_used, n_rows)
    yb = _experts(xs, blk_e, n_used, w_gate.astype(BF16), w_up.astype(BF16), w_down.astype(BF16))
    return pos2, yb, wts, h1, gate2


def kernel(x, c, w_ada, b_ada, norm_mix, w_in, w_gk2, b_gk2, gla_norm, w_a, w_pool, pool_scale, w_b, w_out, norm_ffn, w_rg, b_rg, w_re, b_re, w_gate, w_up, w_down, norm_final):
    bsz, seq, _ = x.shape
    depth = w_ada.shape[0]
    assert depth == 1
    h2d = x.reshape(bsz * seq, D)
    pos2, yb, wts, h1, gate2 = _layer(
        h2d, c, bsz, seq, w_ada[0], b_ada[0], norm_mix[0], w_in[0], w_gk2[0], b_gk2[0], gla_norm[0], w_a[0],
        w_pool[0], pool_scale[0], w_b[0], w_out[0], norm_ffn[0], w_rg[0], b_rg[0], w_re[0], b_re[0],
        w_gate[0], w_up[0], w_down[0])
    out = _combine(pos2, yb, wts, h1, gate2, norm_final.reshape(1, D), seq)
    return out.reshape(bsz, seq, D)
```

```python
import functools

import jax
import jax.numpy as jnp
from jax import lax
from jax.experimental import pallas as pl
from jax.experimental.pallas import tpu as pltpu

F32 = jnp.float32
BF16 = jnp.bfloat16
I32 = jnp.int32

D = 2048
HEADS = 4
DK = 256
DV = 512
KEY_DIM = HEADS * DK
VAL_DIM = HEADS * DV
GATE_RANK = 16
GATE_NORMALIZER = 16.0
CHUNK = 64
POOL_WINDOWS = (2, 4, 8, 16)
POOL_DIM = 1024
POOL_GROUP_DIM = 256
POOL_HALO = 16
N_GROUPS = 8
EXPERTS_PER_GROUP = 8
N_EXPERTS = 64
EXPERT_HIDDEN = 1024
EPS = 1e-6
LANES = 128
ROW_SUB = D // LANES

Z_COLS = 2 * KEY_DIM + 2 * VAL_DIM + 2 * D + POOL_DIM

VMEM_LIMIT = 56 * 1024 * 1024

TM_IN = 512
TN_IN = 1024
TC_GLA = 512
TM_POOL = 256
TM_MIX = 256
TM_ROUTE = 512
TM_ROWS = 256
ROW_BLOCK = 256


def _sigmoid(x):
    return 1.0 / (1.0 + jnp.exp(-x))


def _params(sem, vmem=VMEM_LIMIT):
    return pltpu.CompilerParams(dimension_semantics=sem, vmem_limit_bytes=vmem)


def _ada_body(c_ref, w_ref, b_ref, o_ref):
    c = c_ref[...]
    s = (c * _sigmoid(c)).astype(BF16)
    o_ref[...] = jnp.dot(s, w_ref[...].astype(BF16), preferred_element_type=F32) + b_ref[...]


def _ada(c, w, b):
    bsz = c.shape[0]
    cp = jnp.zeros((8, D), F32).at[:bsz].set(c)
    n = w.shape[1]
    tn = 1024
    out = pl.pallas_call(
        _ada_body,
        grid=(n // tn,),
        in_specs=[pl.BlockSpec((8, D), lambda j: (0, 0)),
                  pl.BlockSpec((D, tn), lambda j: (0, j)),
                  pl.BlockSpec((1, tn), lambda j: (0, j))],
        out_specs=pl.BlockSpec((8, tn), lambda j: (0, j)),
        out_shape=jax.ShapeDtypeStruct((8, n), F32),
        compiler_params=_params(("arbitrary",)),
        name="ada",
    )(cp, w, b.reshape(1, n))
    return out[:bsz]


def _inproj_body(x_ref, g_ref, sc_ref, sh_ref, w_ref, wgk_ref, z_ref, gk_ref, u_scr):
    @pl.when(pl.program_id(1) == 0)
    def _():
        x = x_ref[...]
        ms = jnp.mean(x * x, axis=-1, keepdims=True)
        u = x * lax.rsqrt(ms + EPS) * g_ref[...]
        u = (u * (1.0 + sc_ref[0]) + sh_ref[0]).astype(BF16)
        u_scr[...] = u
        gk_ref[...] = jnp.dot(u, wgk_ref[...], preferred_element_type=F32)

    z_ref[...] = jnp.dot(u_scr[...], w_ref[...], preferred_element_type=F32).astype(BF16)


def _inproj(x2, gain, scale, shift, w_main, w_gk, seq):
    t = x2.shape[0]
    tm = min(TM_IN, seq)
    per_seq = seq // tm
    return pl.pallas_call(
        _inproj_body,
        grid=(t // tm, Z_COLS // TN_IN),
        in_specs=[pl.BlockSpec((tm, D), lambda i, j: (i, 0)),
                  pl.BlockSpec((1, D), lambda i, j: (0, 0)),
                  pl.BlockSpec((1, 1, D), lambda i, j: (i // per_seq, 0, 0)),
                  pl.BlockSpec((1, 1, D), lambda i, j: (i // per_seq, 0, 0)),
                  pl.BlockSpec((D, TN_IN), lambda i, j: (0, j)),
                  pl.BlockSpec((D, LANES), lambda i, j: (0, 0))],
        out_specs=[pl.BlockSpec((tm, TN_IN), lambda i, j: (i, j)),
                   pl.BlockSpec((tm, LANES), lambda i, j: (i, 0))],
        out_shape=[jax.ShapeDtypeStruct((t, Z_COLS), BF16),
                   jax.ShapeDtypeStruct((t, LANES), F32)],
        scratch_shapes=[pltpu.VMEM((tm, D), BF16)],
        compiler_params=_params(("arbitrary", "arbitrary")),
        name="inproj",
    )(x2, gain, scale, shift, w_main, w_gk)


def _gla_body(q_ref, k_ref, v_ref, g_ref, gk_ref, wgk_ref, bgk_ref, gn_ref, ll_ref, o_ref, st_ref, *, n_chunks):
    @pl.when(pl.program_id(1) == 0)
    def _():
        st_ref[...] = jnp.zeros_like(st_ref)

    row = lax.broadcasted_iota(I32, (CHUNK, CHUNK), 0)
    col = lax.broadcasted_iota(I32, (CHUNK, CHUNK), 1)
    causal = row >= col
    nt = (((1,), (1,)), ((), ()))
    tn = (((0,), (0,)), ((), ()))

    def chunk(ci, carry):
        r0 = pl.multiple_of(ci * CHUNK, CHUNK)
        rows = pl.ds(r0, CHUNK)
        gk = gk_ref[rows, :].astype(BF16)
        xg = jnp.dot(gk, wgk_ref[...], preferred_element_type=F32) + bgk_ref[...]
        la = (jnp.minimum(xg, 0.0) - jnp.log1p(jnp.exp(-jnp.abs(xg)))) * (1.0 / GATE_NORMALIZER)
        la_hi = la.astype(BF16)
        la_lo = (la - la_hi.astype(F32)).astype(BF16)
        ll = ll_ref[...]
        bb = (jnp.dot(ll, la_hi, preferred_element_type=F32)
              + jnp.dot(ll, la_lo, preferred_element_type=F32))
        b = bb[:CHUNK]
        bl = bb[CHUNK:]
        q = q_ref[rows, :].astype(F32)
        k = k_ref[rows, :].astype(F32)
        qe = (q * (DK ** -0.5) * jnp.exp(b)).astype(BF16)
        ke = (k * jnp.exp(-b)).astype(BF16)
        kd = (k * jnp.exp(bl - b)).astype(BF16)
        dec = jnp.exp(bl[0:1, :])
        for h in range(HEADS):
            sk = slice(h * DK, (h + 1) * DK)
            sv = slice(h * DV, (h + 1) * DV)
            att = lax.dot_general(qe[:, sk], ke[:, sk], nt, preferred_element_type=F32)
            att = jnp.where(causal, att, 0.0).astype(BF16)
            vc = v_ref[rows, sv]
            st = st_ref[h]
            o = (jnp.dot(att, vc, preferred_element_type=F32)
                 + lax.dot_general(qe[:, sk], st.astype(BF16), nt, preferred_element_type=F32))
            st_ref[h] = st * dec[:, sk] + lax.dot_general(vc, kd[:, sk], tn, preferred_element_type=F32)
            ms = jnp.mean(o * o, axis=-1, keepdims=True)
            on = o * lax.rsqrt(ms + EPS) * gn_ref[...]
            gg = g_ref[rows, sv].astype(F32)
            o_ref[rows, sv] = (on * (gg * _sigmoid(gg))).astype(BF16)
        return carry

    lax.fori_loop(0, n_chunks, chunk, 0)


def _gla(z, gk, w_gk2, b_gk2, gla_norm, bsz, seq):
    t = z.shape[0]
    tc = min(TC_GLA, seq)
    per_seq = seq // tc
    wgk = jnp.zeros((LANES, KEY_DIM), F32).at[:GATE_RANK].set(w_gk2).astype(BF16)
    r = jnp.arange(2 * CHUNK)[:, None]
    c = jnp.arange(CHUNK)[None, :]
    ll = jnp.where((r >= CHUNK) | (r >= c), 1.0, 0.0).astype(BF16)
    kb = KEY_DIM // KEY_DIM
    return pl.pallas_call(
        functools.partial(_gla_body, n_chunks=tc // CHUNK),
        grid=(bsz, per_seq),
        in_specs=[pl.BlockSpec((tc, KEY_DIM), lambda b, s: (b * per_seq + s, 0)),
                  pl.BlockSpec((tc, KEY_DIM), lambda b, s: (b * per_seq + s, kb)),
                  pl.BlockSpec((tc, VAL_DIM), lambda b, s: (b * per_seq + s, 1)),
                  pl.BlockSpec((tc, VAL_DIM), lambda b, s: (b * per_seq + s, 2)),
                  pl.BlockSpec((tc, LANES), lambda b, s: (b * per_seq + s, 0)),
                  pl.BlockSpec((LANES, KEY_DIM), lambda b, s: (0, 0)),
                  pl.BlockSpec((1, KEY_DIM), lambda b, s: (0, 0)),
                  pl.BlockSpec((1, DV), lambda b, s: (0, 0)),
                  pl.BlockSpec((2 * CHUNK, CHUNK), lambda b, s: (0, 0))],
        out_specs=pl.BlockSpec((tc, VAL_DIM), lambda b, s: (b * per_seq + s, 0)),
        out_shape=jax.ShapeDtypeStruct((t, VAL_DIM), BF16),
        scratch_shapes=[pltpu.VMEM((HEADS, DV, DK), F32)],
        compiler_params=_params(("arbitrary", "arbitrary")),
        name="gla",
    )(z, z, z, z, gk, wgk, b_gk2.reshape(1, KEY_DIM), gla_norm.reshape(1, DV), ll)


def _pool_body(p_ref, halo_ref, gb_ref, wp_ref, ps_ref, wb_ref, o_ref, *, tm, seq):
    base = (pl.program_id(0) * tm) % seq
    pf = p_ref[...].astype(F32)
    hal = halo_ref[...].astype(F32)
    hal = jnp.where(base == 0, 0.0, hal)
    ext = jnp.concatenate([hal, pf], axis=0)
    pos = base + lax.broadcasted_iota(I32, (tm, 1), 0)
    ys = []
    for gi, w in enumerate(POOL_WINDOWS):
        cols = slice(gi * POOL_GROUP_DIM, (gi + 1) * POOL_GROUP_DIM)
        s = ext[:, cols]
        sh = 1
        while sh < w:
            s = s + pltpu.roll(s, sh, 0)
            sh *= 2
        cnt = jnp.minimum(pos + 1, w).astype(F32)
        m = s[POOL_HALO:] / cnt - pf[:, cols]
        ys.append(jnp.dot(m.astype(BF16), wp_ref[gi], preferred_element_type=F32))
    y = jnp.concatenate(ys, axis=1) * ps_ref[...]
    yb = jnp.dot(y.astype(BF16), wb_ref[...], preferred_element_type=F32)
    o_ref[...] = (yb * _sigmoid(gb_ref[...].astype(F32))).astype(BF16)


def _pool(z, w_pool, pool_scale, w_b, seq):
    t = z.shape[0]
    tm = min(TM_POOL, seq)
    p_blk = (Z_COLS - POOL_DIM) // POOL_DIM
    gb_blk = (2 * KEY_DIM + 2 * VAL_DIM + D) // D
    hb = tm // POOL_HALO
    return pl.pallas_call(
        functools.partial(_pool_body, tm=tm, seq=seq),
        grid=(t // tm,),
        in_specs=[pl.BlockSpec((tm, POOL_DIM), lambda i: (i, p_blk)),
                  pl.BlockSpec((POOL_HALO, POOL_DIM), lambda i: (jnp.maximum(i * hb - 1, 0), p_blk)),
                  pl.BlockSpec((tm, D), lambda i: (i, gb_blk)),
                  pl.BlockSpec((len(POOL_WINDOWS), POOL_GROUP_DIM, POOL_GROUP_DIM), lambda i: (0, 0, 0)),
                  pl.BlockSpec((1, POOL_DIM), lambda i: (0, 0)),
                  pl.BlockSpec((POOL_DIM, D), lambda i: (0, 0))],
        out_specs=pl.BlockSpec((tm, D), lambda i: (i, 0)),
        out_shape=jax.ShapeDtypeStruct((t, D), BF16),
        compiler_params=_params(("arbitrary",)),
        name="pool",
    )(z, z, z, w_pool.astype(BF16), pool_scale.reshape(1, POOL_DIM), w_b.astype(BF16))


def _mix_body(o_ref, ga_ref, cb_ref, x_ref, g1_ref, nf_ref, sc_ref, sh_ref, wa_ref, wo_ref, wr_ref,
              h_ref, u_ref, lg_ref):
    ya = jnp.dot(o_ref[...], wa_ref[...], preferred_element_type=F32)
    merged = ya * _sigmoid(ga_ref[...].astype(F32)) + cb_ref[...].astype(F32)
    mo = jnp.dot(merged.astype(BF16), wo_ref[...], preferred_element_type=F32)
    h = x_ref[...] + g1_ref[0] * mo
    h_ref[...] = h
    ms = jnp.mean(h * h, axis=-1, keepdims=True)
    u = h * lax.rsqrt(ms + EPS) * nf_ref[...]
    u = u * (1.0 + sc_ref[0]) + sh_ref[0]
    ub = u.astype(BF16)
    u_ref[...] = ub.reshape(u_ref.shape)
    lg_ref[...] = jnp.dot(ub, wr_ref[...], preferred_element_type=F32)


def _mix(o_gla, z, cb, x2, gate1, norm_ffn, scale2, shift2, w_a, w_out, w_router, seq):
    t = x2.shape[0]
    tm = min(TM_MIX, seq)
    per_seq = seq // tm
    ga_blk = (2 * KEY_DIM + 2 * VAL_DIM) // D
    tok = lambda i: (i, 0)
    const = lambda i: (0, 0)
    per_b = lambda i: (i // per_seq, 0, 0)
    return pl.pallas_call(
        _mix_body,
        grid=(t // tm,),
        in_specs=[pl.BlockSpec((tm, VAL_DIM), tok),
                  pl.BlockSpec((tm, D), lambda i: (i, ga_blk)),
                  pl.BlockSpec((tm, D), tok),
                  pl.BlockSpec((tm, D), tok),
                  pl.BlockSpec((1, 1, D), per_b),
                  pl.BlockSpec((1, D), const),
                  pl.BlockSpec((1, 1, D), per_b),
                  pl.BlockSpec((1, 1, D), per_b),
                  pl.BlockSpec((VAL_DIM, D), const, pipeline_mode=pl.Buffered(1)),
                  pl.BlockSpec((D, D), const, pipeline_mode=pl.Buffered(1)),
                  pl.BlockSpec((D, LANES), const)],
        out_specs=[pl.BlockSpec((tm, D), tok),
                   pl.BlockSpec((tm, ROW_SUB, LANES), lambda i: (i, 0, 0)),
                   pl.BlockSpec((tm, LANES), tok)],
        out_shape=[jax.ShapeDtypeStruct((t, D), F32),
                   jax.ShapeDtypeStruct((t, ROW_SUB, LANES), BF16),
                   jax.ShapeDtypeStruct((t, LANES), F32)],
        compiler_params=_params(("arbitrary",)),
        name="mix",
    )(o_gla, z, cb, x2, gate1, norm_ffn, scale2, shift2, w_a, w_out, w_router)


def _route_topk(gl, tm):
    lane = lax.broadcasted_iota(I32, (tm, LANES), 1)
    lanef = lane.astype(F32)
    neg = float("-inf")
    big = float(LANES)
    gmask = lane < N_GROUPS
    gmax = jnp.max(jnp.where(gmask, gl, neg), axis=-1, keepdims=True)
    gsel = jnp.min(jnp.where(gmask & (gl == gmax), lanef, big), axis=-1, keepdims=True)
    gsum = jnp.sum(jnp.where(gmask, jnp.exp(gl - gmax), 0.0), axis=-1, keepdims=True)
    pg = 1.0 / gsum
    lo = N_GROUPS + gsel * EXPERTS_PER_GROUP
    emask = (lanef >= lo) & (lanef < lo + EXPERTS_PER_GROUP)
    emax = jnp.max(jnp.where(emask, gl, neg), axis=-1, keepdims=True)
    ee = jnp.where(emask, jnp.exp(gl - emax), 0.0)
    eprob = ee / jnp.sum(ee, axis=-1, keepdims=True)
    v1 = jnp.max(jnp.where(emask, eprob, -1.0), axis=-1, keepdims=True)
    i1 = jnp.min(jnp.where(emask & (eprob == v1), lanef, big), axis=-1, keepdims=True)
    m2 = emask & (lanef != i1)
    v2 = jnp.max(jnp.where(m2, eprob, -1.0), axis=-1, keepdims=True)
    i2 = jnp.min(jnp.where(m2 & (eprob == v2), lanef, big), axis=-1, keepdims=True)
    tot = v1 + v2
    w1 = pg * (v1 / tot)
    w2 = pg * (v2 / tot)
    hit1 = lanef == i1 - N_GROUPS
    hit2 = lanef == i2 - N_GROUPS
    return hit1, hit2, w1, w2


def _count_body(lg_ref, bias_ref, cnt_ref, *, tm):
    @pl.when(pl.program_id(0) == 0)
    def _():
        cnt_ref[...] = jnp.zeros_like(cnt_ref)

    hit1, hit2, _, _ = _route_topk(lg_ref[...] + bias_ref[...], tm)
    oh = jnp.where(hit1 | hit2, 1.0, 0.0)
    cnt_ref[...] += jnp.sum(oh, axis=0, keepdims=True)


def _assign_body(lg_ref, bias_ref, cnt_ref, ls_ref, ut_ref, pos_ref, wt_ref, carry_ref, pst_ref, *, tm):
    lane = lax.broadcasted_iota(I32, (tm, LANES), 1)

    @pl.when(pl.program_id(0) == 0)
    def _():
        cnt = cnt_ref[0:1, :]
        nblk = jnp.floor((cnt + (ROW_BLOCK - 1)) * (1.0 / ROW_BLOCK))
        hi = jnp.floor(nblk * (1.0 / 256.0))
        lo = nblk - 256.0 * hi
        ut = ut_ref[...]
        hi8 = jnp.broadcast_to(hi, (8, LANES)).astype(BF16)
        lo8 = jnp.broadcast_to(lo, (8, LANES)).astype(BF16)
        pre = (256.0 * jnp.dot(hi8, ut, preferred_element_type=F32)
               + jnp.dot(lo8, ut, preferred_element_type=F32))
        pst_ref[...] = pre[0:1] * float(ROW_BLOCK)
        carry_ref[...] = jnp.zeros_like(carry_ref)

    hit1, hit2, w1, w2 = _route_topk(lg_ref[...] + bias_ref[...], tm)
    oh = jnp.where(hit1 | hit2, 1.0, 0.0)
    cum = jnp.dot(ls_ref[...], oh.astype(BF16), preferred_element_type=F32)
    slot = cum + carry_ref[...] + pst_ref[...]
    carry_ref[...] += jnp.sum(oh, axis=0, keepdims=True)
    p1 = jnp.sum(jnp.where(hit1, slot, 0.0), axis=-1, keepdims=True)
    p2 = jnp.sum(jnp.where(hit2, slot, 0.0), axis=-1, keepdims=True)
    pos_ref[...] = jnp.where(lane == 0, p1, jnp.where(lane == 1, p2, 0.0)).astype(I32)
    wt_ref[...] = jnp.where(lane == 0, w1, jnp.where(lane == 1, w2, 0.0))


def _route(logits, bias):
    t = logits.shape[0]
    tm = min(TM_ROUTE, t)
    r = jnp.arange(tm)
    ls = jnp.where(r[:, None] > r[None, :], 1.0, 0.0).astype(BF16)
    e = jnp.arange(LANES)
    ut = jnp.where(e[:, None] < e[None, :], 1.0, 0.0).astype(BF16)
    tok = lambda i: (i, 0)
    const = lambda i: (0, 0)
    cnt = pl.pallas_call(
        functools.partial(_count_body, tm=tm),
        grid=(t // tm,),
        in_specs=[pl.BlockSpec((tm, LANES), tok), pl.BlockSpec((1, LANES), const)],
        out_specs=pl.BlockSpec((8, LANES), const),
        out_shape=jax.ShapeDtypeStruct((8, LANES), F32),
        compiler_params=_params(("arbitrary",)),
        name="route_count",
    )(logits, bias)
    pos, wts = pl.pallas_call(
        functools.partial(_assign_body, tm=tm),
        grid=(t // tm,),
        in_specs=[pl.BlockSpec((tm, LANES), tok),
                  pl.BlockSpec((1, LANES), const),
                  pl.BlockSpec((8, LANES), const),
                  pl.BlockSpec((tm, tm), const),
                  pl.BlockSpec((LANES, LANES), const)],
        out_specs=[pl.BlockSpec((tm, LANES), tok), pl.BlockSpec((tm, LANES), tok)],
        out_shape=[jax.ShapeDtypeStruct((t, LANES), I32), jax.ShapeDtypeStruct((t, LANES), F32)],
        scratch_shapes=[pltpu.VMEM((1, LANES), F32), pltpu.VMEM((1, LANES), F32)],
        compiler_params=_params(("arbitrary",)),
        name="route_assign",
    )(logits, bias, cnt, ls, ut)
    return pos, wts, cnt


def _row_copy(src, s, dst, d, sem):
    return pltpu.make_async_copy(src.at[s], dst.at[d], sem)


def _dispatch_body(pst_ref, pad_ref, nu_ref, pos_ref, u_ref, zero_ref, xs_ref, sem, zsem, *, tm, n_blocks):
    i = pl.program_id(0)

    @pl.when(i == 0)
    def _():
        def zero_block(start):
            cp = pltpu.make_async_copy(zero_ref, xs_ref.at[pl.ds(start, ROW_BLOCK)], zsem)
            cp.start()
            cp.wait()

        def zero_tail(e, carry):
            @pl.when(pad_ref[e] > 0)
            def _():
                zero_block(pst_ref[e] + pad_ref[e] - ROW_BLOCK)
            return carry
        lax.fori_loop(0, N_EXPERTS, zero_tail, 0)

        def zero_unused(b, carry):
            zero_block(b * ROW_BLOCK)
            return carry
        lax.fori_loop(nu_ref[0], n_blocks, zero_unused, 0)

    def issue(r, carry):
        for k in range(2):
            _row_copy(u_ref, r, xs_ref, pos_ref[0, 0, 2 * r + k], sem).start()
        return carry
    lax.fori_loop(0, tm, issue, 0)

    def drain(r, carry):
        for k in range(2):
            _row_copy(u_ref, 0, xs_ref, 0, sem).wait()
        return carry
    lax.fori_loop(0, tm, drain, 0)


def _dispatch(u2, pos2, pstart, padded, n_used, n_rows):
    t = u2.shape[0]
    tm = min(TM_ROWS, t)
    zero = jnp.zeros((ROW_BLOCK, ROW_SUB, LANES), BF16)
    grid_spec = pltpu.PrefetchScalarGridSpec(
        num_scalar_prefetch=3,
        grid=(t // tm,),
        in_specs=[pl.BlockSpec((1, 1, 2 * tm), lambda i, a, b, c: (i, 0, 0), memory_space=pltpu.SMEM),
                  pl.BlockSpec((tm, ROW_SUB, LANES), lambda i, a, b, c: (i, 0, 0)),
                  pl.BlockSpec(memory_space=pl.ANY)],
        out_specs=pl.BlockSpec(memory_space=pl.ANY),
        scratch_shapes=[pltpu.SemaphoreType.DMA(()), pltpu.SemaphoreType.DMA(())],
    )
    return pl.pallas_call(
        functools.partial(_dispatch_body, tm=tm, n_blocks=n_rows // ROW_BLOCK),
        grid_spec=grid_spec,
        out_shape=jax.ShapeDtypeStruct((n_rows, ROW_SUB, LANES), BF16),
        compiler_params=_params(("arbitrary",)),
        name="dispatch",
    )(pstart, padded, n_used, pos2.reshape(t // tm, 1, 2 * tm), u2, zero)


def _expert_body(be_ref, nu_ref, x_ref, wg_ref, wu_ref, wd_ref, y_ref):
    @pl.when(pl.program_id(0) < nu_ref[0])
    def _():
        x = x_ref[...].reshape(ROW_BLOCK, D)
        a = jnp.dot(x, wg_ref[0], preferred_element_type=F32)
        b = jnp.dot(x, wu_ref[0], preferred_element_type=F32)
        hdn = (a * _sigmoid(a) * b).astype(BF16)
        y = jnp.dot(hdn, wd_ref[0], preferred_element_type=F32)
        y_ref[...] = y.astype(BF16).reshape(y_ref.shape)

    @pl.when(pl.program_id(0) >= nu_ref[0])
    def _():
        y_ref[...] = jnp.zeros_like(y_ref)


def _experts(xs, blk_e, n_used, w_gate, w_up, w_down):
    n_rows = xs.shape[0]
    nb = n_rows // ROW_BLOCK
    rows = lambda i, be, nu: (i, 0, 0)
    wsel = lambda i, be, nu: (be[i], 0, 0)
    grid_spec = pltpu.PrefetchScalarGridSpec(
        num_scalar_prefetch=2,
        grid=(nb,),
        in_specs=[pl.BlockSpec((ROW_BLOCK, ROW_SUB, LANES), rows),
                  pl.BlockSpec((1, D, EXPERT_HIDDEN), wsel),
                  pl.BlockSpec((1, D, EXPERT_HIDDEN), wsel),
                  pl.BlockSpec((1, EXPERT_HIDDEN, D), wsel)],
        out_specs=pl.BlockSpec((ROW_BLOCK, ROW_SUB, LANES), rows),
    )
    return pl.pallas_call(
        _expert_body,
        grid_spec=grid_spec,
        out_shape=jax.ShapeDtypeStruct((n_rows, ROW_SUB, LANES), BF16),
        compiler_params=_params(("arbitrary",)),
        name="experts",
    )(blk_e, n_used, xs, w_gate, w_up, w_down)


def _combine_body(pos_ref, yb_ref, wt_ref, h_ref, g2_ref, nf_ref, o_ref, ga_ref, gb_ref, sem, *, tm):
    def issue(r, carry):
        _row_copy(yb_ref, pos_ref[0, 0, 2 * r], ga_ref, r, sem).start()
        _row_copy(yb_ref, pos_ref[0, 0, 2 * r + 1], gb_ref, r, sem).start()
        return carry
    lax.fori_loop(0, tm, issue, 0)

    def drain(r, carry):
        _row_copy(yb_ref, 0, ga_ref, 0, sem).wait()
        _row_copy(yb_ref, 0, gb_ref, 0, sem).wait()
        return carry
    lax.fori_loop(0, tm, drain, 0)

    wt = wt_ref[...]
    ya = ga_ref[...].reshape(tm, D).astype(F32)
    yb = gb_ref[...].reshape(tm, D).astype(F32)
    y = wt[:, 0:1] * ya + wt[:, 1:2] * yb
    h = h_ref[...] + g2_ref[0] * y
    ms = jnp.mean(h * h, axis=-1, keepdims=True)
    o_ref[...] = h * lax.rsqrt(ms + EPS) * nf_ref[...]


def _combine(pos2, yb, wts, h1, gate2, norm_final, seq):
    t = h1.shape[0]
    tm = min(TM_ROWS, seq)
    per_seq = seq // tm
    tok = lambda i: (i, 0)
    return pl.pallas_call(
        functools.partial(_combine_body, tm=tm),
        grid=(t // tm,),
        in_specs=[pl.BlockSpec((1, 1, 2 * tm), lambda i: (i, 0, 0), memory_space=pltpu.SMEM),
                  pl.BlockSpec(memory_space=pl.ANY),
                  pl.BlockSpec((tm, LANES), tok),
                  pl.BlockSpec((tm, D), tok),
                  pl.BlockSpec((1, 1, D), lambda i: (i // per_seq, 0, 0)),
                  pl.BlockSpec((1, D), lambda i: (0, 0))],
        out_specs=pl.BlockSpec((tm, D), tok),
        out_shape=jax.ShapeDtypeStruct((t, D), F32),
        scratch_shapes=[pltpu.VMEM((tm, ROW_SUB, LANES), BF16), pltpu.VMEM((tm, ROW_SUB, LANES), BF16),
                        pltpu.SemaphoreType.DMA(())],
        compiler_params=_params(("arbitrary",)),
        name="combine",
    )(pos2.reshape(t // tm, 1, 2 * tm), yb, wts, h1, gate2, norm_final)


def _layer(h2d, c, bsz, seq, w_ada, b_ada, norm_mix, w_in, w_gk2, b_gk2, gla_norm, w_a, w_pool, pool_scale, w_b,
           w_out, norm_ffn, w_rg, b_rg, w_re, b_re, w_gate, w_up, w_down):
    t = bsz * seq
    mod = _ada(c, w_ada, b_ada)
    shift1, scale1, gate1, shift2, scale2, gate2 = [m.reshape(bsz, 1, D) for m in jnp.split(mod, 6, axis=-1)]

    c_gk = 2 * KEY_DIM + 2 * VAL_DIM
    c_p = c_gk + GATE_RANK
    c_ga = c_p + POOL_DIM
    w_main = jnp.concatenate([w_in[:, :c_gk], w_in[:, c_ga:], w_in[:, c_p:c_ga]], axis=1).astype(BF16)
    w_gk = jnp.zeros((D, LANES), F32).at[:, :GATE_RANK].set(w_in[:, c_gk:c_p]).astype(BF16)

    z, gk = _inproj(h2d, norm_mix.reshape(1, D), scale1, shift1, w_main, w_gk, seq)
    o_gla = _gla(z, gk, w_gk2, b_gk2, gla_norm, bsz, seq)
    cb = _pool(z, w_pool, pool_scale, w_b, seq)

    w_router = jnp.zeros((D, LANES), F32).at[:, :N_GROUPS].set(w_rg)
    w_router = w_router.at[:, N_GROUPS:N_GROUPS + N_EXPERTS].set(w_re).astype(BF16)
    h1, u2, logits = _mix(o_gla, z, cb, h2d, gate1, norm_ffn.reshape(1, D), scale2, shift2,
                          w_a.astype(BF16), w_out.astype(BF16), w_router, seq)

    bias = jnp.zeros((1, LANES), F32).at[0, :N_GROUPS].set(b_rg).at[0, N_GROUPS:N_GROUPS + N_EXPERTS].set(b_re)
    pos, wts, cnt = _route(logits, bias)
    pos2 = pos[:, :2]

    counts = cnt[0, :N_EXPERTS].astype(I32)
    nblk = (counts + ROW_BLOCK - 1) // ROW_BLOCK
    bends = jnp.cumsum(nblk)
    pstart = ((bends - nblk) * ROW_BLOCK).astype(I32)
    padded = (nblk * ROW_BLOCK).astype(I32)
    n_rows = 2 * t + N_EXPERTS * ROW_BLOCK
    n_blocks = n_rows // ROW_BLOCK
    blk_e = jnp.sum(bends[None, :] <= jnp.arange(n_blocks, dtype=I32)[:, None], axis=1)
    blk_e = jnp.minimum(blk_e, N_EXPERTS - 1).astype(I32)
    n_used = bends[-1:].astype(I32)

    xs = _dispatch(u2, pos2, pstart, padded, n_used, n_rows)
    yb = _experts(xs, blk_e, n_used, w_gate.astype(BF16), w_up.astype(BF16), w_down.astype(BF16))
    return pos2, yb, wts, h1, gate2


def kernel(x, c, w_ada, b_ada, norm_mix, w_in, w_gk2, b_gk2, gla_norm, w_a, w_pool, pool_scale, w_b, w_out, norm_ffn, w_rg, b_rg, w_re, b_re, w_gate, w_up, w_down, norm_final):
    bsz, seq, _ = x.shape
    depth = w_ada.shape[0]
    assert depth == 1
    h2d = x.reshape(bsz * seq, D)
    pos2, yb, wts, h1, gate2 = _layer(
        h2d, c, bsz, seq, w_ada[0], b_ada[0], norm_mix[0], w_in[0], w_gk2[0], b_gk2[0], gla_norm[0], w_a[0],
        w_pool[0], pool_scale[0], w_b[0], w_out[0], norm_ffn[0], w_rg[0], b_rg[0], w_re[0], b_re[0],
        w_gate[0], w_up[0], w_down[0])
    out = _combine(pos2, yb, wts, h1, gate2, norm_final.reshape(1, D), seq)
    return out.reshape(bsz, seq, D)
```

```python
import functools

import jax
import jax.numpy as jnp
from jax import lax
from jax.experimental import pallas as pl
from jax.experimental.pallas import tpu as pltpu

F32 = jnp.float32
BF16 = jnp.bfloat16
I32 = jnp.int32

D = 2048
HEADS = 4
DK = 256
DV = 512
KEY_DIM = HEADS * DK
VAL_DIM = HEADS * DV
GATE_RANK = 16
GATE_NORMALIZER = 16.0
CHUNK = 64
POOL_WINDOWS = (2, 4, 8, 16)
POOL_DIM = 1024
POOL_GROUP_DIM = 256
POOL_HALO = 16
N_GROUPS = 8
EXPERTS_PER_GROUP = 8
N_EXPERTS = 64
EXPERT_HIDDEN = 1024
EPS = 1e-6
LANES = 128
ROW_SUB = D // LANES

Z_COLS = 2 * KEY_DIM + 2 * VAL_DIM + 2 * D + POOL_DIM

VMEM_LIMIT = 56 * 1024 * 1024

TM_IN = 1024
TN_IN = 1024
TC_GLA = 512
TM_POOL = 256
TM_MIX = 256
TM_ROUTE = 512
TM_ROWS = 256
ROW_BLOCK = 256


def _sigmoid(x):
    return 1.0 / (1.0 + jnp.exp(-x))


def _params(sem, vmem=VMEM_LIMIT):
    return pltpu.CompilerParams(dimension_semantics=sem, vmem_limit_bytes=vmem)


def _ada_body(c_ref, w_ref, b_ref, o_ref):
    c = c_ref[...]
    s = (c * _sigmoid(c)).astype(BF16)
    o_ref[...] = jnp.dot(s, w_ref[...].astype(BF16), preferred_element_type=F32) + b_ref[...]


def _ada(c, w, b):
    bsz = c.shape[0]
    cp = jnp.zeros((8, D), F32).at[:bsz].set(c)
    n = w.shape[1]
    tn = 1024
    out = pl.pallas_call(
        _ada_body,
        grid=(n // tn,),
        in_specs=[pl.BlockSpec((8, D), lambda j: (0, 0)),
                  pl.BlockSpec((D, tn), lambda j: (0, j)),
                  pl.BlockSpec((1, tn), lambda j: (0, j))],
        out_specs=pl.BlockSpec((8, tn), lambda j: (0, j)),
        out_shape=jax.ShapeDtypeStruct((8, n), F32),
        compiler_params=_params(("arbitrary",)),
        name="ada",
    )(cp, w, b.reshape(1, n))
    return out[:bsz]


def _inproj_body(x_ref, g_ref, sc_ref, sh_ref, w_ref, wgk_ref, z_ref, gk_ref, u_scr):
    @pl.when(pl.program_id(1) == 0)
    def _():
        x = x_ref[...]
        ms = jnp.mean(x * x, axis=-1, keepdims=True)
        u = x * lax.rsqrt(ms + EPS) * g_ref[...]
        u = (u * (1.0 + sc_ref[0]) + sh_ref[0]).astype(BF16)
        u_scr[...] = u
        gk_ref[...] = jnp.dot(u, wgk_ref[...], preferred_element_type=F32)

    z_ref[...] = jnp.dot(u_scr[...], w_ref[...], preferred_element_type=F32).astype(BF16)


def _inproj(x2, gain, scale, shift, w_main, w_gk, seq):
    t = x2.shape[0]
    tm = min(TM_IN, seq)
    per_seq = seq // tm
    return pl.pallas_call(
        _inproj_body,
        grid=(t // tm, Z_COLS // TN_IN),
        in_specs=[pl.BlockSpec((tm, D), lambda i, j: (i, 0)),
                  pl.BlockSpec((1, D), lambda i, j: (0, 0)),
                  pl.BlockSpec((1, 1, D), lambda i, j: (i // per_seq, 0, 0)),
                  pl.BlockSpec((1, 1, D), lambda i, j: (i // per_seq, 0, 0)),
                  pl.BlockSpec((D, TN_IN), lambda i, j: (0, j)),
                  pl.BlockSpec((D, LANES), lambda i, j: (0, 0))],
        out_specs=[pl.BlockSpec((tm, TN_IN), lambda i, j: (i, j)),
                   pl.BlockSpec((tm, LANES), lambda i, j: (i, 0))],
        out_shape=[jax.ShapeDtypeStruct((t, Z_COLS), BF16),
                   jax.ShapeDtypeStruct((t, LANES), F32)],
        scratch_shapes=[pltpu.VMEM((tm, D), BF16)],
        compiler_params=_params(("arbitrary", "arbitrary")),
        name="inproj",
    )(x2, gain, scale, shift, w_main, w_gk)


def _gla_body(q_ref, k_ref, v_ref, g_ref, gk_ref, wgk_ref, bgk_ref, gn_ref, ll_ref, o_ref, st_ref, *, n_chunks):
    @pl.when(pl.program_id(1) == 0)
    def _():
        st_ref[...] = jnp.zeros_like(st_ref)

    row = lax.broadcasted_iota(I32, (CHUNK, CHUNK), 0)
    col = lax.broadcasted_iota(I32, (CHUNK, CHUNK), 1)
    causal = row >= col
    nt = (((1,), (1,)), ((), ()))
    tn = (((0,), (0,)), ((), ()))

    def chunk(ci, carry):
        r0 = pl.multiple_of(ci * CHUNK, CHUNK)
        rows = pl.ds(r0, CHUNK)
        gk = gk_ref[rows, :].astype(BF16)
        xg = jnp.dot(gk, wgk_ref[...], preferred_element_type=F32) + bgk_ref[...]
        la = (jnp.minimum(xg, 0.0) - jnp.log1p(jnp.exp(-jnp.abs(xg)))) * (1.0 / GATE_NORMALIZER)
        la_hi = la.astype(BF16)
        la_lo = (la - la_hi.astype(F32)).astype(BF16)
        ll = ll_ref[...]
        bb = (jnp.dot(ll, la_hi, preferred_element_type=F32)
              + jnp.dot(ll, la_lo, preferred_element_type=F32))
        b = bb[:CHUNK]
        bl = bb[CHUNK:]
        q = q_ref[rows, :].astype(F32)
        k = k_ref[rows, :].astype(F32)
        qe = (q * (DK ** -0.5) * jnp.exp(b)).astype(BF16)
        ke = (k * jnp.exp(-b)).astype(BF16)
        kd = (k * jnp.exp(bl - b)).astype(BF16)
        dec = jnp.exp(bl[0:1, :])
        for h in range(HEADS):
            sk = slice(h * DK, (h + 1) * DK)
            sv = slice(h * DV, (h + 1) * DV)
            att = lax.dot_general(qe[:, sk], ke[:, sk], nt, preferred_element_type=F32)
            att = jnp.where(causal, att, 0.0).astype(BF16)
            vc = v_ref[rows, sv]
            st = st_ref[h]
            o = (jnp.dot(att, vc, preferred_element_type=F32)
                 + lax.dot_general(qe[:, sk], st.astype(BF16), nt, preferred_element_type=F32))
            st_ref[h] = st * dec[:, sk] + lax.dot_general(vc, kd[:, sk], tn, preferred_element_type=F32)
            ms = jnp.mean(o * o, axis=-1, keepdims=True)
            on = o * lax.rsqrt(ms + EPS) * gn_ref[...]
            gg = g_ref[rows, sv].astype(F32)
            o_ref[rows, sv] = (on * (gg * _sigmoid(gg))).astype(BF16)
        return carry

    lax.fori_loop(0, n_chunks, chunk, 0)


def _gla(z, gk, w_gk2, b_gk2, gla_norm, bsz, seq):
    t = z.shape[0]
    tc = min(TC_GLA, seq)
    per_seq = seq // tc
    wgk = jnp.zeros((LANES, KEY_DIM), F32).at[:GATE_RANK].set(w_gk2).astype(BF16)
    r = jnp.arange(2 * CHUNK)[:, None]
    c = jnp.arange(CHUNK)[None, :]
    ll = jnp.where((r >= CHUNK) | (r >= c), 1.0, 0.0).astype(BF16)
    kb = KEY_DIM // KEY_DIM
    return pl.pallas_call(
        functools.partial(_gla_body, n_chunks=tc // CHUNK),
        grid=(bsz, per_seq),
        in_specs=[pl.BlockSpec((tc, KEY_DIM), lambda b, s: (b * per_seq + s, 0)),
                  pl.BlockSpec((tc, KEY_DIM), lambda b, s: (b * per_seq + s, kb)),
                  pl.BlockSpec((tc, VAL_DIM), lambda b, s: (b * per_seq + s, 1)),
                  pl.BlockSpec((tc, VAL_DIM), lambda b, s: (b * per_seq + s, 2)),
                  pl.BlockSpec((tc, LANES), lambda b, s: (b * per_seq + s, 0)),
                  pl.BlockSpec((LANES, KEY_DIM), lambda b, s: (0, 0)),
                  pl.BlockSpec((1, KEY_DIM), lambda b, s: (0, 0)),
                  pl.BlockSpec((1, DV), lambda b, s: (0, 0)),
                  pl.BlockSpec((2 * CHUNK, CHUNK), lambda b, s: (0, 0))],
        out_specs=pl.BlockSpec((tc, VAL_DIM), lambda b, s: (b * per_seq + s, 0)),
        out_shape=jax.ShapeDtypeStruct((t, VAL_DIM), BF16),
        scratch_shapes=[pltpu.VMEM((HEADS, DV, DK), F32)],
        compiler_params=_params(("arbitrary", "arbitrary")),
        name="gla",
    )(z, z, z, z, gk, wgk, b_gk2.reshape(1, KEY_DIM), gla_norm.reshape(1, DV), ll)


def _pool_body(p_ref, halo_ref, gb_ref, wp_ref, ps_ref, wb_ref, o_ref, *, tm, seq):
    base = (pl.program_id(0) * tm) % seq
    pf = p_ref[...].astype(F32)
    hal = halo_ref[...].astype(F32)
    hal = jnp.where(base == 0, 0.0, hal)
    ext = jnp.concatenate([hal, pf], axis=0)
    pos = base + lax.broadcasted_iota(I32, (tm, 1), 0)
    ys = []
    for gi, w in enumerate(POOL_WINDOWS):
        cols = slice(gi * POOL_GROUP_DIM, (gi + 1) * POOL_GROUP_DIM)
        s = ext[:, cols]
        sh = 1
        while sh < w:
            s = s + pltpu.roll(s, sh, 0)
            sh *= 2
        cnt = jnp.minimum(pos + 1, w).astype(F32)
        m = s[POOL_HALO:] / cnt - pf[:, cols]
        ys.append(jnp.dot(m.astype(BF16), wp_ref[gi], preferred_element_type=F32))
    y = jnp.concatenate(ys, axis=1) * ps_ref[...]
    yb = jnp.dot(y.astype(BF16), wb_ref[...], preferred_element_type=F32)
    o_ref[...] = (yb * _sigmoid(gb_ref[...].astype(F32))).astype(BF16)


def _pool(z, w_pool, pool_scale, w_b, seq):
    t = z.shape[0]
    tm = min(TM_POOL, seq)
    p_blk = (Z_COLS - POOL_DIM) // POOL_DIM
    gb_blk = (2 * KEY_DIM + 2 * VAL_DIM + D) // D
    hb = tm // POOL_HALO
    return pl.pallas_call(
        functools.partial(_pool_body, tm=tm, seq=seq),
        grid=(t // tm,),
        in_specs=[pl.BlockSpec((tm, POOL_DIM), lambda i: (i, p_blk)),
                  pl.BlockSpec((POOL_HALO, POOL_DIM), lambda i: (jnp.maximum(i * hb - 1, 0), p_blk)),
                  pl.BlockSpec((tm, D), lambda i: (i, gb_blk)),
                  pl.BlockSpec((len(POOL_WINDOWS), POOL_GROUP_DIM, POOL_GROUP_DIM), lambda i: (0, 0, 0)),
                  pl.BlockSpec((1, POOL_DIM), lambda i: (0, 0)),
                  pl.BlockSpec((POOL_DIM, D), lambda i: (0, 0))],
        out_specs=pl.BlockSpec((tm, D), lambda i: (i, 0)),
        out_shape=jax.ShapeDtypeStruct((t, D), BF16),
        compiler_params=_params(("arbitrary",)),
        name="pool",
    )(z, z, z, w_pool.astype(BF16), pool_scale.reshape(1, POOL_DIM), w_b.astype(BF16))


def _mix_body(o_ref, ga_ref, cb_ref, x_ref, g1_ref, nf_ref, sc_ref, sh_ref, wa_ref, wo_ref, wr_ref,
              h_ref, u_ref, lg_ref):
    ya = jnp.dot(o_ref[...], wa_ref[...], preferred_element_type=F32)
    merged = ya * _sigmoid(ga_ref[...].astype(F32)) + cb_ref[...].astype(F32)
    mo = jnp.dot(merged.astype(BF16), wo_ref[...], preferred_element_type=F32)
    h = x_ref[...] + g1_ref[0] * mo
    h_ref[...] = h
    ms = jnp.mean(h * h, axis=-1, keepdims=True)
    u = h * lax.rsqrt(ms + EPS) * nf_ref[...]
    u = u * (1.0 + sc_ref[0]) + sh_ref[0]
    ub = u.astype(BF16)
    u_ref[...] = ub.reshape(u_ref.shape)
    lg_ref[...] = jnp.dot(ub, wr_ref[...], preferred_element_type=F32)


def _mix(o_gla, z, cb, x2, gate1, norm_ffn, scale2, shift2, w_a, w_out, w_router, seq):
    t = x2.shape[0]
    tm = min(TM_MIX, seq)
    per_seq = seq // tm
    ga_blk = (2 * KEY_DIM + 2 * VAL_DIM) // D
    tok = lambda i: (i, 0)
    const = lambda i: (0, 0)
    per_b = lambda i: (i // per_seq, 0, 0)
    return pl.pallas_call(
        _mix_body,
        grid=(t // tm,),
        in_specs=[pl.BlockSpec((tm, VAL_DIM), tok),
                  pl.BlockSpec((tm, D), lambda i: (i, ga_blk)),
                  pl.BlockSpec((tm, D), tok),
                  pl.BlockSpec((tm, D), tok),
                  pl.BlockSpec((1, 1, D), per_b),
                  pl.BlockSpec((1, D), const),
                  pl.BlockSpec((1, 1, D), per_b),
                  pl.BlockSpec((1, 1, D), per_b),
                  pl.BlockSpec((VAL_DIM, D), const, pipeline_mode=pl.Buffered(1)),
                  pl.BlockSpec((D, D), const, pipeline_mode=pl.Buffered(1)),
                  pl.BlockSpec((D, LANES), const)],
        out_specs=[pl.BlockSpec((tm, D), tok),
                   pl.BlockSpec((tm, ROW_SUB, LANES), lambda i: (i, 0, 0)),
                   pl.BlockSpec((tm, LANES), tok)],
        out_shape=[jax.ShapeDtypeStruct((t, D), F32),
                   jax.ShapeDtypeStruct((t, ROW_SUB, LANES), BF16),
                   jax.ShapeDtypeStruct((t, LANES), F32)],
        compiler_params=_params(("arbitrary",)),
        name="mix",
    )(o_gla, z, cb, x2, gate1, norm_ffn, scale2, shift2, w_a, w_out, w_router)


def _route_topk(gl, tm):
    lane = lax.broadcasted_iota(I32, (tm, LANES), 1)
    lanef = lane.astype(F32)
    neg = float("-inf")
    big = float(LANES)
    gmask = lane < N_GROUPS
    gmax = jnp.max(jnp.where(gmask, gl, neg), axis=-1, keepdims=True)
    gsel = jnp.min(jnp.where(gmask & (gl == gmax), lanef, big), axis=-1, keepdims=True)
    gsum = jnp.sum(jnp.where(gmask, jnp.exp(gl - gmax), 0.0), axis=-1, keepdims=True)
    pg = 1.0 / gsum
    lo = N_GROUPS + gsel * EXPERTS_PER_GROUP
    emask = (lanef >= lo) & (lanef < lo + EXPERTS_PER_GROUP)
    emax = jnp.max(jnp.where(emask, gl, neg), axis=-1, keepdims=True)
    ee = jnp.where(emask, jnp.exp(gl - emax), 0.0)
    eprob = ee / jnp.sum(ee, axis=-1, keepdims=True)
    v1 = jnp.max(jnp.where(emask, eprob, -1.0), axis=-1, keepdims=True)
    i1 = jnp.min(jnp.where(emask & (eprob == v1), lanef, big), axis=-1, keepdims=True)
    m2 = emask & (lanef != i1)
    v2 = jnp.max(jnp.where(m2, eprob, -1.0), axis=-1, keepdims=True)
    i2 = jnp.min(jnp.where(m2 & (eprob == v2), lanef, big), axis=-1, keepdims=True)
    tot = v1 + v2
    w1 = pg * (v1 / tot)
    w2 = pg * (v2 / tot)
    hit1 = lanef == i1 - N_GROUPS
    hit2 = lanef == i2 - N_GROUPS
    return hit1, hit2, w1, w2


def _count_body(lg_ref, bias_ref, cnt_ref, *, tm):
    @pl.when(pl.program_id(0) == 0)
    def _():
        cnt_ref[...] = jnp.zeros_like(cnt_ref)

    hit1, hit2, _, _ = _route_topk(lg_ref[...] + bias_ref[...], tm)
    oh = jnp.where(hit1 | hit2, 1.0, 0.0)
    cnt_ref[...] += jnp.sum(oh, axis=0, keepdims=True)


def _assign_body(lg_ref, bias_ref, cnt_ref, ls_ref, ut_ref, pos_ref, wt_ref, carry_ref, pst_ref, *, tm):
    lane = lax.broadcasted_iota(I32, (tm, LANES), 1)

    @pl.when(pl.program_id(0) == 0)
    def _():
        cnt = cnt_ref[0:1, :]
        nblk = jnp.floor((cnt + (ROW_BLOCK - 1)) * (1.0 / ROW_BLOCK))
        hi = jnp.floor(nblk * (1.0 / 256.0))
        lo = nblk - 256.0 * hi
        ut = ut_ref[...]
        hi8 = jnp.broadcast_to(hi, (8, LANES)).astype(BF16)
        lo8 = jnp.broadcast_to(lo, (8, LANES)).astype(BF16)
        pre = (256.0 * jnp.dot(hi8, ut, preferred_element_type=F32)
               + jnp.dot(lo8, ut, preferred_element_type=F32))
        pst_ref[...] = pre[0:1] * float(ROW_BLOCK)
        carry_ref[...] = jnp.zeros_like(carry_ref)

    hit1, hit2, w1, w2 = _route_topk(lg_ref[...] + bias_ref[...], tm)
    oh = jnp.where(hit1 | hit2, 1.0, 0.0)
    cum = jnp.dot(ls_ref[...], oh.astype(BF16), preferred_element_type=F32)
    slot = cum + carry_ref[...] + pst_ref[...]
    carry_ref[...] += jnp.sum(oh, axis=0, keepdims=True)
    p1 = jnp.sum(jnp.where(hit1, slot, 0.0), axis=-1, keepdims=True)
    p2 = jnp.sum(jnp.where(hit2, slot, 0.0), axis=-1, keepdims=True)
    pos_ref[...] = jnp.where(lane == 0, p1, jnp.where(lane == 1, p2, 0.0)).astype(I32)
    wt_ref[...] = jnp.where(lane == 0, w1, jnp.where(lane == 1, w2, 0.0))


def _route(logits, bias):
    t = logits.shape[0]
    tm = min(TM_ROUTE, t)
    r = jnp.arange(tm)
    ls = jnp.where(r[:, None] > r[None, :], 1.0, 0.0).astype(BF16)
    e = jnp.arange(LANES)
    ut = jnp.where(e[:, None] < e[None, :], 1.0, 0.0).astype(BF16)
    tok = lambda i: (i, 0)
    const = lambda i: (0, 0)
    cnt = pl.pallas_call(
        functools.partial(_count_body, tm=tm),
        grid=(t // tm,),
        in_specs=[pl.BlockSpec((tm, LANES), tok), pl.BlockSpec((1, LANES), const)],
        out_specs=pl.BlockSpec((8, LANES), const),
        out_shape=jax.ShapeDtypeStruct((8, LANES), F32),
        compiler_params=_params(("arbitrary",)),
        name="route_count",
    )(logits, bias)
    pos, wts = pl.pallas_call(
        functools.partial(_assign_body, tm=tm),
        grid=(t // tm,),
        in_specs=[pl.BlockSpec((tm, LANES), tok),
                  pl.BlockSpec((1, LANES), const),
                  pl.BlockSpec((8, LANES), const),
                  pl.BlockSpec((tm, tm), const),
                  pl.BlockSpec((LANES, LANES), const)],
        out_specs=[pl.BlockSpec((tm, LANES), tok), pl.BlockSpec((tm, LANES), tok)],
        out_shape=[jax.ShapeDtypeStruct((t, LANES), I32), jax.ShapeDtypeStruct((t, LANES), F32)],
        scratch_shapes=[pltpu.VMEM((1, LANES), F32), pltpu.VMEM((1, LANES), F32)],
        compiler_params=_params(("arbitrary",)),
        name="route_assign",
    )(logits, bias, cnt, ls, ut)
    return pos, wts, cnt


def _row_copy(src, s, dst, d, sem):
    return pltpu.make_async_copy(src.at[s], dst.at[d], sem)


def _expert_body(be_ref, nu_ref, cur_ref, nxt_ref, u_ref, wg_ref, wu_ref, wd_ref, y_ref, xbuf, sem, *, n_blocks):
    i = pl.program_id(0)
    slot = i % 2

    def gather(tok_ref, dst, dsem):
        def issue(p, carry):
            for k in range(2):
                r = 2 * p + k
                _row_copy(u_ref, tok_ref[0, 0, r], dst, r, dsem).start(priority=k)
            return carry
        lax.fori_loop(0, ROW_BLOCK // 2, issue, 0, unroll=4)

    @pl.when(i == 0)
    def _():
        gather(cur_ref, xbuf.at[0], sem.at[0])

    @pl.when(i + 1 < n_blocks)
    def _():
        gather(nxt_ref, xbuf.at[1 - slot], sem.at[1 - slot])

    def drain(r, carry):
        _row_copy(u_ref, 0, xbuf.at[slot], 0, sem.at[slot]).wait()
        return carry
    lax.fori_loop(0, ROW_BLOCK, drain, 0, unroll=8)

    @pl.when(i < nu_ref[0])
    def _():
        x = xbuf[slot].reshape(ROW_BLOCK, D)
        a = jnp.dot(x, wg_ref[0], preferred_element_type=F32)
        b = jnp.dot(x, wu_ref[0], preferred_element_type=F32)
        hdn = (a * _sigmoid(a) * b).astype(BF16)
        y = jnp.dot(hdn, wd_ref[0], preferred_element_type=F32)
        y_ref[...] = y.astype(BF16).reshape(y_ref.shape)

    @pl.when(i >= nu_ref[0])
    def _():
        y_ref[...] = jnp.zeros_like(y_ref)


def _experts(u3, slot_tok, blk_e, n_used, w_gate, w_up, w_down):
    n_rows = slot_tok.shape[0]
    nb = n_rows // ROW_BLOCK
    rows = lambda i, be, nu: (i, 0, 0)
    wsel = lambda i, be, nu: (be[i], 0, 0)
    toks = slot_tok.reshape(nb, 1, ROW_BLOCK)
    grid_spec = pltpu.PrefetchScalarGridSpec(
        num_scalar_prefetch=2,
        grid=(nb,),
        in_specs=[pl.BlockSpec((1, 1, ROW_BLOCK), rows, memory_space=pltpu.SMEM),
                  pl.BlockSpec((1, 1, ROW_BLOCK), lambda i, be, nu: (jnp.minimum(i + 1, nb - 1), 0, 0),
                               memory_space=pltpu.SMEM),
                  pl.BlockSpec(memory_space=pl.ANY),
                  pl.BlockSpec((1, D, EXPERT_HIDDEN), wsel),
                  pl.BlockSpec((1, D, EXPERT_HIDDEN), wsel),
                  pl.BlockSpec((1, EXPERT_HIDDEN, D), wsel)],
        out_specs=pl.BlockSpec((ROW_BLOCK, ROW_SUB, LANES), rows),
        scratch_shapes=[pltpu.VMEM((2, ROW_BLOCK, ROW_SUB, LANES), BF16), pltpu.SemaphoreType.DMA((2,))],
    )
    return pl.pallas_call(
        functools.partial(_expert_body, n_blocks=nb),
        grid_spec=grid_spec,
        out_shape=jax.ShapeDtypeStruct((n_rows, ROW_SUB, LANES), BF16),
        compiler_params=_params(("arbitrary",)),
        name="experts",
    )(blk_e, n_used, toks, toks, u3, w_gate, w_up, w_down)


def _combine_body(cur_ref, nxt_ref, yb_ref, wt_ref, h_ref, g2_ref, nf_ref, o_ref, gbuf, sem, *, tm, n_steps):
    i = pl.program_id(0)
    slot = i % 2

    def gather(pos_ref, s):
        def issue(r, carry):
            for k in range(2):
                _row_copy(yb_ref, pos_ref[0, 0, 2 * r + k], gbuf.at[s, k], r, sem.at[s]).start(priority=k)
            return carry
        lax.fori_loop(0, tm, issue, 0, unroll=4)

    @pl.when(i == 0)
    def _():
        gather(cur_ref, 0)

    @pl.when(i + 1 < n_steps)
    def _():
        gather(nxt_ref, 1 - slot)

    def drain(r, carry):
        for k in range(2):
            _row_copy(yb_ref, 0, gbuf.at[slot, k], 0, sem.at[slot]).wait()
        return carry
    lax.fori_loop(0, tm, drain, 0, unroll=4)

    wt = wt_ref[...]
    ya = gbuf[slot, 0].reshape(tm, D).astype(F32)
    yb = gbuf[slot, 1].reshape(tm, D).astype(F32)
    y = wt[:, 0:1] * ya + wt[:, 1:2] * yb
    h = h_ref[...] + g2_ref[0] * y
    ms = jnp.mean(h * h, axis=-1, keepdims=True)
    o_ref[...] = h * lax.rsqrt(ms + EPS) * nf_ref[...]


def _combine(pos2, yb, wts, h1, gate2, norm_final, seq):
    t = h1.shape[0]
    tm = min(TM_ROWS, seq)
    per_seq = seq // tm
    tok = lambda i: (i, 0)
    n_steps = t // tm
    pos3 = pos2.reshape(n_steps, 1, 2 * tm)
    return pl.pallas_call(
        functools.partial(_combine_body, tm=tm, n_steps=n_steps),
        grid=(n_steps,),
        in_specs=[pl.BlockSpec((1, 1, 2 * tm), lambda i: (i, 0, 0), memory_space=pltpu.SMEM),
                  pl.BlockSpec((1, 1, 2 * tm), lambda i: (jnp.minimum(i + 1, n_steps - 1), 0, 0),
                               memory_space=pltpu.SMEM),
                  pl.BlockSpec(memory_space=pl.ANY),
                  pl.BlockSpec((tm, LANES), tok),
                  pl.BlockSpec((tm, D), tok),
                  pl.BlockSpec((1, 1, D), lambda i: (i // per_seq, 0, 0)),
                  pl.BlockSpec((1, D), lambda i: (0, 0))],
        out_specs=pl.BlockSpec((tm, D), tok),
        out_shape=jax.ShapeDtypeStruct((t, D), F32),
        scratch_shapes=[pltpu.VMEM((2, 2, tm, ROW_SUB, LANES), BF16), pltpu.SemaphoreType.DMA((2,))],
        compiler_params=_params(("arbitrary",)),
        name="combine",
    )(pos3, pos3, yb, wts, h1, gate2, norm_final)


def _layer(h2d, c, bsz, seq, w_ada, b_ada, norm_mix, w_in, w_gk2, b_gk2, gla_norm, w_a, w_pool, pool_scale, w_b,
           w_out, norm_ffn, w_rg, b_rg, w_re, b_re, w_gate, w_up, w_down):
    t = bsz * seq
    mod = _ada(c, w_ada, b_ada)
    shift1, scale1, gate1, shift2, scale2, gate2 = [m.reshape(bsz, 1, D) for m in jnp.split(mod, 6, axis=-1)]

    c_gk = 2 * KEY_DIM + 2 * VAL_DIM
    c_p = c_gk + GATE_RANK
    c_ga = c_p + POOL_DIM
    w_main = jnp.concatenate([w_in[:, :c_gk], w_in[:, c_ga:], w_in[:, c_p:c_ga]], axis=1).astype(BF16)
    w_gk = jnp.zeros((D, LANES), F32).at[:, :GATE_RANK].set(w_in[:, c_gk:c_p]).astype(BF16)

    z, gk = _inproj(h2d, norm_mix.reshape(1, D), scale1, shift1, w_main, w_gk, seq)
    o_gla = _gla(z, gk, w_gk2, b_gk2, gla_norm, bsz, seq)
    cb = _pool(z, w_pool, pool_scale, w_b, seq)

    w_router = jnp.zeros((D, LANES), F32).at[:, :N_GROUPS].set(w_rg)
    w_router = w_router.at[:, N_GROUPS:N_GROUPS + N_EXPERTS].set(w_re).astype(BF16)
    h1, u2, logits = _mix(o_gla, z, cb, h2d, gate1, norm_ffn.reshape(1, D), scale2, shift2,
                          w_a.astype(BF16), w_out.astype(BF16), w_router, seq)

    bias = jnp.zeros((1, LANES), F32).at[0, :N_GROUPS].set(b_rg).at[0, N_GROUPS:N_GROUPS + N_EXPERTS].set(b_re)
    pos, wts, cnt = _route(logits, bias)
    pos2 = pos[:, :2]

    counts = cnt[0, :N_EXPERTS].astype(I32)
    nblk = (counts + ROW_BLOCK - 1) // ROW_BLOCK
    bends = jnp.cumsum(nblk)
    n_rows = 2 * t + N_EXPERTS * ROW_BLOCK
    n_blocks = n_rows // ROW_BLOCK
    blk_e = jnp.sum(bends[None, :] <= jnp.arange(n_blocks, dtype=I32)[:, None], axis=1)
    blk_e = jnp.minimum(blk_e, N_EXPERTS - 1).astype(I32)
    n_used = bends[-1:].astype(I32)
    slot_tok = jnp.zeros((n_rows,), I32).at[pos2.reshape(-1)].set(
        jnp.arange(2 * t, dtype=I32) // 2, unique_indices=True)

    yb = _experts(u2, slot_tok, blk_e, n_used, w_gate.astype(BF16), w_up.astype(BF16), w_down.astype(BF16))
    return pos2, yb, wts, h1, gate2


def kernel(x, c, w_ada, b_ada, norm_mix, w_in, w_gk2, b_gk2, gla_norm, w_a, w_pool, pool_scale, w_b, w_out, norm_ffn, w_rg, b_rg, w_re, b_re, w_gate, w_up, w_down, norm_final):
    bsz, seq, _ = x.shape
    depth = w_ada.shape[0]
    assert depth == 1
    h2d = x.reshape(bsz * seq, D)
    pos2, yb, wts, h1, gate2 = _layer(
        h2d, c, bsz, seq, w_ada[0], b_ada[0], norm_mix[0], w_in[0], w_gk2[0], b_gk2[0], gla_norm[0], w_a[0],
        w_pool[0], pool_scale[0], w_b[0], w_out[0], norm_ffn[0], w_rg[0], b_rg[0], w_re[0], b_re[0],
        w_gate[0], w_up[0], w_down[0])
    out = _combine(pos2, yb, wts, h1, gate2, norm_final.reshape(1, D), seq)
    return out.reshape(bsz, seq, D)
```

```python
import functools

import jax
import jax.numpy as jnp
from jax import lax
from jax.experimental import pallas as pl
from jax.experimental.pallas import tpu as pltpu

F32 = jnp.float32
BF16 = jnp.bfloat16
I32 = jnp.int32

D = 2048
HEADS = 4
DK = 256
DV = 512
KEY_DIM = HEADS * DK
VAL_DIM = HEADS * DV
GATE_RANK = 16
GATE_NORMALIZER = 16.0
CHUNK = 64
POOL_WINDOWS = (2, 4, 8, 16)
POOL_DIM = 1024
POOL_GROUP_DIM = 256
POOL_HALO = 16
N_GROUPS = 8
EXPERTS_PER_GROUP = 8
N_EXPERTS = 64
EXPERT_HIDDEN = 1024
EPS = 1e-6
LANES = 128
ROW_SUB = D // LANES

Z_COLS = 2 * KEY_DIM + 2 * VAL_DIM + 2 * D + POOL_DIM

VMEM_LIMIT = 56 * 1024 * 1024
VMEM_LIMIT_INPROJ = 60 * 1024 * 1024
CAST_STEPS = 8

TM_IN = 1024
TN_IN = 1024
TC_GLA = 512
TM_POOL = 256
TM_MIX = 256
TM_ROUTE = 512
TM_ROWS = 256
ROW_BLOCK = 256
EXPERT_PARTS = 4


def _sigmoid(x):
    return 1.0 / (1.0 + jnp.exp(-x))


def _params(sem, vmem=VMEM_LIMIT):
    return pltpu.CompilerParams(dimension_semantics=sem, vmem_limit_bytes=vmem)


def _ada_body(c_ref, w_ref, b_ref, o_ref):
    c = c_ref[...]
    s = (c * _sigmoid(c)).astype(BF16)
    o_ref[...] = jnp.dot(s, w_ref[...].astype(BF16), preferred_element_type=F32) + b_ref[...]


def _ada(c, w, b):
    bsz = c.shape[0]
    cp = jnp.zeros((8, D), F32).at[:bsz].set(c)
    n = w.shape[1]
    tn = 1024
    out = pl.pallas_call(
        _ada_body,
        grid=(n // tn,),
        in_specs=[pl.BlockSpec((8, D), lambda j: (0, 0)),
                  pl.BlockSpec((D, tn), lambda j: (0, j)),
                  pl.BlockSpec((1, tn), lambda j: (0, j))],
        out_specs=pl.BlockSpec((8, tn), lambda j: (0, j)),
        out_shape=jax.ShapeDtypeStruct((8, n), F32),
        compiler_params=_params(("arbitrary",)),
        name="ada",
    )(cp, w, b.reshape(1, n))
    return out[:bsz]


def _inproj_body(x_ref, g_ref, sc_ref, sh_ref, w_ref, wgk_ref, eg_ref, eu_ref, ed_ref,
                 z_ref, gk_ref, og_ref, ou_ref, od_ref, u_scr):
    @pl.when(pl.program_id(1) < CAST_STEPS)
    def _():
        og_ref[...] = eg_ref[...].astype(BF16)
        ou_ref[...] = eu_ref[...].astype(BF16)
        od_ref[...] = ed_ref[...].astype(BF16)

    @pl.when(pl.program_id(1) == 0)
    def _():
        x = x_ref[...]
        ms = jnp.mean(x * x, axis=-1, keepdims=True)
        u = x * lax.rsqrt(ms + EPS) * g_ref[...]
        u = (u * (1.0 + sc_ref[0]) + sh_ref[0]).astype(BF16)
        u_scr[...] = u
        gk_ref[...] = jnp.dot(u, wgk_ref[...], preferred_element_type=F32)

    z_ref[...] = jnp.dot(u_scr[...], w_ref[...], preferred_element_type=F32).astype(BF16)


def _inproj(x2, gain, scale, shift, w_main, w_gk, w_gate, w_up, w_down, seq):
    t = x2.shape[0]
    tm = min(TM_IN, seq)
    per_seq = seq // tm
    n_i = t // tm
    n_j = Z_COLS // TN_IN
    assert CAST_STEPS <= n_j
    n_slabs = n_i * CAST_STEPS
    eg = w_gate.reshape(N_EXPERTS * D, EXPERT_HIDDEN)
    eu = w_up.reshape(N_EXPERTS * D, EXPERT_HIDDEN)
    ed = w_down.reshape(N_EXPERTS * EXPERT_HIDDEN, D)
    rg, rd = eg.shape[0] // n_slabs, ed.shape[0] // n_slabs
    assert rg * n_slabs == eg.shape[0] and rd * n_slabs == ed.shape[0] and rd % 16 == 0
    slab = lambda i, j: (i * CAST_STEPS + jnp.minimum(j, CAST_STEPS - 1), 0)
    z, gk, og, ou, od = pl.pallas_call(
        _inproj_body,
        grid=(n_i, n_j),
        in_specs=[pl.BlockSpec((tm, D), lambda i, j: (i, 0)),
                  pl.BlockSpec((1, D), lambda i, j: (0, 0)),
                  pl.BlockSpec((1, 1, D), lambda i, j: (i // per_seq, 0, 0)),
                  pl.BlockSpec((1, 1, D), lambda i, j: (i // per_seq, 0, 0)),
                  pl.BlockSpec((D, TN_IN), lambda i, j: (0, j)),
                  pl.BlockSpec((D, LANES), lambda i, j: (0, 0)),
                  pl.BlockSpec((rg, EXPERT_HIDDEN), slab),
                  pl.BlockSpec((rg, EXPERT_HIDDEN), slab),
                  pl.BlockSpec((rd, D), slab)],
        out_specs=[pl.BlockSpec((tm, TN_IN), lambda i, j: (i, j)),
                   pl.BlockSpec((tm, LANES), lambda i, j: (i, 0)),
                   pl.BlockSpec((rg, EXPERT_HIDDEN), slab),
                   pl.BlockSpec((rg, EXPERT_HIDDEN), slab),
                   pl.BlockSpec((rd, D), slab)],
        out_shape=[jax.ShapeDtypeStruct((t, Z_COLS), BF16),
                   jax.ShapeDtypeStruct((t, LANES), F32),
                   jax.ShapeDtypeStruct(eg.shape, BF16),
                   jax.ShapeDtypeStruct(eu.shape, BF16),
                   jax.ShapeDtypeStruct(ed.shape, BF16)],
        scratch_shapes=[pltpu.VMEM((tm, D), BF16)],
        compiler_params=_params(("arbitrary", "arbitrary"), vmem=VMEM_LIMIT_INPROJ),
        name="inproj",
    )(x2, gain, scale, shift, w_main, w_gk, eg, eu, ed)
    return (z, gk, og.reshape(w_gate.shape), ou.reshape(w_up.shape), od.reshape(w_down.shape))


def _gla_body(q_ref, k_ref, v_ref, g_ref, gk_ref, wgk_ref, bgk_ref, gn_ref, ll_ref, o_ref, st_ref, *, n_chunks):
    @pl.when(pl.program_id(1) == 0)
    def _():
        st_ref[...] = jnp.zeros_like(st_ref)

    row = lax.broadcasted_iota(I32, (CHUNK, CHUNK), 0)
    col = lax.broadcasted_iota(I32, (CHUNK, CHUNK), 1)
    causal = row >= col
    nt = (((1,), (1,)), ((), ()))
    tn = (((0,), (0,)), ((), ()))

    def chunk(ci, carry):
        r0 = pl.multiple_of(ci * CHUNK, CHUNK)
        rows = pl.ds(r0, CHUNK)
        gk = gk_ref[rows, :].astype(BF16)
        xg = jnp.dot(gk, wgk_ref[...], preferred_element_type=F32) + bgk_ref[...]
        la = (jnp.minimum(xg, 0.0) - jnp.log1p(jnp.exp(-jnp.abs(xg)))) * (1.0 / GATE_NORMALIZER)
        la_hi = la.astype(BF16)
        la_lo = (la - la_hi.astype(F32)).astype(BF16)
        ll = ll_ref[...]
        bb = (jnp.dot(ll, la_hi, preferred_element_type=F32)
              + jnp.dot(ll, la_lo, preferred_element_type=F32))
        b = bb[:CHUNK]
        bl = bb[CHUNK:]
        q = q_ref[rows, :].astype(F32)
        k = k_ref[rows, :].astype(F32)
        qe = (q * (DK ** -0.5) * jnp.exp(b)).astype(BF16)
        ke = (k * jnp.exp(-b)).astype(BF16)
        kd = (k * jnp.exp(bl - b)).astype(BF16)
        dec = jnp.exp(bl[0:1, :])
        for h in range(HEADS):
            sk = slice(h * DK, (h + 1) * DK)
            sv = slice(h * DV, (h + 1) * DV)
            att = lax.dot_general(qe[:, sk], ke[:, sk], nt, preferred_element_type=F32)
            att = jnp.where(causal, att, 0.0).astype(BF16)
            vc = v_ref[rows, sv]
            st = st_ref[h]
            o = (jnp.dot(att, vc, preferred_element_type=F32)
                 + lax.dot_general(qe[:, sk], st.astype(BF16), nt, preferred_element_type=F32))
            st_ref[h] = st * dec[:, sk] + lax.dot_general(vc, kd[:, sk], tn, preferred_element_type=F32)
            ms = jnp.mean(o * o, axis=-1, keepdims=True)
            on = o * lax.rsqrt(ms + EPS) * gn_ref[...]
            gg = g_ref[rows, sv].astype(F32)
            o_ref[rows, sv] = (on * (gg * _sigmoid(gg))).astype(BF16)
        return carry

    lax.fori_loop(0, n_chunks, chunk, 0, unroll=2)


def _gla(z, gk, w_gk2, b_gk2, gla_norm, bsz, seq):
    t = z.shape[0]
    tc = min(TC_GLA, seq)
    per_seq = seq // tc
    wgk = jnp.zeros((LANES, KEY_DIM), F32).at[:GATE_RANK].set(w_gk2).astype(BF16)
    r = jnp.arange(2 * CHUNK)[:, None]
    c = jnp.arange(CHUNK)[None, :]
    ll = jnp.where((r >= CHUNK) | (r >= c), 1.0, 0.0).astype(BF16)
    kb = KEY_DIM // KEY_DIM
    return pl.pallas_call(
        functools.partial(_gla_body, n_chunks=tc // CHUNK),
        grid=(bsz, per_seq),
        in_specs=[pl.BlockSpec((tc, KEY_DIM), lambda b, s: (b * per_seq + s, 0)),
                  pl.BlockSpec((tc, KEY_DIM), lambda b, s: (b * per_seq + s, kb)),
                  pl.BlockSpec((tc, VAL_DIM), lambda b, s: (b * per_seq + s, 1)),
                  pl.BlockSpec((tc, VAL_DIM), lambda b, s: (b * per_seq + s, 2)),
                  pl.BlockSpec((tc, LANES), lambda b, s: (b * per_seq + s, 0)),
                  pl.BlockSpec((LANES, KEY_DIM), lambda b, s: (0, 0)),
                  pl.BlockSpec((1, KEY_DIM), lambda b, s: (0, 0)),
                  pl.BlockSpec((1, DV), lambda b, s: (0, 0)),
                  pl.BlockSpec((2 * CHUNK, CHUNK), lambda b, s: (0, 0))],
        out_specs=pl.BlockSpec((tc, VAL_DIM), lambda b, s: (b * per_seq + s, 0)),
        out_shape=jax.ShapeDtypeStruct((t, VAL_DIM), BF16),
        scratch_shapes=[pltpu.VMEM((HEADS, DV, DK), F32)],
        compiler_params=_params(("arbitrary", "arbitrary")),
        name="gla",
    )(z, z, z, z, gk, wgk, b_gk2.reshape(1, KEY_DIM), gla_norm.reshape(1, DV), ll)


def _pool_body(p_ref, halo_ref, gb_ref, wp_ref, ps_ref, wb_ref, o_ref, *, tm, seq):
    base = (pl.program_id(0) * tm) % seq
    pf = p_ref[...].astype(F32)
    hal = halo_ref[...].astype(F32)
    hal = jnp.where(base == 0, 0.0, hal)
    ext = jnp.concatenate([hal, pf], axis=0)
    pos = base + lax.broadcasted_iota(I32, (tm, 1), 0)
    ys = []
    for gi, w in enumerate(POOL_WINDOWS):
        cols = slice(gi * POOL_GROUP_DIM, (gi + 1) * POOL_GROUP_DIM)
        s = ext[:, cols]
        sh = 1
        while sh < w:
            s = s + pltpu.roll(s, sh, 0)
            sh *= 2
        cnt = jnp.minimum(pos + 1, w).astype(F32)
        m = s[POOL_HALO:] / cnt - pf[:, cols]
        ys.append(jnp.dot(m.astype(BF16), wp_ref[gi], preferred_element_type=F32))
    y = jnp.concatenate(ys, axis=1) * ps_ref[...]
    yb = jnp.dot(y.astype(BF16), wb_ref[...], preferred_element_type=F32)
    o_ref[...] = (yb * _sigmoid(gb_ref[...].astype(F32))).astype(BF16)


def _pool(z, w_pool, pool_scale, w_b, seq):
    t = z.shape[0]
    tm = min(TM_POOL, seq)
    p_blk = (Z_COLS - POOL_DIM) // POOL_DIM
    gb_blk = (2 * KEY_DIM + 2 * VAL_DIM + D) // D
    hb = tm // POOL_HALO
    return pl.pallas_call(
        functools.partial(_pool_body, tm=tm, seq=seq),
        grid=(t // tm,),
        in_specs=[pl.BlockSpec((tm, POOL_DIM), lambda i: (i, p_blk)),
                  pl.BlockSpec((POOL_HALO, POOL_DIM), lambda i: (jnp.maximum(i * hb - 1, 0), p_blk)),
                  pl.BlockSpec((tm, D), lambda i: (i, gb_blk)),
                  pl.BlockSpec((len(POOL_WINDOWS), POOL_GROUP_DIM, POOL_GROUP_DIM), lambda i: (0, 0, 0)),
                  pl.BlockSpec((1, POOL_DIM), lambda i: (0, 0)),
                  pl.BlockSpec((POOL_DIM, D), lambda i: (0, 0))],
        out_specs=pl.BlockSpec((tm, D), lambda i: (i, 0)),
        out_shape=jax.ShapeDtypeStruct((t, D), BF16),
        compiler_params=_params(("arbitrary",)),
        name="pool",
    )(z, z, z, w_pool.astype(BF16), pool_scale.reshape(1, POOL_DIM), w_b.astype(BF16))


def _mix_body(o_ref, ga_ref, cb_ref, x_ref, g1_ref, nf_ref, sc_ref, sh_ref, wa_ref, wo_ref, wr_ref,
              h_ref, u_ref, lg_ref):
    ya = jnp.dot(o_ref[...], wa_ref[...], preferred_element_type=F32)
    merged = ya * _sigmoid(ga_ref[...].astype(F32)) + cb_ref[...].astype(F32)
    mo = jnp.dot(merged.astype(BF16), wo_ref[...], preferred_element_type=F32)
    h = x_ref[...] + g1_ref[0] * mo
    h_ref[...] = h
    ms = jnp.mean(h * h, axis=-1, keepdims=True)
    u = h * lax.rsqrt(ms + EPS) * nf_ref[...]
    u = u * (1.0 + sc_ref[0]) + sh_ref[0]
    ub = u.astype(BF16)
    u_ref[...] = ub.reshape(u_ref.shape)
    lg_ref[...] = jnp.dot(ub, wr_ref[...], preferred_element_type=F32)


def _mix(o_gla, z, cb, x2, gate1, norm_ffn, scale2, shift2, w_a, w_out, w_router, seq):
    t = x2.shape[0]
    tm = min(TM_MIX, seq)
    per_seq = seq // tm
    ga_blk = (2 * KEY_DIM + 2 * VAL_DIM) // D
    tok = lambda i: (i, 0)
    const = lambda i: (0, 0)
    per_b = lambda i: (i // per_seq, 0, 0)
    return pl.pallas_call(
        _mix_body,
        grid=(t // tm,),
        in_specs=[pl.BlockSpec((tm, VAL_DIM), tok),
                  pl.BlockSpec((tm, D), lambda i: (i, ga_blk)),
                  pl.BlockSpec((tm, D), tok),
                  pl.BlockSpec((tm, D), tok),
                  pl.BlockSpec((1, 1, D), per_b),
                  pl.BlockSpec((1, D), const),
                  pl.BlockSpec((1, 1, D), per_b),
                  pl.BlockSpec((1, 1, D), per_b),
                  pl.BlockSpec((VAL_DIM, D), const, pipeline_mode=pl.Buffered(1)),
                  pl.BlockSpec((D, D), const, pipeline_mode=pl.Buffered(1)),
                  pl.BlockSpec((D, LANES), const)],
        out_specs=[pl.BlockSpec((tm, D), tok),
                   pl.BlockSpec((tm, ROW_SUB, LANES), lambda i: (i, 0, 0)),
                   pl.BlockSpec((tm, LANES), tok)],
        out_shape=[jax.ShapeDtypeStruct((t, D), F32),
                   jax.ShapeDtypeStruct((t, ROW_SUB, LANES), BF16),
                   jax.ShapeDtypeStruct((t, LANES), F32)],
        compiler_params=_params(("arbitrary",)),
        name="mix",
    )(o_gla, z, cb, x2, gate1, norm_ffn, scale2, shift2, w_a, w_out, w_router)


def _route_topk(gl, tm):
    lane = lax.broadcasted_iota(I32, (tm, LANES), 1)
    lanef = lane.astype(F32)
    neg = float("-inf")
    big = float(LANES)
    gmask = lane < N_GROUPS
    gmax = jnp.max(jnp.where(gmask, gl, neg), axis=-1, keepdims=True)
    gsel = jnp.min(jnp.where(gmask & (gl == gmax), lanef, big), axis=-1, keepdims=True)
    gsum = jnp.sum(jnp.where(gmask, jnp.exp(gl - gmax), 0.0), axis=-1, keepdims=True)
    pg = 1.0 / gsum
    lo = N_GROUPS + gsel * EXPERTS_PER_GROUP
    emask = (lanef >= lo) & (lanef < lo + EXPERTS_PER_GROUP)
    emax = jnp.max(jnp.where(emask, gl, neg), axis=-1, keepdims=True)
    ee = jnp.where(emask, jnp.exp(gl - emax), 0.0)
    eprob = ee / jnp.sum(ee, axis=-1, keepdims=True)
    v1 = jnp.max(jnp.where(emask, eprob, -1.0), axis=-1, keepdims=True)
    i1 = jnp.min(jnp.where(emask & (eprob == v1), lanef, big), axis=-1, keepdims=True)
    m2 = emask & (lanef != i1)
    v2 = jnp.max(jnp.where(m2, eprob, -1.0), axis=-1, keepdims=True)
    i2 = jnp.min(jnp.where(m2 & (eprob == v2), lanef, big), axis=-1, keepdims=True)
    tot = v1 + v2
    w1 = pg * (v1 / tot)
    w2 = pg * (v2 / tot)
    hit1 = lanef == i1 - N_GROUPS
    hit2 = lanef == i2 - N_GROUPS
    return hit1, hit2, w1, w2


def _count_body(lg_ref, bias_ref, cnt_ref, *, tm):
    @pl.when(pl.program_id(0) == 0)
    def _():
        cnt_ref[...] = jnp.zeros_like(cnt_ref)

    hit1, hit2, _, _ = _route_topk(lg_ref[...] + bias_ref[...], tm)
    oh = jnp.where(hit1 | hit2, 1.0, 0.0)
    cnt_ref[...] += jnp.sum(oh, axis=0, keepdims=True)


def _assign_body(lg_ref, bias_ref, cnt_ref, ls_ref, ut_ref, pos_ref, wt_ref, carry_ref, pst_ref, *, tm):
    lane = lax.broadcasted_iota(I32, (tm, LANES), 1)

    @pl.when(pl.program_id(0) == 0)
    def _():
        cnt = cnt_ref[0:1, :]
        nblk = jnp.floor((cnt + (ROW_BLOCK - 1)) * (1.0 / ROW_BLOCK))
        hi = jnp.floor(nblk * (1.0 / 256.0))
        lo = nblk - 256.0 * hi
        ut = ut_ref[...]
        hi8 = jnp.broadcast_to(hi, (8, LANES)).astype(BF16)
        lo8 = jnp.broadcast_to(lo, (8, LANES)).astype(BF16)
        pre = (256.0 * jnp.dot(hi8, ut, preferred_element_type=F32)
               + jnp.dot(lo8, ut, preferred_element_type=F32))
        pst_ref[...] = pre[0:1] * float(ROW_BLOCK)
        carry_ref[...] = jnp.zeros_like(carry_ref)

    hit1, hit2, w1, w2 = _route_topk(lg_ref[...] + bias_ref[...], tm)
    oh = jnp.where(hit1 | hit2, 1.0, 0.0)
    cum = jnp.dot(ls_ref[...], oh.astype(BF16), preferred_element_type=F32)
    slot = cum + carry_ref[...] + pst_ref[...]
    carry_ref[...] += jnp.sum(oh, axis=0, keepdims=True)
    p1 = jnp.sum(jnp.where(hit1, slot, 0.0), axis=-1, keepdims=True)
    p2 = jnp.sum(jnp.where(hit2, slot, 0.0), axis=-1, keepdims=True)
    pos_ref[...] = jnp.where(lane == 0, p1, jnp.where(lane == 1, p2, 0.0)).astype(I32)
    wt_ref[...] = jnp.where(lane == 0, w1, jnp.where(lane == 1, w2, 0.0))


def _route(logits, bias):
    t = logits.shape[0]
    tm = min(TM_ROUTE, t)
    r = jnp.arange(tm)
    ls = jnp.where(r[:, None] > r[None, :], 1.0, 0.0).astype(BF16)
    e = jnp.arange(LANES)
    ut = jnp.where(e[:, None] < e[None, :], 1.0, 0.0).astype(BF16)
    tok = lambda i: (i, 0)
    const = lambda i: (0, 0)
    cnt = pl.pallas_call(
        functools.partial(_count_body, tm=tm),
        grid=(t // tm,),
        in_specs=[pl.BlockSpec((tm, LANES), tok), pl.BlockSpec((1, LANES), const)],
        out_specs=pl.BlockSpec((8, LANES), const),
        out_shape=jax.ShapeDtypeStruct((8, LANES), F32),
        compiler_params=_params(("arbitrary",)),
        name="route_count",
    )(logits, bias)
    pos, wts = pl.pallas_call(
        functools.partial(_assign_body, tm=tm),
        grid=(t // tm,),
        in_specs=[pl.BlockSpec((tm, LANES), tok),
                  pl.BlockSpec((1, LANES), const),
                  pl.BlockSpec((8, LANES), const),
                  pl.BlockSpec((tm, tm), const),
                  pl.BlockSpec((LANES, LANES), const)],
        out_specs=[pl.BlockSpec((tm, LANES), tok), pl.BlockSpec((tm, LANES), tok)],
        out_shape=[jax.ShapeDtypeStruct((t, LANES), I32), jax.ShapeDtypeStruct((t, LANES), F32)],
        scratch_shapes=[pltpu.VMEM((1, LANES), F32), pltpu.VMEM((1, LANES), F32)],
        compiler_params=_params(("arbitrary",)),
        name="route_assign",
    )(logits, bias, cnt, ls, ut)
    return pos, wts, cnt


def _row_copy(src, s, dst, d, sem):
    return pltpu.make_async_copy(src.at[s], dst.at[d], sem)


def _expert_body(be_ref, nu_ref, cur_ref, nxt_ref, u_ref, wg_ref, wu_ref, wd_ref, y_ref, xbuf, sem, *, n_blocks):
    i = pl.program_id(0)
    slot = i % 2

    used = i < nu_ref[0]
    part_rows = ROW_BLOCK // EXPERT_PARTS

    def gather(tok_ref, dst, dsem, part):
        def issue(p, carry):
            for k in range(2):
                r = part * part_rows + 2 * p + k
                _row_copy(u_ref, tok_ref[0, 0, r], dst, r, dsem).start(priority=k)
            return carry
        lax.fori_loop(0, part_rows // 2, issue, 0, unroll=4)

    @pl.when((i == 0) & used)
    def _():
        for part in range(EXPERT_PARTS):
            gather(cur_ref, xbuf.at[0], sem.at[0], part)

    @pl.when(used)
    def _():
        def drain(r, carry):
            _row_copy(u_ref, 0, xbuf.at[slot], 0, sem.at[slot]).wait()
            return carry
        lax.fori_loop(0, ROW_BLOCK, drain, 0, unroll=8)

        x = xbuf[slot].reshape(ROW_BLOCK, D)
        hw = EXPERT_HIDDEN // EXPERT_PARTS
        y = None
        for part in range(EXPERT_PARTS):
            @pl.when(i + 1 < nu_ref[0])
            def _():
                gather(nxt_ref, xbuf.at[1 - slot], sem.at[1 - slot], part)

            cols = slice(part * hw, (part + 1) * hw)
            a = jnp.dot(x, wg_ref[0, :, cols], preferred_element_type=F32)
            b = jnp.dot(x, wu_ref[0, :, cols], preferred_element_type=F32)
            hdn = (a * _sigmoid(a) * b).astype(BF16)
            yp = jnp.dot(hdn, wd_ref[0, cols, :], preferred_element_type=F32)
            y = yp if y is None else y + yp
        y_ref[...] = y.astype(BF16).reshape(y_ref.shape)

    @pl.when(jnp.logical_not(used))
    def _():
        y_ref[...] = jnp.zeros_like(y_ref)


def _experts(u3, slot_tok, blk_e, n_used, w_gate, w_up, w_down):
    n_rows = slot_tok.shape[0]
    nb = n_rows // ROW_BLOCK
    rows = lambda i, be, nu: (i, 0, 0)
    wsel = lambda i, be, nu: (be[i], 0, 0)
    toks = slot_tok.reshape(nb, 1, ROW_BLOCK)
    grid_spec = pltpu.PrefetchScalarGridSpec(
        num_scalar_prefetch=2,
        grid=(nb,),
        in_specs=[pl.BlockSpec((1, 1, ROW_BLOCK), rows, memory_space=pltpu.SMEM),
                  pl.BlockSpec((1, 1, ROW_BLOCK), lambda i, be, nu: (jnp.minimum(i + 1, nb - 1), 0, 0),
                               memory_space=pltpu.SMEM),
                  pl.BlockSpec(memory_space=pl.ANY),
                  pl.BlockSpec((1, D, EXPERT_HIDDEN), wsel),
                  pl.BlockSpec((1, D, EXPERT_HIDDEN), wsel),
                  pl.BlockSpec((1, EXPERT_HIDDEN, D), wsel)],
        out_specs=pl.BlockSpec((ROW_BLOCK, ROW_SUB, LANES), rows),
        scratch_shapes=[pltpu.VMEM((2, ROW_BLOCK, ROW_SUB, LANES), BF16), pltpu.SemaphoreType.DMA((2,))],
    )
    return pl.pallas_call(
        functools.partial(_expert_body, n_blocks=nb),
        grid_spec=grid_spec,
        out_shape=jax.ShapeDtypeStruct((n_rows, ROW_SUB, LANES), BF16),
        compiler_params=_params(("arbitrary",)),
        name="experts",
    )(blk_e, n_used, toks, toks, u3, w_gate, w_up, w_down)


def _combine_body(cur_ref, nxt_ref, yb_ref, wt_ref, h_ref, g2_ref, nf_ref, o_ref, gbuf, sem, *, tm, n_steps):
    i = pl.program_id(0)
    slot = i % 2

    def gather(pos_ref, s):
        def issue(r, carry):
            for k in range(2):
                _row_copy(yb_ref, pos_ref[0, 0, 2 * r + k], gbuf.at[s, k], r, sem.at[s]).start(priority=k)
            return carry
        lax.fori_loop(0, tm, issue, 0, unroll=4)

    @pl.when(i == 0)
    def _():
        gather(cur_ref, 0)

    @pl.when(i + 1 < n_steps)
    def _():
        gather(nxt_ref, 1 - slot)

    def drain(r, carry):
        for k in range(2):
            _row_copy(yb_ref, 0, gbuf.at[slot, k], 0, sem.at[slot]).wait()
        return carry
    lax.fori_loop(0, tm, drain, 0, unroll=4)

    wt = wt_ref[...]
    ya = gbuf[slot, 0].reshape(tm, D).astype(F32)
    yb = gbuf[slot, 1].reshape(tm, D).astype(F32)
    y = wt[:, 0:1] * ya + wt[:, 1:2] * yb
    h = h_ref[...] + g2_ref[0] * y
    ms = jnp.mean(h * h, axis=-1, keepdims=True)
    o_ref[...] = h * lax.rsqrt(ms + EPS) * nf_ref[...]


def _combine(pos2, yb, wts, h1, gate2, norm_final, seq):
    t = h1.shape[0]
    tm = min(TM_ROWS, seq)
    per_seq = seq // tm
    tok = lambda i: (i, 0)
    n_steps = t // tm
    pos3 = pos2.reshape(n_steps, 1, 2 * tm)
    return pl.pallas_call(
        functools.partial(_combine_body, tm=tm, n_steps=n_steps),
        grid=(n_steps,),
        in_specs=[pl.BlockSpec((1, 1, 2 * tm), lambda i: (i, 0, 0), memory_space=pltpu.SMEM),
                  pl.BlockSpec((1, 1, 2 * tm), lambda i: (jnp.minimum(i + 1, n_steps - 1), 0, 0),
                               memory_space=pltpu.SMEM),
                  pl.BlockSpec(memory_space=pl.ANY),
                  pl.BlockSpec((tm, LANES), tok),
                  pl.BlockSpec((tm, D), tok),
                  pl.BlockSpec((1, 1, D), lambda i: (i // per_seq, 0, 0)),
                  pl.BlockSpec((1, D), lambda i: (0, 0))],
        out_specs=pl.BlockSpec((tm, D), tok),
        out_shape=jax.ShapeDtypeStruct((t, D), F32),
        scratch_shapes=[pltpu.VMEM((2, 2, tm, ROW_SUB, LANES), BF16), pltpu.SemaphoreType.DMA((2,))],
        compiler_params=_params(("arbitrary",)),
        name="combine",
    )(pos3, pos3, yb, wts, h1, gate2, norm_final)


def _layer(h2d, c, bsz, seq, w_ada, b_ada, norm_mix, w_in, w_gk2, b_gk2, gla_norm, w_a, w_pool, pool_scale, w_b,
           w_out, norm_ffn, w_rg, b_rg, w_re, b_re, w_gate, w_up, w_down):
    t = bsz * seq
    mod = _ada(c, w_ada, b_ada)
    shift1, scale1, gate1, shift2, scale2, gate2 = [m.reshape(bsz, 1, D) for m in jnp.split(mod, 6, axis=-1)]

    c_gk = 2 * KEY_DIM + 2 * VAL_DIM
    c_p = c_gk + GATE_RANK
    c_ga = c_p + POOL_DIM
    w_main = jnp.concatenate([w_in[:, :c_gk], w_in[:, c_ga:], w_in[:, c_p:c_ga]], axis=1).astype(BF16)
    w_gk = jnp.zeros((D, LANES), F32).at[:, :GATE_RANK].set(w_in[:, c_gk:c_p]).astype(BF16)

    z, gk, wg16, wu16, wd16 = _inproj(h2d, norm_mix.reshape(1, D), scale1, shift1, w_main, w_gk,
                                      w_gate, w_up, w_down, seq)
    o_gla = _gla(z, gk, w_gk2, b_gk2, gla_norm, bsz, seq)
    cb = _pool(z, w_pool, pool_scale, w_b, seq)

    w_router = jnp.zeros((D, LANES), F32).at[:, :N_GROUPS].set(w_rg)
    w_router = w_router.at[:, N_GROUPS:N_GROUPS + N_EXPERTS].set(w_re).astype(BF16)
    h1, u2, logits = _mix(o_gla, z, cb, h2d, gate1, norm_ffn.reshape(1, D), scale2, shift2,
                          w_a.astype(BF16), w_out.astype(BF16), w_router, seq)

    bias = jnp.zeros((1, LANES), F32).at[0, :N_GROUPS].set(b_rg).at[0, N_GROUPS:N_GROUPS + N_EXPERTS].set(b_re)
    pos, wts, cnt = _route(logits, bias)
    pos2 = pos[:, :2]

    counts = cnt[0, :N_EXPERTS].astype(I32)
    nblk = (counts + ROW_BLOCK - 1) // ROW_BLOCK
    bends = jnp.cumsum(nblk)
    n_rows = 2 * t + N_EXPERTS * ROW_BLOCK
    n_blocks = n_rows // ROW_BLOCK
    blk_e = jnp.sum(bends[None, :] <= jnp.arange(n_blocks, dtype=I32)[:, None], axis=1)
    blk_e = jnp.minimum(blk_e, N_EXPERTS - 1).astype(I32)
    n_used = bends[-1:].astype(I32)
    slot_tok = jnp.zeros((n_rows,), I32).at[pos2.reshape(-1)].set(
        jnp.arange(2 * t, dtype=I32) // 2, unique_indices=True)

    yb = _experts(u2, slot_tok, blk_e, n_used, wg16, wu16, wd16)
    return pos2, yb, wts, h1, gate2


def kernel(x, c, w_ada, b_ada, norm_mix, w_in, w_gk2, b_gk2, gla_norm, w_a, w_pool, pool_scale, w_b, w_out, norm_ffn, w_rg, b_rg, w_re, b_re, w_gate, w_up, w_down, norm_final):
    bsz, seq, _ = x.shape
    depth = w_ada.shape[0]
    assert depth == 1
    h2d = x.reshape(bsz * seq, D)
    pos2, yb, wts, h1, gate2 = _layer(
        h2d, c, bsz, seq, w_ada[0], b_ada[0], norm_mix[0], w_in[0], w_gk2[0], b_gk2[0], gla_norm[0], w_a[0],
        w_pool[0], pool_scale[0], w_b[0], w_out[0], norm_ffn[0], w_rg[0], b_rg[0], w_re[0], b_re[0],
        w_gate[0], w_up[0], w_down[0])
    out = _combine(pos2, yb, wts, h1, gate2, norm_final.reshape(1, D), seq)
    return out.reshape(bsz, seq, D)
```

```python
import functools

import jax
import jax.numpy as jnp
from jax import lax
from jax.experimental import pallas as pl
from jax.experimental.pallas import tpu as pltpu

F32 = jnp.float32
BF16 = jnp.bfloat16
I32 = jnp.int32

D = 2048
HEADS = 4
DK = 256
DV = 512
KEY_DIM = HEADS * DK
VAL_DIM = HEADS * DV
GATE_RANK = 16
GATE_NORMALIZER = 16.0
CHUNK = 64
POOL_WINDOWS = (2, 4, 8, 16)
POOL_DIM = 1024
POOL_GROUP_DIM = 256
POOL_HALO = 16
N_GROUPS = 8
EXPERTS_PER_GROUP = 8
N_EXPERTS = 64
EXPERT_HIDDEN = 1024
EPS = 1e-6
LANES = 128
ROW_SUB = D // LANES

Z_COLS = 2 * KEY_DIM + 2 * VAL_DIM + 2 * D + POOL_DIM

VMEM_LIMIT = 56 * 1024 * 1024
VMEM_LIMIT_INPROJ = 60 * 1024 * 1024
CAST_STEPS = 8

TM_IN = 1024
TN_IN = 1024
TC_GLA = 512
TM_POOL = 256
TM_MIX = 256
TM_ROUTE = 512
TM_ROWS = 256
ROW_BLOCK = 256


def _sigmoid(x):
    return 1.0 / (1.0 + jnp.exp(-x))


def _params(sem, vmem=VMEM_LIMIT):
    return pltpu.CompilerParams(dimension_semantics=sem, vmem_limit_bytes=vmem)


def _ada_body(c_ref, w_ref, b_ref, o_ref):
    c = c_ref[...]
    s = (c * _sigmoid(c)).astype(BF16)
    o_ref[...] = jnp.dot(s, w_ref[...].astype(BF16), preferred_element_type=F32) + b_ref[...]


def _ada(c, w, b):
    bsz = c.shape[0]
    cp = jnp.zeros((8, D), F32).at[:bsz].set(c)
    n = w.shape[1]
    tn = 1024
    out = pl.pallas_call(
        _ada_body,
        grid=(n // tn,),
        in_specs=[pl.BlockSpec((8, D), lambda j: (0, 0)),
                  pl.BlockSpec((D, tn), lambda j: (0, j)),
                  pl.BlockSpec((1, tn), lambda j: (0, j))],
        out_specs=pl.BlockSpec((8, tn), lambda j: (0, j)),
        out_shape=jax.ShapeDtypeStruct((8, n), F32),
        compiler_params=_params(("arbitrary",)),
        name="ada",
    )(cp, w, b.reshape(1, n))
    return out[:bsz]


def _inproj_body(x_ref, g_ref, sc_ref, sh_ref, w_ref, wgk_ref, eg_ref, eu_ref, ed_ref,
                 z_ref, gk_ref, og_ref, ou_ref, od_ref, u_scr):
    @pl.when(pl.program_id(1) < CAST_STEPS)
    def _():
        og_ref[...] = eg_ref[...].astype(BF16)
        ou_ref[...] = eu_ref[...].astype(BF16)
        od_ref[...] = ed_ref[...].astype(BF16)

    @pl.when(pl.program_id(1) == 0)
    def _():
        x = x_ref[...]
        ms = jnp.mean(x * x, axis=-1, keepdims=True)
        u = x * lax.rsqrt(ms + EPS) * g_ref[...]
        u = (u * (1.0 + sc_ref[0]) + sh_ref[0]).astype(BF16)
        u_scr[...] = u
        gk_ref[...] = jnp.dot(u, wgk_ref[...], preferred_element_type=F32)

    z_ref[...] = jnp.dot(u_scr[...], w_ref[...], preferred_element_type=F32).astype(BF16)


def _inproj(x2, gain, scale, shift, w_main, w_gk, w_gate, w_up, w_down, seq):
    t = x2.shape[0]
    tm = min(TM_IN, seq)
    per_seq = seq // tm
    n_i = t // tm
    n_j = Z_COLS // TN_IN
    assert CAST_STEPS <= n_j
    n_slabs = n_i * CAST_STEPS
    eg = w_gate.reshape(N_EXPERTS * D, EXPERT_HIDDEN)
    eu = w_up.reshape(N_EXPERTS * D, EXPERT_HIDDEN)
    ed = w_down.reshape(N_EXPERTS * EXPERT_HIDDEN, D)
    rg, rd = eg.shape[0] // n_slabs, ed.shape[0] // n_slabs
    assert rg * n_slabs == eg.shape[0] and rd * n_slabs == ed.shape[0] and rd % 16 == 0
    slab = lambda i, j: (i * CAST_STEPS + jnp.minimum(j, CAST_STEPS - 1), 0)
    z, gk, og, ou, od = pl.pallas_call(
        _inproj_body,
        grid=(n_i, n_j),
        in_specs=[pl.BlockSpec((tm, D), lambda i, j: (i, 0)),
                  pl.BlockSpec((1, D), lambda i, j: (0, 0)),
                  pl.BlockSpec((1, 1, D), lambda i, j: (i // per_seq, 0, 0)),
                  pl.BlockSpec((1, 1, D), lambda i, j: (i // per_seq, 0, 0)),
                  pl.BlockSpec((D, TN_IN), lambda i, j: (0, j)),
                  pl.BlockSpec((D, LANES), lambda i, j: (0, 0)),
                  pl.BlockSpec((rg, EXPERT_HIDDEN), slab),
                  pl.BlockSpec((rg, EXPERT_HIDDEN), slab),
                  pl.BlockSpec((rd, D), slab)],
        out_specs=[pl.BlockSpec((tm, TN_IN), lambda i, j: (i, j)),
                   pl.BlockSpec((tm, LANES), lambda i, j: (i, 0)),
                   pl.BlockSpec((rg, EXPERT_HIDDEN), slab),
                   pl.BlockSpec((rg, EXPERT_HIDDEN), slab),
                   pl.BlockSpec((rd, D), slab)],
        out_shape=[jax.ShapeDtypeStruct((t, Z_COLS), BF16),
                   jax.ShapeDtypeStruct((t, LANES), F32),
                   jax.ShapeDtypeStruct(eg.shape, BF16),
                   jax.ShapeDtypeStruct(eu.shape, BF16),
                   jax.ShapeDtypeStruct(ed.shape, BF16)],
        scratch_shapes=[pltpu.VMEM((tm, D), BF16)],
        compiler_params=_params(("arbitrary", "arbitrary"), vmem=VMEM_LIMIT_INPROJ),
        name="inproj",
    )(x2, gain, scale, shift, w_main, w_gk, eg, eu, ed)
    return (z, gk, og.reshape(w_gate.shape), ou.reshape(w_up.shape), od.reshape(w_down.shape))


def _gla_body(q_ref, k_ref, v_ref, g_ref, gk_ref, wgk_ref, bgk_ref, gn_ref, ll_ref, o_ref, st_ref, *, n_chunks):
    @pl.when(pl.program_id(1) == 0)
    def _():
        st_ref[...] = jnp.zeros_like(st_ref)

    row = lax.broadcasted_iota(I32, (CHUNK, CHUNK), 0)
    col = lax.broadcasted_iota(I32, (CHUNK, CHUNK), 1)
    causal = row >= col
    nt = (((1,), (1,)), ((), ()))
    tn = (((0,), (0,)), ((), ()))

    def chunk(ci, carry):
        r0 = pl.multiple_of(ci * CHUNK, CHUNK)
        rows = pl.ds(r0, CHUNK)
        gk = gk_ref[rows, :].astype(BF16)
        xg = jnp.dot(gk, wgk_ref[...], preferred_element_type=F32) + bgk_ref[...]
        la = (jnp.minimum(xg, 0.0) - jnp.log1p(jnp.exp(-jnp.abs(xg)))) * (1.0 / GATE_NORMALIZER)
        la_hi = la.astype(BF16)
        la_lo = (la - la_hi.astype(F32)).astype(BF16)
        ll = ll_ref[...]
        bb = (jnp.dot(ll, la_hi, preferred_element_type=F32)
              + jnp.dot(ll, la_lo, preferred_element_type=F32))
        b = bb[:CHUNK]
        bl = bb[CHUNK:]
        q = q_ref[rows, :].astype(F32)
        k = k_ref[rows, :].astype(F32)
        qe = (q * (DK ** -0.5) * jnp.exp(b)).astype(BF16)
        ke = (k * jnp.exp(-b)).astype(BF16)
        kd = (k * jnp.exp(bl - b)).astype(BF16)
        dec = jnp.exp(bl[0:1, :])
        for h in range(HEADS):
            sk = slice(h * DK, (h + 1) * DK)
            sv = slice(h * DV, (h + 1) * DV)
            att = lax.dot_general(qe[:, sk], ke[:, sk], nt, preferred_element_type=F32)
            att = jnp.where(causal, att, 0.0).astype(BF16)
            vc = v_ref[rows, sv]
            st = st_ref[h]
            o = (jnp.dot(att, vc, preferred_element_type=F32)
                 + lax.dot_general(qe[:, sk], st.astype(BF16), nt, preferred_element_type=F32))
            st_ref[h] = st * dec[:, sk] + lax.dot_general(vc, kd[:, sk], tn, preferred_element_type=F32)
            ms = jnp.mean(o * o, axis=-1, keepdims=True)
            on = o * lax.rsqrt(ms + EPS) * gn_ref[...]
            gg = g_ref[rows, sv].astype(F32)
            o_ref[rows, sv] = (on * (gg * _sigmoid(gg))).astype(BF16)
        return carry

    lax.fori_loop(0, n_chunks, chunk, 0, unroll=2)


def _gla(z, gk, w_gk2, b_gk2, gla_norm, bsz, seq):
    t = z.shape[0]
    tc = min(TC_GLA, seq)
    per_seq = seq // tc
    wgk = jnp.zeros((LANES, KEY_DIM), F32).at[:GATE_RANK].set(w_gk2).astype(BF16)
    r = jnp.arange(2 * CHUNK)[:, None]
    c = jnp.arange(CHUNK)[None, :]
    ll = jnp.where((r >= CHUNK) | (r >= c), 1.0, 0.0).astype(BF16)
    kb = KEY_DIM // KEY_DIM
    return pl.pallas_call(
        functools.partial(_gla_body, n_chunks=tc // CHUNK),
        grid=(bsz, per_seq),
        in_specs=[pl.BlockSpec((tc, KEY_DIM), lambda b, s: (b * per_seq + s, 0)),
                  pl.BlockSpec((tc, KEY_DIM), lambda b, s: (b * per_seq + s, kb)),
                  pl.BlockSpec((tc, VAL_DIM), lambda b, s: (b * per_seq + s, 1)),
                  pl.BlockSpec((tc, VAL_DIM), lambda b, s: (b * per_seq + s, 2)),
                  pl.BlockSpec((tc, LANES), lambda b, s: (b * per_seq + s, 0)),
                  pl.BlockSpec((LANES, KEY_DIM), lambda b, s: (0, 0)),
                  pl.BlockSpec((1, KEY_DIM), lambda b, s: (0, 0)),
                  pl.BlockSpec((1, DV), lambda b, s: (0, 0)),
                  pl.BlockSpec((2 * CHUNK, CHUNK), lambda b, s: (0, 0))],
        out_specs=pl.BlockSpec((tc, VAL_DIM), lambda b, s: (b * per_seq + s, 0)),
        out_shape=jax.ShapeDtypeStruct((t, VAL_DIM), BF16),
        scratch_shapes=[pltpu.VMEM((HEADS, DV, DK), F32)],
        compiler_params=_params(("arbitrary", "arbitrary")),
        name="gla",
    )(z, z, z, z, gk, wgk, b_gk2.reshape(1, KEY_DIM), gla_norm.reshape(1, DV), ll)


def _pool_body(p_ref, halo_ref, gb_ref, wp_ref, ps_ref, wb_ref, o_ref, *, tm, seq):
    base = (pl.program_id(0) * tm) % seq
    pf = p_ref[...].astype(F32)
    hal = halo_ref[...].astype(F32)
    hal = jnp.where(base == 0, 0.0, hal)
    ext = jnp.concatenate([hal, pf], axis=0)
    pos = base + lax.broadcasted_iota(I32, (tm, 1), 0)
    ys = []
    for gi, w in enumerate(POOL_WINDOWS):
        cols = slice(gi * POOL_GROUP_DIM, (gi + 1) * POOL_GROUP_DIM)
        s = ext[:, cols]
        sh = 1
        while sh < w:
            s = s + pltpu.roll(s, sh, 0)
            sh *= 2
        cnt = jnp.minimum(pos + 1, w).astype(F32)
        m = s[POOL_HALO:] / cnt - pf[:, cols]
        ys.append(jnp.dot(m.astype(BF16), wp_ref[gi], preferred_element_type=F32))
    y = jnp.concatenate(ys, axis=1) * ps_ref[...]
    yb = jnp.dot(y.astype(BF16), wb_ref[...], preferred_element_type=F32)
    o_ref[...] = (yb * _sigmoid(gb_ref[...].astype(F32))).astype(BF16)


def _pool(z, w_pool, pool_scale, w_b, seq):
    t = z.shape[0]
    tm = min(TM_POOL, seq)
    p_blk = (Z_COLS - POOL_DIM) // POOL_DIM
    gb_blk = (2 * KEY_DIM + 2 * VAL_DIM + D) // D
    hb = tm // POOL_HALO
    return pl.pallas_call(
        functools.partial(_pool_body, tm=tm, seq=seq),
        grid=(t // tm,),
        in_specs=[pl.BlockSpec((tm, POOL_DIM), lambda i: (i, p_blk)),
                  pl.BlockSpec((POOL_HALO, POOL_DIM), lambda i: (jnp.maximum(i * hb - 1, 0), p_blk)),
                  pl.BlockSpec((tm, D), lambda i: (i, gb_blk)),
                  pl.BlockSpec((len(POOL_WINDOWS), POOL_GROUP_DIM, POOL_GROUP_DIM), lambda i: (0, 0, 0)),
                  pl.BlockSpec((1, POOL_DIM), lambda i: (0, 0)),
                  pl.BlockSpec((POOL_DIM, D), lambda i: (0, 0))],
        out_specs=pl.BlockSpec((tm, D), lambda i: (i, 0)),
        out_shape=jax.ShapeDtypeStruct((t, D), BF16),
        compiler_params=_params(("arbitrary",)),
        name="pool",
    )(z, z, z, w_pool.astype(BF16), pool_scale.reshape(1, POOL_DIM), w_b.astype(BF16))


def _mix_body(o_ref, ga_ref, cb_ref, x_ref, g1_ref, nf_ref, sc_ref, sh_ref, wa_ref, wo_ref, wr_ref,
              h_ref, u_ref, lg_ref):
    ya = jnp.dot(o_ref[...], wa_ref[...], preferred_element_type=F32)
    merged = ya * _sigmoid(ga_ref[...].astype(F32)) + cb_ref[...].astype(F32)
    mo = jnp.dot(merged.astype(BF16), wo_ref[...], preferred_element_type=F32)
    h = x_ref[...] + g1_ref[0] * mo
    h_ref[...] = h
    ms = jnp.mean(h * h, axis=-1, keepdims=True)
    u = h * lax.rsqrt(ms + EPS) * nf_ref[...]
    u = u * (1.0 + sc_ref[0]) + sh_ref[0]
    ub = u.astype(BF16)
    u_ref[...] = ub.reshape(u_ref.shape)
    lg_ref[...] = jnp.dot(ub, wr_ref[...], preferred_element_type=F32)


def _mix(o_gla, z, cb, x2, gate1, norm_ffn, scale2, shift2, w_a, w_out, w_router, seq):
    t = x2.shape[0]
    tm = min(TM_MIX, seq)
    per_seq = seq // tm
    ga_blk = (2 * KEY_DIM + 2 * VAL_DIM) // D
    tok = lambda i: (i, 0)
    const = lambda i: (0, 0)
    per_b = lambda i: (i // per_seq, 0, 0)
    return pl.pallas_call(
        _mix_body,
        grid=(t // tm,),
        in_specs=[pl.BlockSpec((tm, VAL_DIM), tok),
                  pl.BlockSpec((tm, D), lambda i: (i, ga_blk)),
                  pl.BlockSpec((tm, D), tok),
                  pl.BlockSpec((tm, D), tok),
                  pl.BlockSpec((1, 1, D), per_b),
                  pl.BlockSpec((1, D), const),
                  pl.BlockSpec((1, 1, D), per_b),
                  pl.BlockSpec((1, 1, D), per_b),
                  pl.BlockSpec((VAL_DIM, D), const, pipeline_mode=pl.Buffered(1)),
                  pl.BlockSpec((D, D), const, pipeline_mode=pl.Buffered(1)),
                  pl.BlockSpec((D, LANES), const)],
        out_specs=[pl.BlockSpec((tm, D), tok),
                   pl.BlockSpec((tm, ROW_SUB, LANES), lambda i: (i, 0, 0)),
                   pl.BlockSpec((tm, LANES), tok)],
        out_shape=[jax.ShapeDtypeStruct((t, D), F32),
                   jax.ShapeDtypeStruct((t, ROW_SUB, LANES), BF16),
                   jax.ShapeDtypeStruct((t, LANES), F32)],
        compiler_params=_params(("arbitrary",)),
        name="mix",
    )(o_gla, z, cb, x2, gate1, norm_ffn, scale2, shift2, w_a, w_out, w_router)


def _route_topk(gl, tm):
    lane = lax.broadcasted_iota(I32, (tm, LANES), 1)
    lanef = lane.astype(F32)
    neg = float("-inf")
    big = float(LANES)
    gmask = lane < N_GROUPS
    gmax = jnp.max(jnp.where(gmask, gl, neg), axis=-1, keepdims=True)
    gsel = jnp.min(jnp.where(gmask & (gl == gmax), lanef, big), axis=-1, keepdims=True)
    gsum = jnp.sum(jnp.where(gmask, jnp.exp(gl - gmax), 0.0), axis=-1, keepdims=True)
    pg = 1.0 / gsum
    lo = N_GROUPS + gsel * EXPERTS_PER_GROUP
    emask = (lanef >= lo) & (lanef < lo + EXPERTS_PER_GROUP)
    emax = jnp.max(jnp.where(emask, gl, neg), axis=-1, keepdims=True)
    ee = jnp.where(emask, jnp.exp(gl - emax), 0.0)
    eprob = ee / jnp.sum(ee, axis=-1, keepdims=True)
    v1 = jnp.max(jnp.where(emask, eprob, -1.0), axis=-1, keepdims=True)
    i1 = jnp.min(jnp.where(emask & (eprob == v1), lanef, big), axis=-1, keepdims=True)
    m2 = emask & (lanef != i1)
    v2 = jnp.max(jnp.where(m2, eprob, -1.0), axis=-1, keepdims=True)
    i2 = jnp.min(jnp.where(m2 & (eprob == v2), lanef, big), axis=-1, keepdims=True)
    tot = v1 + v2
    w1 = pg * (v1 / tot)
    w2 = pg * (v2 / tot)
    hit1 = lanef == i1 - N_GROUPS
    hit2 = lanef == i2 - N_GROUPS
    return hit1, hit2, w1, w2


def _count_body(lg_ref, bias_ref, cnt_ref, *, tm):
    @pl.when(pl.program_id(0) == 0)
    def _():
        cnt_ref[...] = jnp.zeros_like(cnt_ref)

    hit1, hit2, _, _ = _route_topk(lg_ref[...] + bias_ref[...], tm)
    oh = jnp.where(hit1 | hit2, 1.0, 0.0)
    cnt_ref[...] += jnp.sum(oh, axis=0, keepdims=True)


def _assign_body(lg_ref, bias_ref, cnt_ref, ls_ref, ut_ref, pos_ref, wt_ref, carry_ref, pst_ref, *, tm):
    lane = lax.broadcasted_iota(I32, (tm, LANES), 1)

    @pl.when(pl.program_id(0) == 0)
    def _():
        cnt = cnt_ref[0:1, :]
        nblk = jnp.floor((cnt + (ROW_BLOCK - 1)) * (1.0 / ROW_BLOCK))
        hi = jnp.floor(nblk * (1.0 / 256.0))
        lo = nblk - 256.0 * hi
        ut = ut_ref[...]
        hi8 = jnp.broadcast_to(hi, (8, LANES)).astype(BF16)
        lo8 = jnp.broadcast_to(lo, (8, LANES)).astype(BF16)
        pre = (256.0 * jnp.dot(hi8, ut, preferred_element_type=F32)
               + jnp.dot(lo8, ut, preferred_element_type=F32))
        pst_ref[...] = pre[0:1] * float(ROW_BLOCK)
        carry_ref[...] = jnp.zeros_like(carry_ref)

    hit1, hit2, w1, w2 = _route_topk(lg_ref[...] + bias_ref[...], tm)
    oh = jnp.where(hit1 | hit2, 1.0, 0.0)
    cum = jnp.dot(ls_ref[...], oh.astype(BF16), preferred_element_type=F32)
    slot = cum + carry_ref[...] + pst_ref[...]
    carry_ref[...] += jnp.sum(oh, axis=0, keepdims=True)
    p1 = jnp.sum(jnp.where(hit1, slot, 0.0), axis=-1, keepdims=True)
    p2 = jnp.sum(jnp.where(hit2, slot, 0.0), axis=-1, keepdims=True)
    pos_ref[...] = jnp.where(lane == 0, p1, jnp.where(lane == 1, p2, 0.0)).astype(I32)
    wt_ref[...] = jnp.where(lane == 0, w1, jnp.where(lane == 1, w2, 0.0))


def _route(logits, bias):
    t = logits.shape[0]
    tm = min(TM_ROUTE, t)
    r = jnp.arange(tm)
    ls = jnp.where(r[:, None] > r[None, :], 1.0, 0.0).astype(BF16)
    e = jnp.arange(LANES)
    ut = jnp.where(e[:, None] < e[None, :], 1.0, 0.0).astype(BF16)
    tok = lambda i: (i, 0)
    const = lambda i: (0, 0)
    cnt = pl.pallas_call(
        functools.partial(_count_body, tm=tm),
        grid=(t // tm,),
        in_specs=[pl.BlockSpec((tm, LANES), tok), pl.BlockSpec((1, LANES), const)],
        out_specs=pl.BlockSpec((8, LANES), const),
        out_shape=jax.ShapeDtypeStruct((8, LANES), F32),
        compiler_params=_params(("arbitrary",)),
        name="route_count",
    )(logits, bias)
    pos, wts = pl.pallas_call(
        functools.partial(_assign_body, tm=tm),
        grid=(t // tm,),
        in_specs=[pl.BlockSpec((tm, LANES), tok),
                  pl.BlockSpec((1, LANES), const),
                  pl.BlockSpec((8, LANES), const),
                  pl.BlockSpec((tm, tm), const),
                  pl.BlockSpec((LANES, LANES), const)],
        out_specs=[pl.BlockSpec((tm, LANES), tok), pl.BlockSpec((tm, LANES), tok)],
        out_shape=[jax.ShapeDtypeStruct((t, LANES), I32), jax.ShapeDtypeStruct((t, LANES), F32)],
        scratch_shapes=[pltpu.VMEM((1, LANES), F32), pltpu.VMEM((1, LANES), F32)],
        compiler_params=_params(("arbitrary",)),
        name="route_assign",
    )(logits, bias, cnt, ls, ut)
    return pos, wts, cnt


def _row_copy(src, s, dst, d, sem):
    return pltpu.make_async_copy(src.at[s], dst.at[d], sem)


def _list_copy(src, b, dst, s, sem):
    return pltpu.make_async_copy(src.at[b], dst.at[s], sem.at[s])


def _expert_body(be_ref, nu_ref, tok_ref, u_ref, wg_ref, wu_ref, wd_ref, y_ref, xbuf, tokbuf, sem, tsem):
    i = pl.program_id(0)
    nu = nu_ref[0]
    slot = i % 2
    used = i < nu

    def gather(s):
        def issue(p, carry):
            for k in range(2):
                r = 2 * p + k
                _row_copy(u_ref, tokbuf[s, 0, r], xbuf.at[s], r, sem.at[s]).start(priority=k)
            return carry
        lax.fori_loop(0, ROW_BLOCK // 2, issue, 0, unroll=4)

    @pl.when((i == 0) & used)
    def _():
        _list_copy(tok_ref, 0, tokbuf, 0, tsem).start()
        _list_copy(tok_ref, 0, tokbuf, 0, tsem).wait()
        gather(0)

        @pl.when(1 < nu)
        def _():
            _list_copy(tok_ref, 1, tokbuf, 1, tsem).start()

    @pl.when(used)
    def _():
        @pl.when(i + 1 < nu)
        def _():
            _list_copy(tok_ref, i + 1, tokbuf, 1 - slot, tsem).wait()
            gather(1 - slot)

        def drain(r, carry):
            _row_copy(u_ref, 0, xbuf.at[slot], 0, sem.at[slot]).wait()
            return carry
        lax.fori_loop(0, ROW_BLOCK, drain, 0, unroll=8)

        @pl.when(i + 2 < nu)
        def _():
            _list_copy(tok_ref, i + 2, tokbuf, slot, tsem).start()

        x = xbuf[slot].reshape(ROW_BLOCK, D)
        a = jnp.dot(x, wg_ref[0], preferred_element_type=F32)
        b = jnp.dot(x, wu_ref[0], preferred_element_type=F32)
        hdn = (a * _sigmoid(a) * b).astype(BF16)
        y = jnp.dot(hdn, wd_ref[0], preferred_element_type=F32)
        y_ref[...] = y.astype(BF16).reshape(y_ref.shape)

    @pl.when(jnp.logical_not(used))
    def _():
        y_ref[...] = jnp.zeros_like(y_ref)


def _experts(u3, slot_tok, blk_e, n_used, w_gate, w_up, w_down):
    n_rows = slot_tok.shape[0]
    nb = n_rows // ROW_BLOCK
    rows = lambda i, be, nu: (i, 0, 0)
    wsel = lambda i, be, nu: (be[i], 0, 0)
    grid_spec = pltpu.PrefetchScalarGridSpec(
        num_scalar_prefetch=2,
        grid=(nb,),
        in_specs=[pl.BlockSpec(memory_space=pl.ANY),
                  pl.BlockSpec(memory_space=pl.ANY),
                  pl.BlockSpec((1, D, EXPERT_HIDDEN), wsel),
                  pl.BlockSpec((1, D, EXPERT_HIDDEN), wsel),
                  pl.BlockSpec((1, EXPERT_HIDDEN, D), wsel)],
        out_specs=pl.BlockSpec((ROW_BLOCK, ROW_SUB, LANES), rows),
        scratch_shapes=[pltpu.VMEM((2, ROW_BLOCK, ROW_SUB, LANES), BF16),
                        pltpu.SMEM((2, 1, ROW_BLOCK), I32),
                        pltpu.SemaphoreType.DMA((2,)),
                        pltpu.SemaphoreType.DMA((2,))],
    )
    return pl.pallas_call(
        _expert_body,
        grid_spec=grid_spec,
        out_shape=jax.ShapeDtypeStruct((n_rows, ROW_SUB, LANES), BF16),
        compiler_params=_params(("arbitrary",)),
        name="experts",
    )(blk_e, n_used, slot_tok.reshape(nb, 1, ROW_BLOCK), u3, w_gate, w_up, w_down)


def _combine_body(pos_ref, yb_ref, wt_ref, h_ref, g2_ref, nf_ref, o_ref, gbuf, posbuf, sem, psem, *, tm, n_steps):
    i = pl.program_id(0)
    slot = i % 2

    def gather(s):
        def issue(r, carry):
            for k in range(2):
                _row_copy(yb_ref, posbuf[s, 0, 2 * r + k], gbuf.at[s, k], r, sem.at[s]).start(priority=k)
            return carry
        lax.fori_loop(0, tm, issue, 0, unroll=4)

    @pl.when(i == 0)
    def _():
        _list_copy(pos_ref, 0, posbuf, 0, psem).start()
        _list_copy(pos_ref, 0, posbuf, 0, psem).wait()
        gather(0)
        if n_steps > 1:
            _list_copy(pos_ref, 1, posbuf, 1, psem).start()

    @pl.when(i + 1 < n_steps)
    def _():
        _list_copy(pos_ref, i + 1, posbuf, 1 - slot, psem).wait()
        gather(1 - slot)

    def drain(r, carry):
        for k in range(2):
            _row_copy(yb_ref, 0, gbuf.at[slot, k], 0, sem.at[slot]).wait()
        return carry
    lax.fori_loop(0, tm, drain, 0, unroll=4)

    @pl.when(i + 2 < n_steps)
    def _():
        _list_copy(pos_ref, i + 2, posbuf, slot, psem).start()

    wt = wt_ref[...]
    ya = gbuf[slot, 0].reshape(tm, D).astype(F32)
    yb = gbuf[slot, 1].reshape(tm, D).astype(F32)
    y = wt[:, 0:1] * ya + wt[:, 1:2] * yb
    h = h_ref[...] + g2_ref[0] * y
    ms = jnp.mean(h * h, axis=-1, keepdims=True)
    o_ref[...] = h * lax.rsqrt(ms + EPS) * nf_ref[...]


def _combine(pos2, yb, wts, h1, gate2, norm_final, seq):
    t = h1.shape[0]
    tm = min(TM_ROWS, seq)
    per_seq = seq // tm
    tok = lambda i: (i, 0)
    n_steps = t // tm
    return pl.pallas_call(
        functools.partial(_combine_body, tm=tm, n_steps=n_steps),
        grid=(n_steps,),
        in_specs=[pl.BlockSpec(memory_space=pl.ANY),
                  pl.BlockSpec(memory_space=pl.ANY),
                  pl.BlockSpec((tm, LANES), tok),
                  pl.BlockSpec((tm, D), tok),
                  pl.BlockSpec((1, 1, D), lambda i: (i // per_seq, 0, 0)),
                  pl.BlockSpec((1, D), lambda i: (0, 0))],
        out_specs=pl.BlockSpec((tm, D), tok),
        out_shape=jax.ShapeDtypeStruct((t, D), F32),
        scratch_shapes=[pltpu.VMEM((2, 2, tm, ROW_SUB, LANES), BF16),
                        pltpu.SMEM((2, 1, 2 * tm), I32),
                        pltpu.SemaphoreType.DMA((2,)),
                        pltpu.SemaphoreType.DMA((2,))],
        compiler_params=_params(("arbitrary",)),
        name="combine",
    )(pos2.reshape(n_steps, 1, 2 * tm), yb, wts, h1, gate2, norm_final)


def _layer(h2d, c, bsz, seq, w_ada, b_ada, norm_mix, w_in, w_gk2, b_gk2, gla_norm, w_a, w_pool, pool_scale, w_b,
           w_out, norm_ffn, w_rg, b_rg, w_re, b_re, w_gate, w_up, w_down):
    t = bsz * seq
    mod = _ada(c, w_ada, b_ada)
    shift1, scale1, gate1, shift2, scale2, gate2 = [m.reshape(bsz, 1, D) for m in jnp.split(mod, 6, axis=-1)]

    c_gk = 2 * KEY_DIM + 2 * VAL_DIM
    c_p = c_gk + GATE_RANK
    c_ga = c_p + POOL_DIM
    w_main = jnp.concatenate([w_in[:, :c_gk], w_in[:, c_ga:], w_in[:, c_p:c_ga]], axis=1).astype(BF16)
    w_gk = jnp.zeros((D, LANES), F32).at[:, :GATE_RANK].set(w_in[:, c_gk:c_p]).astype(BF16)

    z, gk, wg16, wu16, wd16 = _inproj(h2d, norm_mix.reshape(1, D), scale1, shift1, w_main, w_gk,
                                      w_gate, w_up, w_down, seq)
    o_gla = _gla(z, gk, w_gk2, b_gk2, gla_norm, bsz, seq)
    cb = _pool(z, w_pool, pool_scale, w_b, seq)

    w_router = jnp.zeros((D, LANES), F32).at[:, :N_GROUPS].set(w_rg)
    w_router = w_router.at[:, N_GROUPS:N_GROUPS + N_EXPERTS].set(w_re).astype(BF16)
    h1, u2, logits = _mix(o_gla, z, cb, h2d, gate1, norm_ffn.reshape(1, D), scale2, shift2,
                          w_a.astype(BF16), w_out.astype(BF16), w_router, seq)

    bias = jnp.zeros((1, LANES), F32).at[0, :N_GROUPS].set(b_rg).at[0, N_GROUPS:N_GROUPS + N_EXPERTS].set(b_re)
    pos, wts, cnt = _route(logits, bias)
    pos2 = pos[:, :2]

    counts = cnt[0, :N_EXPERTS].astype(I32)
    nblk = (counts + ROW_BLOCK - 1) // ROW_BLOCK
    bends = jnp.cumsum(nblk)
    n_rows = 2 * t + N_EXPERTS * ROW_BLOCK
    n_blocks = n_rows // ROW_BLOCK
    blk_e = jnp.sum(bends[None, :] <= jnp.arange(n_blocks, dtype=I32)[:, None], axis=1)
    blk_e = jnp.minimum(blk_e, N_EXPERTS - 1).astype(I32)
    n_used = bends[-1:].astype(I32)
    slot_tok = jnp.zeros((n_rows,), I32).at[pos2.reshape(-1)].set(
        jnp.arange(2 * t, dtype=I32) // 2, unique_indices=True)

    yb = _experts(u2, slot_tok, blk_e, n_used, wg16, wu16, wd16)
    return pos2, yb, wts, h1, gate2


def kernel(x, c, w_ada, b_ada, norm_mix, w_in, w_gk2, b_gk2, gla_norm, w_a, w_pool, pool_scale, w_b, w_out, norm_ffn, w_rg, b_rg, w_re, b_re, w_gate, w_up, w_down, norm_final):
    bsz, seq, _ = x.shape
    depth = w_ada.shape[0]
    assert depth == 1
    h2d = x.reshape(bsz * seq, D)
    pos2, yb, wts, h1, gate2 = _layer(
        h2d, c, bsz, seq, w_ada[0], b_ada[0], norm_mix[0], w_in[0], w_gk2[0], b_gk2[0], gla_norm[0], w_a[0],
        w_pool[0], pool_scale[0], w_b[0], w_out[0], norm_ffn[0], w_rg[0], b_rg[0], w_re[0], b_re[0],
        w_gate[0], w_up[0], w_down[0])
    out = _combine(pos2, yb, wts, h1, gate2, norm_final.reshape(1, D), seq)
    return out.reshape(bsz, seq, D)
```

```python
import functools

import jax
import jax.numpy as jnp
from jax import lax
from jax.experimental import pallas as pl
from jax.experimental.pallas import tpu as pltpu

F32 = jnp.float32
BF16 = jnp.bfloat16
I32 = jnp.int32

D = 2048
HEADS = 4
DK = 256
DV = 512
KEY_DIM = HEADS * DK
VAL_DIM = HEADS * DV
GATE_RANK = 16
GATE_NORMALIZER = 16.0
CHUNK = 64
POOL_WINDOWS = (2, 4, 8, 16)
POOL_DIM = 1024
POOL_GROUP_DIM = 256
POOL_HALO = 16
N_GROUPS = 8
EXPERTS_PER_GROUP = 8
N_EXPERTS = 64
EXPERT_HIDDEN = 1024
EPS = 1e-6
LANES = 128
ROW_SUB = D // LANES

Z_COLS = 2 * KEY_DIM + 2 * VAL_DIM + 2 * D + POOL_DIM

VMEM_LIMIT = 56 * 1024 * 1024
VMEM_LIMIT_INPROJ = 60 * 1024 * 1024
CAST_STEPS = 8

TM_IN = 1024
TN_IN = 1024
TC_GLA = 512
TM_POOL = 256
TM_MIX = 256
TM_ROUTE = 512
TM_ROWS = 256
ROW_BLOCK = 256


def _sigmoid(x):
    return 1.0 / (1.0 + jnp.exp(-x))


def _params(sem, vmem=VMEM_LIMIT):
    return pltpu.CompilerParams(dimension_semantics=sem, vmem_limit_bytes=vmem)


def _ada_body(c_ref, w_ref, b_ref, o_ref):
    c = c_ref[...]
    s = (c * _sigmoid(c)).astype(BF16)
    o_ref[...] = jnp.dot(s, w_ref[...].astype(BF16), preferred_element_type=F32) + b_ref[...]


def _ada(c, w, b):
    bsz = c.shape[0]
    cp = jnp.zeros((8, D), F32).at[:bsz].set(c)
    n = w.shape[1]
    tn = 1024
    out = pl.pallas_call(
        _ada_body,
        grid=(n // tn,),
        in_specs=[pl.BlockSpec((8, D), lambda j: (0, 0)),
                  pl.BlockSpec((D, tn), lambda j: (0, j)),
                  pl.BlockSpec((1, tn), lambda j: (0, j))],
        out_specs=pl.BlockSpec((8, tn), lambda j: (0, j)),
        out_shape=jax.ShapeDtypeStruct((8, n), F32),
        compiler_params=_params(("arbitrary",)),
        name="ada",
    )(cp, w, b.reshape(1, n))
    return out[:bsz]


def _inproj_body(x_ref, g_ref, sc_ref, sh_ref, w_ref, wgk_ref, eg_ref, eu_ref, ed_ref,
                 z_ref, gk_ref, og_ref, ou_ref, od_ref, u_scr):
    @pl.when(pl.program_id(1) < CAST_STEPS)
    def _():
        og_ref[...] = eg_ref[...].astype(BF16)
        ou_ref[...] = eu_ref[...].astype(BF16)
        od_ref[...] = ed_ref[...].astype(BF16)

    @pl.when(pl.program_id(1) == 0)
    def _():
        x = x_ref[...]
        ms = jnp.mean(x * x, axis=-1, keepdims=True)
        u = x * lax.rsqrt(ms + EPS) * g_ref[...]
        u = (u * (1.0 + sc_ref[0]) + sh_ref[0]).astype(BF16)
        u_scr[...] = u
        gk_ref[...] = jnp.dot(u, wgk_ref[...], preferred_element_type=F32)

    z_ref[...] = jnp.dot(u_scr[...], w_ref[...], preferred_element_type=F32).astype(BF16)


def _inproj(x2, gain, scale, shift, w_main, w_gk, w_gate, w_up, w_down, seq):
    t = x2.shape[0]
    tm = min(TM_IN, seq)
    per_seq = seq // tm
    n_i = t // tm
    n_j = Z_COLS // TN_IN
    assert CAST_STEPS <= n_j
    n_slabs = n_i * CAST_STEPS
    eg = w_gate.reshape(N_EXPERTS * D, EXPERT_HIDDEN)
    eu = w_up.reshape(N_EXPERTS * D, EXPERT_HIDDEN)
    ed = w_down.reshape(N_EXPERTS * EXPERT_HIDDEN, D)
    rg, rd = eg.shape[0] // n_slabs, ed.shape[0] // n_slabs
    assert rg * n_slabs == eg.shape[0] and rd * n_slabs == ed.shape[0] and rd % 16 == 0
    slab = lambda i, j: (i * CAST_STEPS + jnp.minimum(j, CAST_STEPS - 1), 0)
    z, gk, og, ou, od = pl.pallas_call(
        _inproj_body,
        grid=(n_i, n_j),
        in_specs=[pl.BlockSpec((tm, D), lambda i, j: (i, 0)),
                  pl.BlockSpec((1, D), lambda i, j: (0, 0)),
                  pl.BlockSpec((1, 1, D), lambda i, j: (i // per_seq, 0, 0)),
                  pl.BlockSpec((1, 1, D), lambda i, j: (i // per_seq, 0, 0)),
                  pl.BlockSpec((D, TN_IN), lambda i, j: (0, j)),
                  pl.BlockSpec((D, LANES), lambda i, j: (0, 0)),
                  pl.BlockSpec((rg, EXPERT_HIDDEN), slab),
                  pl.BlockSpec((rg, EXPERT_HIDDEN), slab),
                  pl.BlockSpec((rd, D), slab)],
        out_specs=[pl.BlockSpec((tm, TN_IN), lambda i, j: (i, j)),
                   pl.BlockSpec((tm, LANES), lambda i, j: (i, 0)),
                   pl.BlockSpec((rg, EXPERT_HIDDEN), slab),
                   pl.BlockSpec((rg, EXPERT_HIDDEN), slab),
                   pl.BlockSpec((rd, D), slab)],
        out_shape=[jax.ShapeDtypeStruct((t, Z_COLS), BF16),
                   jax.ShapeDtypeStruct((t, LANES), F32),
                   jax.ShapeDtypeStruct(eg.shape, BF16),
                   jax.ShapeDtypeStruct(eu.shape, BF16),
                   jax.ShapeDtypeStruct(ed.shape, BF16)],
        scratch_shapes=[pltpu.VMEM((tm, D), BF16)],
        compiler_params=_params(("arbitrary", "arbitrary"), vmem=VMEM_LIMIT_INPROJ),
        name="inproj",
    )(x2, gain, scale, shift, w_main, w_gk, eg, eu, ed)
    return (z, gk, og.reshape(w_gate.shape), ou.reshape(w_up.shape), od.reshape(w_down.shape))


def _gla_body(q_ref, k_ref, v_ref, g_ref, gk_ref, wgk_ref, bgk_ref, gn_ref, ll_ref, o_ref, st_ref, *, n_chunks):
    @pl.when(pl.program_id(1) == 0)
    def _():
        st_ref[...] = jnp.zeros_like(st_ref)

    row = lax.broadcasted_iota(I32, (CHUNK, CHUNK), 0)
    col = lax.broadcasted_iota(I32, (CHUNK, CHUNK), 1)
    causal = row >= col
    nt = (((1,), (1,)), ((), ()))
    tn = (((0,), (0,)), ((), ()))

    def chunk(ci, carry):
        r0 = pl.multiple_of(ci * CHUNK, CHUNK)
        rows = pl.ds(r0, CHUNK)
        gk = gk_ref[rows, :].astype(BF16)
        xg = jnp.dot(gk, wgk_ref[...], preferred_element_type=F32) + bgk_ref[...]
        la = (jnp.minimum(xg, 0.0) - jnp.log(1.0 + jnp.exp(-jnp.abs(xg)))) * (1.0 / GATE_NORMALIZER)
        la_hi = la.astype(BF16)
        la_lo = (la - la_hi.astype(F32)).astype(BF16)
        ll = ll_ref[...]
        bb = (jnp.dot(ll, la_hi, preferred_element_type=F32)
              + jnp.dot(ll, la_lo, preferred_element_type=F32))
        b = bb[:CHUNK]
        bl = bb[CHUNK:]
        q = q_ref[rows, :].astype(F32)
        k = k_ref[rows, :].astype(F32)
        qe = (q * (DK ** -0.5) * jnp.exp(b)).astype(BF16)
        ke = (k * jnp.exp(-b)).astype(BF16)
        kd = (k * jnp.exp(bl - b)).astype(BF16)
        dec = jnp.exp(bl[0:1, :])
        for h in range(HEADS):
            sk = slice(h * DK, (h + 1) * DK)
            sv = slice(h * DV, (h + 1) * DV)
            att = lax.dot_general(qe[:, sk], ke[:, sk], nt, preferred_element_type=F32)
            att = jnp.where(causal, att, 0.0).astype(BF16)
            vc = v_ref[rows, sv]
            st = st_ref[h]
            o = (jnp.dot(att, vc, preferred_element_type=F32)
                 + lax.dot_general(qe[:, sk], st.astype(BF16), nt, preferred_element_type=F32))
            st_ref[h] = st * dec[:, sk] + lax.dot_general(vc, kd[:, sk], tn, preferred_element_type=F32)
            ms = jnp.mean(o * o, axis=-1, keepdims=True)
            on = o * lax.rsqrt(ms + EPS) * gn_ref[...]
            gg = g_ref[rows, sv].astype(F32)
            o_ref[rows, sv] = (on * (gg * _sigmoid(gg))).astype(BF16)
        return carry

    lax.fori_loop(0, n_chunks, chunk, 0, unroll=2)


def _gla(z, gk, w_gk2, b_gk2, gla_norm, bsz, seq):
    t = z.shape[0]
    tc = min(TC_GLA, seq)
    per_seq = seq // tc
    wgk = jnp.zeros((LANES, KEY_DIM), F32).at[:GATE_RANK].set(w_gk2).astype(BF16)
    r = jnp.arange(2 * CHUNK)[:, None]
    c = jnp.arange(CHUNK)[None, :]
    ll = jnp.where((r >= CHUNK) | (r >= c), 1.0, 0.0).astype(BF16)
    kb = KEY_DIM // KEY_DIM
    return pl.pallas_call(
        functools.partial(_gla_body, n_chunks=tc // CHUNK),
        grid=(bsz, per_seq),
        in_specs=[pl.BlockSpec((tc, KEY_DIM), lambda b, s: (b * per_seq + s, 0)),
                  pl.BlockSpec((tc, KEY_DIM), lambda b, s: (b * per_seq + s, kb)),
                  pl.BlockSpec((tc, VAL_DIM), lambda b, s: (b * per_seq + s, 1)),
                  pl.BlockSpec((tc, VAL_DIM), lambda b, s: (b * per_seq + s, 2)),
                  pl.BlockSpec((tc, LANES), lambda b, s: (b * per_seq + s, 0)),
                  pl.BlockSpec((LANES, KEY_DIM), lambda b, s: (0, 0)),
                  pl.BlockSpec((1, KEY_DIM), lambda b, s: (0, 0)),
                  pl.BlockSpec((1, DV), lambda b, s: (0, 0)),
                  pl.BlockSpec((2 * CHUNK, CHUNK), lambda b, s: (0, 0))],
        out_specs=pl.BlockSpec((tc, VAL_DIM), lambda b, s: (b * per_seq + s, 0)),
        out_shape=jax.ShapeDtypeStruct((t, VAL_DIM), BF16),
        scratch_shapes=[pltpu.VMEM((HEADS, DV, DK), F32)],
        compiler_params=_params(("arbitrary", "arbitrary")),
        name="gla",
    )(z, z, z, z, gk, wgk, b_gk2.reshape(1, KEY_DIM), gla_norm.reshape(1, DV), ll)


def _pool_body(p_ref, halo_ref, gb_ref, wp_ref, ps_ref, wb_ref, o_ref, *, tm, seq):
    base = (pl.program_id(0) * tm) % seq
    pf = p_ref[...].astype(F32)
    hal = halo_ref[...].astype(F32)
    hal = jnp.where(base == 0, 0.0, hal)
    ext = jnp.concatenate([hal, pf], axis=0)
    pos = base + lax.broadcasted_iota(I32, (tm, 1), 0)
    ys = []
    for gi, w in enumerate(POOL_WINDOWS):
        cols = slice(gi * POOL_GROUP_DIM, (gi + 1) * POOL_GROUP_DIM)
        s = ext[:, cols]
        sh = 1
        while sh < w:
            s = s + pltpu.roll(s, sh, 0)
            sh *= 2
        cnt = jnp.minimum(pos + 1, w).astype(F32)
        m = s[POOL_HALO:] / cnt - pf[:, cols]
        ys.append(jnp.dot(m.astype(BF16), wp_ref[gi], preferred_element_type=F32))
    y = jnp.concatenate(ys, axis=1) * ps_ref[...]
    yb = jnp.dot(y.astype(BF16), wb_ref[...], preferred_element_type=F32)
    o_ref[...] = (yb * _sigmoid(gb_ref[...].astype(F32))).astype(BF16)


def _pool(z, w_pool, pool_scale, w_b, seq):
    t = z.shape[0]
    tm = min(TM_POOL, seq)
    p_blk = (Z_COLS - POOL_DIM) // POOL_DIM
    gb_blk = (2 * KEY_DIM + 2 * VAL_DIM + D) // D
    hb = tm // POOL_HALO
    return pl.pallas_call(
        functools.partial(_pool_body, tm=tm, seq=seq),
        grid=(t // tm,),
        in_specs=[pl.BlockSpec((tm, POOL_DIM), lambda i: (i, p_blk)),
                  pl.BlockSpec((POOL_HALO, POOL_DIM), lambda i: (jnp.maximum(i * hb - 1, 0), p_blk)),
                  pl.BlockSpec((tm, D), lambda i: (i, gb_blk)),
                  pl.BlockSpec((len(POOL_WINDOWS), POOL_GROUP_DIM, POOL_GROUP_DIM), lambda i: (0, 0, 0)),
                  pl.BlockSpec((1, POOL_DIM), lambda i: (0, 0)),
                  pl.BlockSpec((POOL_DIM, D), lambda i: (0, 0))],
        out_specs=pl.BlockSpec((tm, D), lambda i: (i, 0)),
        out_shape=jax.ShapeDtypeStruct((t, D), BF16),
        compiler_params=_params(("arbitrary",)),
        name="pool",
    )(z, z, z, w_pool.astype(BF16), pool_scale.reshape(1, POOL_DIM), w_b.astype(BF16))


def _mix_body(o_ref, ga_ref, cb_ref, x_ref, g1_ref, nf_ref, sc_ref, sh_ref, wa_ref, wo_ref, wr_ref,
              h_ref, u_ref, lg_ref):
    ya = jnp.dot(o_ref[...], wa_ref[...], preferred_element_type=F32)
    merged = ya * _sigmoid(ga_ref[...].astype(F32)) + cb_ref[...].astype(F32)
    mo = jnp.dot(merged.astype(BF16), wo_ref[...], preferred_element_type=F32)
    h = x_ref[...] + g1_ref[0] * mo
    h_ref[...] = h
    ms = jnp.mean(h * h, axis=-1, keepdims=True)
    u = h * lax.rsqrt(ms + EPS) * nf_ref[...]
    u = u * (1.0 + sc_ref[0]) + sh_ref[0]
    ub = u.astype(BF16)
    u_ref[...] = ub.reshape(u_ref.shape)
    lg_ref[...] = jnp.dot(ub, wr_ref[...], preferred_element_type=F32)


def _mix(o_gla, z, cb, x2, gate1, norm_ffn, scale2, shift2, w_a, w_out, w_router, seq):
    t = x2.shape[0]
    tm = min(TM_MIX, seq)
    per_seq = seq // tm
    ga_blk = (2 * KEY_DIM + 2 * VAL_DIM) // D
    tok = lambda i: (i, 0)
    const = lambda i: (0, 0)
    per_b = lambda i: (i // per_seq, 0, 0)
    return pl.pallas_call(
        _mix_body,
        grid=(t // tm,),
        in_specs=[pl.BlockSpec((tm, VAL_DIM), tok),
                  pl.BlockSpec((tm, D), lambda i: (i, ga_blk)),
                  pl.BlockSpec((tm, D), tok),
                  pl.BlockSpec((tm, D), tok),
                  pl.BlockSpec((1, 1, D), per_b),
                  pl.BlockSpec((1, D), const),
                  pl.BlockSpec((1, 1, D), per_b),
                  pl.BlockSpec((1, 1, D), per_b),
                  pl.BlockSpec((VAL_DIM, D), const, pipeline_mode=pl.Buffered(1)),
                  pl.BlockSpec((D, D), const, pipeline_mode=pl.Buffered(1)),
                  pl.BlockSpec((D, LANES), const)],
        out_specs=[pl.BlockSpec((tm, D), tok),
                   pl.BlockSpec((tm, ROW_SUB, LANES), lambda i: (i, 0, 0)),
                   pl.BlockSpec((tm, LANES), tok)],
        out_shape=[jax.ShapeDtypeStruct((t, D), F32),
                   jax.ShapeDtypeStruct((t, ROW_SUB, LANES), BF16),
                   jax.ShapeDtypeStruct((t, LANES), F32)],
        compiler_params=_params(("arbitrary",)),
        name="mix",
    )(o_gla, z, cb, x2, gate1, norm_ffn, scale2, shift2, w_a, w_out, w_router)


def _route_topk(gl, tm):
    lane = lax.broadcasted_iota(I32, (tm, LANES), 1)
    lanef = lane.astype(F32)
    neg = float("-inf")
    big = float(LANES)
    gmask = lane < N_GROUPS
    gmax = jnp.max(jnp.where(gmask, gl, neg), axis=-1, keepdims=True)
    gsel = jnp.min(jnp.where(gmask & (gl == gmax), lanef, big), axis=-1, keepdims=True)
    gsum = jnp.sum(jnp.where(gmask, jnp.exp(gl - gmax), 0.0), axis=-1, keepdims=True)
    pg = 1.0 / gsum
    lo = N_GROUPS + gsel * EXPERTS_PER_GROUP
    emask = (lanef >= lo) & (lanef < lo + EXPERTS_PER_GROUP)
    emax = jnp.max(jnp.where(emask, gl, neg), axis=-1, keepdims=True)
    ee = jnp.where(emask, jnp.exp(gl - emax), 0.0)
    eprob = ee / jnp.sum(ee, axis=-1, keepdims=True)
    v1 = jnp.max(jnp.where(emask, eprob, -1.0), axis=-1, keepdims=True)
    i1 = jnp.min(jnp.where(emask & (eprob == v1), lanef, big), axis=-1, keepdims=True)
    m2 = emask & (lanef != i1)
    v2 = jnp.max(jnp.where(m2, eprob, -1.0), axis=-1, keepdims=True)
    i2 = jnp.min(jnp.where(m2 & (eprob == v2), lanef, big), axis=-1, keepdims=True)
    tot = v1 + v2
    w1 = pg * (v1 / tot)
    w2 = pg * (v2 / tot)
    hit1 = lanef == i1 - N_GROUPS
    hit2 = lanef == i2 - N_GROUPS
    return hit1, hit2, w1, w2


def _count_body(lg_ref, bias_ref, cnt_ref, *, tm):
    @pl.when(pl.program_id(0) == 0)
    def _():
        cnt_ref[...] = jnp.zeros_like(cnt_ref)

    hit1, hit2, _, _ = _route_topk(lg_ref[...] + bias_ref[...], tm)
    oh = jnp.where(hit1 | hit2, 1.0, 0.0)
    cnt_ref[...] += jnp.sum(oh, axis=0, keepdims=True)


def _assign_body(lg_ref, bias_ref, cnt_ref, ls_ref, ut_ref, pos_ref, wt_ref, carry_ref, pst_ref, *, tm):
    lane = lax.broadcasted_iota(I32, (tm, LANES), 1)

    @pl.when(pl.program_id(0) == 0)
    def _():
        cnt = cnt_ref[0:1, :]
        nblk = jnp.floor((cnt + (ROW_BLOCK - 1)) * (1.0 / ROW_BLOCK))
        hi = jnp.floor(nblk * (1.0 / 256.0))
        lo = nblk - 256.0 * hi
        ut = ut_ref[...]
        hi8 = jnp.broadcast_to(hi, (8, LANES)).astype(BF16)
        lo8 = jnp.broadcast_to(lo, (8, LANES)).astype(BF16)
        pre = (256.0 * jnp.dot(hi8, ut, preferred_element_type=F32)
               + jnp.dot(lo8, ut, preferred_element_type=F32))
        pst_ref[...] = pre[0:1] * float(ROW_BLOCK)
        carry_ref[...] = jnp.zeros_like(carry_ref)

    hit1, hit2, w1, w2 = _route_topk(lg_ref[...] + bias_ref[...], tm)
    oh = jnp.where(hit1 | hit2, 1.0, 0.0)
    cum = jnp.dot(ls_ref[...], oh.astype(BF16), preferred_element_type=F32)
    slot = cum + carry_ref[...] + pst_ref[...]
    carry_ref[...] += jnp.sum(oh, axis=0, keepdims=True)
    p1 = jnp.sum(jnp.where(hit1, slot, 0.0), axis=-1, keepdims=True)
    p2 = jnp.sum(jnp.where(hit2, slot, 0.0), axis=-1, keepdims=True)
    pos_ref[...] = jnp.where(lane == 0, p1, jnp.where(lane == 1, p2, 0.0)).astype(I32)
    wt_ref[...] = jnp.where(lane == 0, w1, jnp.where(lane == 1, w2, 0.0))


def _route(logits, bias):
    t = logits.shape[0]
    tm = min(TM_ROUTE, t)
    r = jnp.arange(tm)
    ls = jnp.where(r[:, None] > r[None, :], 1.0, 0.0).astype(BF16)
    e = jnp.arange(LANES)
    ut = jnp.where(e[:, None] < e[None, :], 1.0, 0.0).astype(BF16)
    tok = lambda i: (i, 0)
    const = lambda i: (0, 0)
    cnt = pl.pallas_call(
        functools.partial(_count_body, tm=tm),
        grid=(t // tm,),
        in_specs=[pl.BlockSpec((tm, LANES), tok), pl.BlockSpec((1, LANES), const)],
        out_specs=pl.BlockSpec((8, LANES), const),
        out_shape=jax.ShapeDtypeStruct((8, LANES), F32),
        compiler_params=_params(("arbitrary",)),
        name="route_count",
    )(logits, bias)
    pos, wts = pl.pallas_call(
        functools.partial(_assign_body, tm=tm),
        grid=(t // tm,),
        in_specs=[pl.BlockSpec((tm, LANES), tok),
                  pl.BlockSpec((1, LANES), const),
                  pl.BlockSpec((8, LANES), const),
                  pl.BlockSpec((tm, tm), const),
                  pl.BlockSpec((LANES, LANES), const)],
        out_specs=[pl.BlockSpec((tm, LANES), tok), pl.BlockSpec((tm, LANES), tok)],
        out_shape=[jax.ShapeDtypeStruct((t, LANES), I32), jax.ShapeDtypeStruct((t, LANES), F32)],
        scratch_shapes=[pltpu.VMEM((1, LANES), F32), pltpu.VMEM((1, LANES), F32)],
        compiler_params=_params(("arbitrary",)),
        name="route_assign",
    )(logits, bias, cnt, ls, ut)
    return pos, wts, cnt


def _row_copy(src, s, dst, d, sem):
    return pltpu.make_async_copy(src.at[s], dst.at[d], sem)


def _list_copy(src, b, dst, s, sem):
    return pltpu.make_async_copy(src.at[b], dst.at[s], sem.at[s])


def _expert_body(be_ref, nu_ref, tok_ref, u_ref, wg_ref, wu_ref, wd_ref, y_ref, xbuf, tokbuf, sem, tsem):
    i = pl.program_id(0)
    nu = nu_ref[0]
    slot = i % 2
    used = i < nu

    def gather(s):
        def issue(p, carry):
            for k in range(2):
                r = 2 * p + k
                _row_copy(u_ref, tokbuf[s, 0, r], xbuf.at[s], r, sem.at[s]).start(priority=k)
            return carry
        lax.fori_loop(0, ROW_BLOCK // 2, issue, 0, unroll=4)

    @pl.when((i == 0) & used)
    def _():
        _list_copy(tok_ref, 0, tokbuf, 0, tsem).start()
        _list_copy(tok_ref, 0, tokbuf, 0, tsem).wait()
        gather(0)

        @pl.when(1 < nu)
        def _():
            _list_copy(tok_ref, 1, tokbuf, 1, tsem).start()

    @pl.when(used)
    def _():
        @pl.when(i + 2 < nu)
        def _():
            _list_copy(tok_ref, i + 2, tokbuf, slot, tsem).start()

        @pl.when(i + 1 < nu)
        def _():
            _list_copy(tok_ref, i + 1, tokbuf, 1 - slot, tsem).wait()
            gather(1 - slot)

        def drain(r, carry):
            _row_copy(u_ref, 0, xbuf.at[slot], 0, sem.at[slot]).wait()
            return carry
        lax.fori_loop(0, ROW_BLOCK, drain, 0, unroll=8)

        x = xbuf[slot].reshape(ROW_BLOCK, D)
        a = jnp.dot(x, wg_ref[0], preferred_element_type=F32)
        b = jnp.dot(x, wu_ref[0], preferred_element_type=F32)
        hdn = (a * _sigmoid(a) * b).astype(BF16)
        y = jnp.dot(hdn, wd_ref[0], preferred_element_type=F32)
        y_ref[...] = y.astype(BF16).reshape(y_ref.shape)

    @pl.when(jnp.logical_not(used))
    def _():
        y_ref[...] = jnp.zeros_like(y_ref)


def _experts(u3, slot_tok, blk_e, n_used, w_gate, w_up, w_down):
    n_rows = slot_tok.shape[0]
    nb = n_rows // ROW_BLOCK
    rows = lambda i, be, nu: (i, 0, 0)
    wsel = lambda i, be, nu: (be[i], 0, 0)
    grid_spec = pltpu.PrefetchScalarGridSpec(
        num_scalar_prefetch=2,
        grid=(nb,),
        in_specs=[pl.BlockSpec(memory_space=pl.ANY),
                  pl.BlockSpec(memory_space=pl.ANY),
                  pl.BlockSpec((1, D, EXPERT_HIDDEN), wsel),
                  pl.BlockSpec((1, D, EXPERT_HIDDEN), wsel),
                  pl.BlockSpec((1, EXPERT_HIDDEN, D), wsel)],
        out_specs=pl.BlockSpec((ROW_BLOCK, ROW_SUB, LANES), rows),
        scratch_shapes=[pltpu.VMEM((2, ROW_BLOCK, ROW_SUB, LANES), BF16),
                        pltpu.SMEM((2, 1, ROW_BLOCK), I32),
                        pltpu.SemaphoreType.DMA((2,)),
                        pltpu.SemaphoreType.DMA((2,))],
    )
    return pl.pallas_call(
        _expert_body,
        grid_spec=grid_spec,
        out_shape=jax.ShapeDtypeStruct((n_rows, ROW_SUB, LANES), BF16),
        compiler_params=_params(("arbitrary",)),
        name="experts",
    )(blk_e, n_used, slot_tok.reshape(nb, 1, ROW_BLOCK), u3, w_gate, w_up, w_down)


def _combine_body(pos_ref, yb_ref, wt_ref, h_ref, g2_ref, nf_ref, o_ref, gbuf, posbuf, sem, psem, *, tm, n_steps):
    i = pl.program_id(0)
    slot = i % 2

    def gather(s):
        def issue(r, carry):
            for k in range(2):
                _row_copy(yb_ref, posbuf[s, 0, 2 * r + k], gbuf.at[s, k], r, sem.at[s]).start(priority=k)
            return carry
        lax.fori_loop(0, tm, issue, 0, unroll=4)

    @pl.when(i == 0)
    def _():
        _list_copy(pos_ref, 0, posbuf, 0, psem).start()
        _list_copy(pos_ref, 0, posbuf, 0, psem).wait()
        gather(0)
        if n_steps > 1:
            _list_copy(pos_ref, 1, posbuf, 1, psem).start()

    @pl.when(i + 2 < n_steps)
    def _():
        _list_copy(pos_ref, i + 2, posbuf, slot, psem).start()

    @pl.when(i + 1 < n_steps)
    def _():
        _list_copy(pos_ref, i + 1, posbuf, 1 - slot, psem).wait()
        gather(1 - slot)

    def drain(r, carry):
        for k in range(2):
            _row_copy(yb_ref, 0, gbuf.at[slot, k], 0, sem.at[slot]).wait()
        return carry
    lax.fori_loop(0, tm, drain, 0, unroll=4)

    wt = wt_ref[...]
    ya = gbuf[slot, 0].reshape(tm, D).astype(F32)
    yb = gbuf[slot, 1].reshape(tm, D).astype(F32)
    y = wt[:, 0:1] * ya + wt[:, 1:2] * yb
    h = h_ref[...] + g2_ref[0] * y
    ms = jnp.mean(h * h, axis=-1, keepdims=True)
    o_ref[...] = h * lax.rsqrt(ms + EPS) * nf_ref[...]


def _combine(pos2, yb, wts, h1, gate2, norm_final, seq):
    t = h1.shape[0]
    tm = min(TM_ROWS, seq)
    per_seq = seq // tm
    tok = lambda i: (i, 0)
    n_steps = t // tm
    return pl.pallas_call(
        functools.partial(_combine_body, tm=tm, n_steps=n_steps),
        grid=(n_steps,),
        in_specs=[pl.BlockSpec(memory_space=pl.ANY),
                  pl.BlockSpec(memory_space=pl.ANY),
                  pl.BlockSpec((tm, LANES), tok),
                  pl.BlockSpec((tm, D), tok),
                  pl.BlockSpec((1, 1, D), lambda i: (i // per_seq, 0, 0)),
                  pl.BlockSpec((1, D), lambda i: (0, 0))],
        out_specs=pl.BlockSpec((tm, D), tok),
        out_shape=jax.ShapeDtypeStruct((t, D), F32),
        scratch_shapes=[pltpu.VMEM((2, 2, tm, ROW_SUB, LANES), BF16),
                        pltpu.SMEM((2, 1, 2 * tm), I32),
                        pltpu.SemaphoreType.DMA((2,)),
                        pltpu.SemaphoreType.DMA((2,))],
        compiler_params=_params(("arbitrary",)),
        name="combine",
    )(pos2.reshape(n_steps, 1, 2 * tm), yb, wts, h1, gate2, norm_final)


def _layer(h2d, c, bsz, seq, w_ada, b_ada, norm_mix, w_in, w_gk2, b_gk2, gla_norm, w_a, w_pool, pool_scale, w_b,
           w_out, norm_ffn, w_rg, b_rg, w_re, b_re, w_gate, w_up, w_down):
    t = bsz * seq
    mod = _ada(c, w_ada, b_ada)
    shift1, scale1, gate1, shift2, scale2, gate2 = [m.reshape(bsz, 1, D) for m in jnp.split(mod, 6, axis=-1)]

    c_gk = 2 * KEY_DIM + 2 * VAL_DIM
    c_p = c_gk + GATE_RANK
    c_ga = c_p + POOL_DIM
    w_main = jnp.concatenate([w_in[:, :c_gk], w_in[:, c_ga:], w_in[:, c_p:c_ga]], axis=1).astype(BF16)
    w_gk = jnp.zeros((D, LANES), F32).at[:, :GATE_RANK].set(w_in[:, c_gk:c_p]).astype(BF16)

    z, gk, wg16, wu16, wd16 = _inproj(h2d, norm_mix.reshape(1, D), scale1, shift1, w_main, w_gk,
                                      w_gate, w_up, w_down, seq)
    o_gla = _gla(z, gk, w_gk2, b_gk2, gla_norm, bsz, seq)
    cb = _pool(z, w_pool, pool_scale, w_b, seq)

    w_router = jnp.zeros((D, LANES), F32).at[:, :N_GROUPS].set(w_rg)
    w_router = w_router.at[:, N_GROUPS:N_GROUPS + N_EXPERTS].set(w_re).astype(BF16)
    h1, u2, logits = _mix(o_gla, z, cb, h2d, gate1, norm_ffn.reshape(1, D), scale2, shift2,
                          w_a.astype(BF16), w_out.astype(BF16), w_router, seq)

    bias = jnp.zeros((1, LANES), F32).at[0, :N_GROUPS].set(b_rg).at[0, N_GROUPS:N_GROUPS + N_EXPERTS].set(b_re)
    pos, wts, cnt = _route(logits, bias)
    pos2 = pos[:, :2]

    counts = cnt[0, :N_EXPERTS].astype(I32)
    nblk = (counts + ROW_BLOCK - 1) // ROW_BLOCK
    bends = jnp.cumsum(nblk)
    n_rows = 2 * t + N_EXPERTS * ROW_BLOCK
    n_blocks = n_rows // ROW_BLOCK
    blk_e = jnp.sum(bends[None, :] <= jnp.arange(n_blocks, dtype=I32)[:, None], axis=1)
    blk_e = jnp.minimum(blk_e, N_EXPERTS - 1).astype(I32)
    n_used = bends[-1:].astype(I32)
    slot_tok = jnp.zeros((n_rows,), I32).at[pos2.reshape(-1)].set(
        jnp.arange(2 * t, dtype=I32) // 2, unique_indices=True)

    yb = _experts(u2, slot_tok, blk_e, n_used, wg16, wu16, wd16)
    return pos2, yb, wts, h1, gate2


def kernel(x, c, w_ada, b_ada, norm_mix, w_in, w_gk2, b_gk2, gla_norm, w_a, w_pool, pool_scale, w_b, w_out, norm_ffn, w_rg, b_rg, w_re, b_re, w_gate, w_up, w_down, norm_final):
    bsz, seq, _ = x.shape
    depth = w_ada.shape[0]
    assert depth == 1
    h2d = x.reshape(bsz * seq, D)
    pos2, yb, wts, h1, gate2 = _layer(
        h2d, c, bsz, seq, w_ada[0], b_ada[0], norm_mix[0], w_in[0], w_gk2[0], b_gk2[0], gla_norm[0], w_a[0],
        w_pool[0], pool_scale[0], w_b[0], w_out[0], norm_ffn[0], w_rg[0], b_rg[0], w_re[0], b_re[0],
        w_gate[0], w_up[0], w_down[0])
    out = _combine(pos2, yb, wts, h1, gate2, norm_final.reshape(1, D), seq)
    return out.reshape(bsz, seq, D)
```

```python
import functools

import jax
import jax.numpy as jnp
from jax import lax
from jax.experimental import pallas as pl
from jax.experimental.pallas import tpu as pltpu

F32 = jnp.float32
BF16 = jnp.bfloat16
I32 = jnp.int32

D = 2048
HEADS = 4
DK = 256
DV = 512
KEY_DIM = HEADS * DK
VAL_DIM = HEADS * DV
GATE_RANK = 16
GATE_NORMALIZER = 16.0
CHUNK = 64
POOL_WINDOWS = (2, 4, 8, 16)
POOL_DIM = 1024
POOL_GROUP_DIM = 256
POOL_HALO = 16
N_GROUPS = 8
EXPERTS_PER_GROUP = 8
N_EXPERTS = 64
EXPERT_HIDDEN = 1024
EPS = 1e-6
LANES = 128
ROW_SUB = D // LANES

Z_COLS = 2 * KEY_DIM + 2 * VAL_DIM + 2 * D + POOL_DIM

VMEM_LIMIT = 56 * 1024 * 1024
VMEM_LIMIT_INPROJ = 60 * 1024 * 1024
CAST_STEPS = 8

TM_IN = 1024
TN_IN = 1024
TC_GLA = 512
TM_POOL = 256
TM_MIX = 256
TM_ROUTE = 512
TM_ROWS = 256
ROW_BLOCK = 256


def _sigmoid(x):
    return 1.0 / (1.0 + jnp.exp(-x))


def _params(sem, vmem=VMEM_LIMIT):
    return pltpu.CompilerParams(dimension_semantics=sem, vmem_limit_bytes=vmem)


def _ada_body(c_ref, w_ref, b_ref, o_ref):
    c = c_ref[...]
    s = (c * _sigmoid(c)).astype(BF16)
    o_ref[...] = jnp.dot(s, w_ref[...].astype(BF16), preferred_element_type=F32) + b_ref[...]


def _ada(c, w, b):
    bsz = c.shape[0]
    cp = jnp.zeros((8, D), F32).at[:bsz].set(c)
    n = w.shape[1]
    tn = 1024
    out = pl.pallas_call(
        _ada_body,
        grid=(n // tn,),
        in_specs=[pl.BlockSpec((8, D), lambda j: (0, 0)),
                  pl.BlockSpec((D, tn), lambda j: (0, j)),
                  pl.BlockSpec((1, tn), lambda j: (0, j))],
        out_specs=pl.BlockSpec((8, tn), lambda j: (0, j)),
        out_shape=jax.ShapeDtypeStruct((8, n), F32),
        compiler_params=_params(("arbitrary",)),
        name="ada",
    )(cp, w, b.reshape(1, n))
    return out[:bsz]


def _inproj_body(x_ref, g_ref, sc_ref, sh_ref, w_ref, wgk_ref, eg_ref, eu_ref, ed_ref,
                 z_ref, gk_ref, og_ref, ou_ref, od_ref, u_scr):
    @pl.when(pl.program_id(1) < CAST_STEPS)
    def _():
        og_ref[...] = eg_ref[...].astype(BF16)
        ou_ref[...] = eu_ref[...].astype(BF16)
        od_ref[...] = ed_ref[...].astype(BF16)

    @pl.when(pl.program_id(1) == 0)
    def _():
        x = x_ref[...]
        ms = jnp.mean(x * x, axis=-1, keepdims=True)
        u = x * lax.rsqrt(ms + EPS) * g_ref[...]
        u = (u * (1.0 + sc_ref[0]) + sh_ref[0]).astype(BF16)
        u_scr[...] = u
        gk_ref[...] = jnp.dot(u, wgk_ref[...], preferred_element_type=F32)

    z_ref[...] = jnp.dot(u_scr[...], w_ref[...], preferred_element_type=F32).astype(BF16)


def _inproj(x2, gain, scale, shift, w_main, w_gk, w_gate, w_up, w_down, seq):
    t = x2.shape[0]
    tm = min(TM_IN, seq)
    per_seq = seq // tm
    n_i = t // tm
    n_j = Z_COLS // TN_IN
    assert CAST_STEPS <= n_j
    n_slabs = n_i * CAST_STEPS
    eg = w_gate.reshape(N_EXPERTS * D, EXPERT_HIDDEN)
    eu = w_up.reshape(N_EXPERTS * D, EXPERT_HIDDEN)
    ed = w_down.reshape(N_EXPERTS * EXPERT_HIDDEN, D)
    rg, rd = eg.shape[0] // n_slabs, ed.shape[0] // n_slabs
    assert rg * n_slabs == eg.shape[0] and rd * n_slabs == ed.shape[0] and rd % 16 == 0
    slab = lambda i, j: (i * CAST_STEPS + jnp.minimum(j, CAST_STEPS - 1), 0)
    z, gk, og, ou, od = pl.pallas_call(
        _inproj_body,
        grid=(n_i, n_j),
        in_specs=[pl.BlockSpec((tm, D), lambda i, j: (i, 0)),
                  pl.BlockSpec((1, D), lambda i, j: (0, 0)),
                  pl.BlockSpec((1, 1, D), lambda i, j: (i // per_seq, 0, 0)),
                  pl.BlockSpec((1, 1, D), lambda i, j: (i // per_seq, 0, 0)),
                  pl.BlockSpec((D, TN_IN), lambda i, j: (0, j)),
                  pl.BlockSpec((D, LANES), lambda i, j: (0, 0)),
                  pl.BlockSpec((rg, EXPERT_HIDDEN), slab),
                  pl.BlockSpec((rg, EXPERT_HIDDEN), slab),
                  pl.BlockSpec((rd, D), slab)],
        out_specs=[pl.BlockSpec((tm, TN_IN), lambda i, j: (i, j)),
                   pl.BlockSpec((tm, LANES), lambda i, j: (i, 0)),
                   pl.BlockSpec((rg, EXPERT_HIDDEN), slab),
                   pl.BlockSpec((rg, EXPERT_HIDDEN), slab),
                   pl.BlockSpec((rd, D), slab)],
        out_shape=[jax.ShapeDtypeStruct((t, Z_COLS), BF16),
                   jax.ShapeDtypeStruct((t, LANES), F32),
                   jax.ShapeDtypeStruct(eg.shape, BF16),
                   jax.ShapeDtypeStruct(eu.shape, BF16),
                   jax.ShapeDtypeStruct(ed.shape, BF16)],
        scratch_shapes=[pltpu.VMEM((tm, D), BF16)],
        compiler_params=_params(("arbitrary", "arbitrary"), vmem=VMEM_LIMIT_INPROJ),
        name="inproj",
    )(x2, gain, scale, shift, w_main, w_gk, eg, eu, ed)
    return (z, gk, og.reshape(w_gate.shape), ou.reshape(w_up.shape), od.reshape(w_down.shape))


def _gla_body(q_ref, k_ref, v_ref, g_ref, gk_ref, wgk_ref, bgk_ref, gn_ref, ll_ref, o_ref, st_ref, *, n_chunks):
    @pl.when(pl.program_id(1) == 0)
    def _():
        st_ref[...] = jnp.zeros_like(st_ref)

    row = lax.broadcasted_iota(I32, (CHUNK, CHUNK), 0)
    col = lax.broadcasted_iota(I32, (CHUNK, CHUNK), 1)
    causal = row >= col
    nt = (((1,), (1,)), ((), ()))
    tn = (((0,), (0,)), ((), ()))

    def chunk(ci, carry):
        r0 = pl.multiple_of(ci * CHUNK, CHUNK)
        rows = pl.ds(r0, CHUNK)
        gk = gk_ref[rows, :].astype(BF16)
        xg = jnp.dot(gk, wgk_ref[...], preferred_element_type=F32) + bgk_ref[...]
        la = (jnp.minimum(xg, 0.0) - jnp.log(1.0 + jnp.exp(-jnp.abs(xg)))) * (1.0 / GATE_NORMALIZER)
        la_hi = la.astype(BF16)
        la_lo = (la - la_hi.astype(F32)).astype(BF16)
        ll = ll_ref[...]
        bb = (jnp.dot(ll, la_hi, preferred_element_type=F32)
              + jnp.dot(ll, la_lo, preferred_element_type=F32))
        b = bb[:CHUNK]
        bl = bb[CHUNK:]
        q = q_ref[rows, :].astype(F32)
        k = k_ref[rows, :].astype(F32)
        qe = (q * (DK ** -0.5) * jnp.exp(b)).astype(BF16)
        ke = (k * jnp.exp(-b)).astype(BF16)
        kd = (k * jnp.exp(bl - b)).astype(BF16)
        dec = jnp.exp(bl[0:1, :])
        for h in range(HEADS):
            sk = slice(h * DK, (h + 1) * DK)
            sv = slice(h * DV, (h + 1) * DV)
            att = lax.dot_general(qe[:, sk], ke[:, sk], nt, preferred_element_type=F32)
            att = jnp.where(causal, att, 0.0).astype(BF16)
            vc = v_ref[rows, sv]
            st = st_ref[h]
            o = (jnp.dot(att, vc, preferred_element_type=F32)
                 + lax.dot_general(qe[:, sk], st.astype(BF16), nt, preferred_element_type=F32))
            st_ref[h] = st * dec[:, sk] + lax.dot_general(vc, kd[:, sk], tn, preferred_element_type=F32)
            ms = jnp.mean(o * o, axis=-1, keepdims=True)
            on = o * lax.rsqrt(ms + EPS) * gn_ref[...]
            gg = g_ref[rows, sv].astype(F32)
            o_ref[rows, sv] = (on * (gg * _sigmoid(gg))).astype(BF16)
        return carry

    lax.fori_loop(0, n_chunks, chunk, 0, unroll=4)


def _gla(z, gk, w_gk2, b_gk2, gla_norm, bsz, seq):
    t = z.shape[0]
    tc = min(TC_GLA, seq)
    per_seq = seq // tc
    wgk = jnp.zeros((LANES, KEY_DIM), F32).at[:GATE_RANK].set(w_gk2).astype(BF16)
    r = jnp.arange(2 * CHUNK)[:, None]
    c = jnp.arange(CHUNK)[None, :]
    ll = jnp.where((r >= CHUNK) | (r >= c), 1.0, 0.0).astype(BF16)
    kb = KEY_DIM // KEY_DIM
    return pl.pallas_call(
        functools.partial(_gla_body, n_chunks=tc // CHUNK),
        grid=(bsz, per_seq),
        in_specs=[pl.BlockSpec((tc, KEY_DIM), lambda b, s: (b * per_seq + s, 0)),
                  pl.BlockSpec((tc, KEY_DIM), lambda b, s: (b * per_seq + s, kb)),
                  pl.BlockSpec((tc, VAL_DIM), lambda b, s: (b * per_seq + s, 1)),
                  pl.BlockSpec((tc, VAL_DIM), lambda b, s: (b * per_seq + s, 2)),
                  pl.BlockSpec((tc, LANES), lambda b, s: (b * per_seq + s, 0)),
                  pl.BlockSpec((LANES, KEY_DIM), lambda b, s: (0, 0)),
                  pl.BlockSpec((1, KEY_DIM), lambda b, s: (0, 0)),
                  pl.BlockSpec((1, DV), lambda b, s: (0, 0)),
                  pl.BlockSpec((2 * CHUNK, CHUNK), lambda b, s: (0, 0))],
        out_specs=pl.BlockSpec((tc, VAL_DIM), lambda b, s: (b * per_seq + s, 0)),
        out_shape=jax.ShapeDtypeStruct((t, VAL_DIM), BF16),
        scratch_shapes=[pltpu.VMEM((HEADS, DV, DK), F32)],
        compiler_params=_params(("arbitrary", "arbitrary")),
        name="gla",
    )(z, z, z, z, gk, wgk, b_gk2.reshape(1, KEY_DIM), gla_norm.reshape(1, DV), ll)


def _pool_body(p_ref, halo_ref, gb_ref, wp_ref, ps_ref, wb_ref, o_ref, *, tm, seq):
    base = (pl.program_id(0) * tm) % seq
    pf = p_ref[...].astype(F32)
    hal = halo_ref[...].astype(F32)
    hal = jnp.where(base == 0, 0.0, hal)
    ext = jnp.concatenate([hal, pf], axis=0)
    pos = base + lax.broadcasted_iota(I32, (tm, 1), 0)
    ys = []
    for gi, w in enumerate(POOL_WINDOWS):
        cols = slice(gi * POOL_GROUP_DIM, (gi + 1) * POOL_GROUP_DIM)
        s = ext[:, cols]
        sh = 1
        while sh < w:
            s = s + pltpu.roll(s, sh, 0)
            sh *= 2
        cnt = jnp.minimum(pos + 1, w).astype(F32)
        m = s[POOL_HALO:] / cnt - pf[:, cols]
        ys.append(jnp.dot(m.astype(BF16), wp_ref[gi], preferred_element_type=F32))
    y = jnp.concatenate(ys, axis=1) * ps_ref[...]
    yb = jnp.dot(y.astype(BF16), wb_ref[...], preferred_element_type=F32)
    o_ref[...] = (yb * _sigmoid(gb_ref[...].astype(F32))).astype(BF16)


def _pool(z, w_pool, pool_scale, w_b, seq):
    t = z.shape[0]
    tm = min(TM_POOL, seq)
    p_blk = (Z_COLS - POOL_DIM) // POOL_DIM
    gb_blk = (2 * KEY_DIM + 2 * VAL_DIM + D) // D
    hb = tm // POOL_HALO
    return pl.pallas_call(
        functools.partial(_pool_body, tm=tm, seq=seq),
        grid=(t // tm,),
        in_specs=[pl.BlockSpec((tm, POOL_DIM), lambda i: (i, p_blk)),
                  pl.BlockSpec((POOL_HALO, POOL_DIM), lambda i: (jnp.maximum(i * hb - 1, 0), p_blk)),
                  pl.BlockSpec((tm, D), lambda i: (i, gb_blk)),
                  pl.BlockSpec((len(POOL_WINDOWS), POOL_GROUP_DIM, POOL_GROUP_DIM), lambda i: (0, 0, 0)),
                  pl.BlockSpec((1, POOL_DIM), lambda i: (0, 0)),
                  pl.BlockSpec((POOL_DIM, D), lambda i: (0, 0))],
        out_specs=pl.BlockSpec((tm, D), lambda i: (i, 0)),
        out_shape=jax.ShapeDtypeStruct((t, D), BF16),
        compiler_params=_params(("arbitrary",)),
        name="pool",
    )(z, z, z, w_pool.astype(BF16), pool_scale.reshape(1, POOL_DIM), w_b.astype(BF16))


def _mix_body(o_ref, ga_ref, cb_ref, x_ref, g1_ref, nf_ref, sc_ref, sh_ref, wa_ref, wo_ref, wr_ref,
              h_ref, u_ref, lg_ref):
    ya = jnp.dot(o_ref[...], wa_ref[...], preferred_element_type=F32)
    merged = ya * _sigmoid(ga_ref[...].astype(F32)) + cb_ref[...].astype(F32)
    mo = jnp.dot(merged.astype(BF16), wo_ref[...], preferred_element_type=F32)
    h = x_ref[...] + g1_ref[0] * mo
    h_ref[...] = h
    ms = jnp.mean(h * h, axis=-1, keepdims=True)
    u = h * lax.rsqrt(ms + EPS) * nf_ref[...]
    u = u * (1.0 + sc_ref[0]) + sh_ref[0]
    ub = u.astype(BF16)
    u_ref[...] = ub.reshape(u_ref.shape)
    lg_ref[...] = jnp.dot(ub, wr_ref[...], preferred_element_type=F32)


def _mix(o_gla, z, cb, x2, gate1, norm_ffn, scale2, shift2, w_a, w_out, w_router, seq):
    t = x2.shape[0]
    tm = min(TM_MIX, seq)
    per_seq = seq // tm
    ga_blk = (2 * KEY_DIM + 2 * VAL_DIM) // D
    tok = lambda i: (i, 0)
    const = lambda i: (0, 0)
    per_b = lambda i: (i // per_seq, 0, 0)
    return pl.pallas_call(
        _mix_body,
        grid=(t // tm,),
        in_specs=[pl.BlockSpec((tm, VAL_DIM), tok),
                  pl.BlockSpec((tm, D), lambda i: (i, ga_blk)),
                  pl.BlockSpec((tm, D), tok),
                  pl.BlockSpec((tm, D), tok),
                  pl.BlockSpec((1, 1, D), per_b),
                  pl.BlockSpec((1, D), const),
                  pl.BlockSpec((1, 1, D), per_b),
                  pl.BlockSpec((1, 1, D), per_b),
                  pl.BlockSpec((VAL_DIM, D), const, pipeline_mode=pl.Buffered(1)),
                  pl.BlockSpec((D, D), const, pipeline_mode=pl.Buffered(1)),
                  pl.BlockSpec((D, LANES), const)],
        out_specs=[pl.BlockSpec((tm, D), tok),
                   pl.BlockSpec((tm, ROW_SUB, LANES), lambda i: (i, 0, 0)),
                   pl.BlockSpec((tm, LANES), tok)],
        out_shape=[jax.ShapeDtypeStruct((t, D), F32),
                   jax.ShapeDtypeStruct((t, ROW_SUB, LANES), BF16),
                   jax.ShapeDtypeStruct((t, LANES), F32)],
        compiler_params=_params(("arbitrary",)),
        name="mix",
    )(o_gla, z, cb, x2, gate1, norm_ffn, scale2, shift2, w_a, w_out, w_router)


def _route_topk(gl, tm):
    lane = lax.broadcasted_iota(I32, (tm, LANES), 1)
    lanef = lane.astype(F32)
    neg = float("-inf")
    big = float(LANES)
    gmask = lane < N_GROUPS
    gmax = jnp.max(jnp.where(gmask, gl, neg), axis=-1, keepdims=True)
    gsel = jnp.min(jnp.where(gmask & (gl == gmax), lanef, big), axis=-1, keepdims=True)
    gsum = jnp.sum(jnp.where(gmask, jnp.exp(gl - gmax), 0.0), axis=-1, keepdims=True)
    pg = 1.0 / gsum
    lo = N_GROUPS + gsel * EXPERTS_PER_GROUP
    emask = (lanef >= lo) & (lanef < lo + EXPERTS_PER_GROUP)
    emax = jnp.max(jnp.where(emask, gl, neg), axis=-1, keepdims=True)
    ee = jnp.where(emask, jnp.exp(gl - emax), 0.0)
    eprob = ee / jnp.sum(ee, axis=-1, keepdims=True)
    v1 = jnp.max(jnp.where(emask, eprob, -1.0), axis=-1, keepdims=True)
    i1 = jnp.min(jnp.where(emask & (eprob == v1), lanef, big), axis=-1, keepdims=True)
    m2 = emask & (lanef != i1)
    v2 = jnp.max(jnp.where(m2, eprob, -1.0), axis=-1, keepdims=True)
    i2 = jnp.min(jnp.where(m2 & (eprob == v2), lanef, big), axis=-1, keepdims=True)
    tot = v1 + v2
    w1 = pg * (v1 / tot)
    w2 = pg * (v2 / tot)
    hit1 = lanef == i1 - N_GROUPS
    hit2 = lanef == i2 - N_GROUPS
    return hit1, hit2, w1, w2


def _count_body(lg_ref, bias_ref, cnt_ref, *, tm):
    @pl.when(pl.program_id(0) == 0)
    def _():
        cnt_ref[...] = jnp.zeros_like(cnt_ref)

    hit1, hit2, _, _ = _route_topk(lg_ref[...] + bias_ref[...], tm)
    oh = jnp.where(hit1 | hit2, 1.0, 0.0)
    cnt_ref[...] += jnp.sum(oh, axis=0, keepdims=True)


def _assign_body(lg_ref, bias_ref, cnt_ref, ls_ref, ut_ref, pos_ref, wt_ref, carry_ref, pst_ref, *, tm):
    lane = lax.broadcasted_iota(I32, (tm, LANES), 1)

    @pl.when(pl.program_id(0) == 0)
    def _():
        cnt = cnt_ref[0:1, :]
        nblk = jnp.floor((cnt + (ROW_BLOCK - 1)) * (1.0 / ROW_BLOCK))
        hi = jnp.floor(nblk * (1.0 / 256.0))
        lo = nblk - 256.0 * hi
        ut = ut_ref[...]
        hi8 = jnp.broadcast_to(hi, (8, LANES)).astype(BF16)
        lo8 = jnp.broadcast_to(lo, (8, LANES)).astype(BF16)
        pre = (256.0 * jnp.dot(hi8, ut, preferred_element_type=F32)
               + jnp.dot(lo8, ut, preferred_element_type=F32))
        pst_ref[...] = pre[0:1] * float(ROW_BLOCK)
        carry_ref[...] = jnp.zeros_like(carry_ref)

    hit1, hit2, w1, w2 = _route_topk(lg_ref[...] + bias_ref[...], tm)
    oh = jnp.where(hit1 | hit2, 1.0, 0.0)
    cum = jnp.dot(ls_ref[...], oh.astype(BF16), preferred_element_type=F32)
    slot = cum + carry_ref[...] + pst_ref[...]
    carry_ref[...] += jnp.sum(oh, axis=0, keepdims=True)
    p1 = jnp.sum(jnp.where(hit1, slot, 0.0), axis=-1, keepdims=True)
    p2 = jnp.sum(jnp.where(hit2, slot, 0.0), axis=-1, keepdims=True)
    pos_ref[...] = jnp.where(lane == 0, p1, jnp.where(lane == 1, p2, 0.0)).astype(I32)
    wt_ref[...] = jnp.where(lane == 0, w1, jnp.where(lane == 1, w2, 0.0))


def _route(logits, bias):
    t = logits.shape[0]
    tm = min(TM_ROUTE, t)
    r = jnp.arange(tm)
    ls = jnp.where(r[:, None] > r[None, :], 1.0, 0.0).astype(BF16)
    e = jnp.arange(LANES)
    ut = jnp.where(e[:, None] < e[None, :], 1.0, 0.0).astype(BF16)
    tok = lambda i: (i, 0)
    const = lambda i: (0, 0)
    cnt = pl.pallas_call(
        functools.partial(_count_body, tm=tm),
        grid=(t // tm,),
        in_specs=[pl.BlockSpec((tm, LANES), tok), pl.BlockSpec((1, LANES), const)],
        out_specs=pl.BlockSpec((8, LANES), const),
        out_shape=jax.ShapeDtypeStruct((8, LANES), F32),
        compiler_params=_params(("arbitrary",)),
        name="route_count",
    )(logits, bias)
    pos, wts = pl.pallas_call(
        functools.partial(_assign_body, tm=tm),
        grid=(t // tm,),
        in_specs=[pl.BlockSpec((tm, LANES), tok),
                  pl.BlockSpec((1, LANES), const),
                  pl.BlockSpec((8, LANES), const),
                  pl.BlockSpec((tm, tm), const),
                  pl.BlockSpec((LANES, LANES), const)],
        out_specs=[pl.BlockSpec((tm, LANES), tok), pl.BlockSpec((tm, LANES), tok)],
        out_shape=[jax.ShapeDtypeStruct((t, LANES), I32), jax.ShapeDtypeStruct((t, LANES), F32)],
        scratch_shapes=[pltpu.VMEM((1, LANES), F32), pltpu.VMEM((1, LANES), F32)],
        compiler_params=_params(("arbitrary",)),
        name="route_assign",
    )(logits, bias, cnt, ls, ut)
    return pos, wts, cnt


def _row_copy(src, s, dst, d, sem):
    return pltpu.make_async_copy(src.at[s], dst.at[d], sem)


def _list_copy(src, b, dst, s, sem):
    return pltpu.make_async_copy(src.at[b], dst.at[s], sem.at[s])


def _expert_body(be_ref, nu_ref, tok_ref, u_ref, wg_ref, wu_ref, wd_ref, y_ref, xbuf, tokbuf, sem, tsem):
    i = pl.program_id(0)
    nu = nu_ref[0]
    slot = i % 2
    used = i < nu

    def gather(s):
        def issue(p, carry):
            for k in range(2):
                r = 2 * p + k
                _row_copy(u_ref, tokbuf[s, 0, r], xbuf.at[s], r, sem.at[s]).start(priority=k)
            return carry
        lax.fori_loop(0, ROW_BLOCK // 2, issue, 0, unroll=4)

    @pl.when((i == 0) & used)
    def _():
        _list_copy(tok_ref, 0, tokbuf, 0, tsem).start()
        _list_copy(tok_ref, 0, tokbuf, 0, tsem).wait()
        gather(0)

        @pl.when(1 < nu)
        def _():
            _list_copy(tok_ref, 1, tokbuf, 1, tsem).start()

    @pl.when(used)
    def _():
        @pl.when(i + 2 < nu)
        def _():
            _list_copy(tok_ref, i + 2, tokbuf, slot, tsem).start()

        @pl.when(i + 1 < nu)
        def _():
            _list_copy(tok_ref, i + 1, tokbuf, 1 - slot, tsem).wait()
            gather(1 - slot)

        def drain(r, carry):
            _row_copy(u_ref, 0, xbuf.at[slot], 0, sem.at[slot]).wait()
            return carry
        lax.fori_loop(0, ROW_BLOCK, drain, 0, unroll=8)

        x = xbuf[slot].reshape(ROW_BLOCK, D)
        a = jnp.dot(x, wg_ref[0], preferred_element_type=F32)
        b = jnp.dot(x, wu_ref[0], preferred_element_type=F32)
        hdn = (a * _sigmoid(a) * b).astype(BF16)
        y = jnp.dot(hdn, wd_ref[0], preferred_element_type=F32)
        y_ref[...] = y.astype(BF16).reshape(y_ref.shape)

    @pl.when(jnp.logical_not(used))
    def _():
        y_ref[...] = jnp.zeros_like(y_ref)


def _experts(u3, slot_tok, blk_e, n_used, w_gate, w_up, w_down):
    n_rows = slot_tok.shape[0]
    nb = n_rows // ROW_BLOCK
    rows = lambda i, be, nu: (i, 0, 0)
    wsel = lambda i, be, nu: (be[i], 0, 0)
    grid_spec = pltpu.PrefetchScalarGridSpec(
        num_scalar_prefetch=2,
        grid=(nb,),
        in_specs=[pl.BlockSpec(memory_space=pl.ANY),
                  pl.BlockSpec(memory_space=pl.ANY),
                  pl.BlockSpec((1, D, EXPERT_HIDDEN), wsel),
                  pl.BlockSpec((1, D, EXPERT_HIDDEN), wsel),
                  pl.BlockSpec((1, EXPERT_HIDDEN, D), wsel)],
        out_specs=pl.BlockSpec((ROW_BLOCK, ROW_SUB, LANES), rows),
        scratch_shapes=[pltpu.VMEM((2, ROW_BLOCK, ROW_SUB, LANES), BF16),
                        pltpu.SMEM((2, 1, ROW_BLOCK), I32),
                        pltpu.SemaphoreType.DMA((2,)),
                        pltpu.SemaphoreType.DMA((2,))],
    )
    return pl.pallas_call(
        _expert_body,
        grid_spec=grid_spec,
        out_shape=jax.ShapeDtypeStruct((n_rows, ROW_SUB, LANES), BF16),
        compiler_params=_params(("arbitrary",)),
        name="experts",
    )(blk_e, n_used, slot_tok.reshape(nb, 1, ROW_BLOCK), u3, w_gate, w_up, w_down)


def _combine_body(cur_ref, nxt_ref, yb_ref, wt_ref, h_ref, g2_ref, nf_ref, o_ref, gbuf, sem, *, tm, n_steps):
    i = pl.program_id(0)
    slot = i % 2

    def gather(pos_ref, s):
        def issue(r, carry):
            for k in range(2):
                _row_copy(yb_ref, pos_ref[0, 0, 2 * r + k], gbuf.at[s, k], r, sem.at[s]).start(priority=k)
            return carry
        lax.fori_loop(0, tm, issue, 0, unroll=4)

    @pl.when(i == 0)
    def _():
        gather(cur_ref, 0)

    @pl.when(i + 1 < n_steps)
    def _():
        gather(nxt_ref, 1 - slot)

    def drain(r, carry):
        for k in range(2):
            _row_copy(yb_ref, 0, gbuf.at[slot, k], 0, sem.at[slot]).wait()
        return carry
    lax.fori_loop(0, tm, drain, 0, unroll=4)

    wt = wt_ref[...]
    ya = gbuf[slot, 0].reshape(tm, D).astype(F32)
    yb = gbuf[slot, 1].reshape(tm, D).astype(F32)
    y = wt[:, 0:1] * ya + wt[:, 1:2] * yb
    h = h_ref[...] + g2_ref[0] * y
    ms = jnp.mean(h * h, axis=-1, keepdims=True)
    o_ref[...] = h * lax.rsqrt(ms + EPS) * nf_ref[...]


def _combine(pos2, yb, wts, h1, gate2, norm_final, seq):
    t = h1.shape[0]
    tm = min(TM_ROWS, seq)
    per_seq = seq // tm
    tok = lambda i: (i, 0)
    n_steps = t // tm
    pos3 = pos2.reshape(n_steps, 1, 2 * tm)
    return pl.pallas_call(
        functools.partial(_combine_body, tm=tm, n_steps=n_steps),
        grid=(n_steps,),
        in_specs=[pl.BlockSpec((1, 1, 2 * tm), lambda i: (i, 0, 0), memory_space=pltpu.SMEM),
                  pl.BlockSpec((1, 1, 2 * tm), lambda i: (jnp.minimum(i + 1, n_steps - 1), 0, 0),
                               memory_space=pltpu.SMEM),
                  pl.BlockSpec(memory_space=pl.ANY),
                  pl.BlockSpec((tm, LANES), tok),
                  pl.BlockSpec((tm, D), tok),
                  pl.BlockSpec((1, 1, D), lambda i: (i // per_seq, 0, 0)),
                  pl.BlockSpec((1, D), lambda i: (0, 0))],
        out_specs=pl.BlockSpec((tm, D), tok),
        out_shape=jax.ShapeDtypeStruct((t, D), F32),
        scratch_shapes=[pltpu.VMEM((2, 2, tm, ROW_SUB, LANES), BF16), pltpu.SemaphoreType.DMA((2,))],
        compiler_params=_params(("arbitrary",)),
        name="combine",
    )(pos3, pos3, yb, wts, h1, gate2, norm_final)


def _layer(h2d, c, bsz, seq, w_ada, b_ada, norm_mix, w_in, w_gk2, b_gk2, gla_norm, w_a, w_pool, pool_scale, w_b,
           w_out, norm_ffn, w_rg, b_rg, w_re, b_re, w_gate, w_up, w_down):
    t = bsz * seq
    mod = _ada(c, w_ada, b_ada)
    shift1, scale1, gate1, shift2, scale2, gate2 = [m.reshape(bsz, 1, D) for m in jnp.split(mod, 6, axis=-1)]

    c_gk = 2 * KEY_DIM + 2 * VAL_DIM
    c_p = c_gk + GATE_RANK
    c_ga = c_p + POOL_DIM
    w_main = jnp.concatenate([w_in[:, :c_gk], w_in[:, c_ga:], w_in[:, c_p:c_ga]], axis=1).astype(BF16)
    w_gk = jnp.zeros((D, LANES), F32).at[:, :GATE_RANK].set(w_in[:, c_gk:c_p]).astype(BF16)

    z, gk, wg16, wu16, wd16 = _inproj(h2d, norm_mix.reshape(1, D), scale1, shift1, w_main, w_gk,
                                      w_gate, w_up, w_down, seq)
    o_gla = _gla(z, gk, w_gk2, b_gk2, gla_norm, bsz, seq)
    cb = _pool(z, w_pool, pool_scale, w_b, seq)

    w_router = jnp.zeros((D, LANES), F32).at[:, :N_GROUPS].set(w_rg)
    w_router = w_router.at[:, N_GROUPS:N_GROUPS + N_EXPERTS].set(w_re).astype(BF16)
    h1, u2, logits = _mix(o_gla, z, cb, h2d, gate1, norm_ffn.reshape(1, D), scale2, shift2,
                          w_a.astype(BF16), w_out.astype(BF16), w_router, seq)

    bias = jnp.zeros((1, LANES), F32).at[0, :N_GROUPS].set(b_rg).at[0, N_GROUPS:N_GROUPS + N_EXPERTS].set(b_re)
    pos, wts, cnt = _route(logits, bias)
    pos2 = pos[:, :2]

    counts = cnt[0, :N_EXPERTS].astype(I32)
    nblk = (counts + ROW_BLOCK - 1) // ROW_BLOCK
    bends = jnp.cumsum(nblk)
    n_rows = 2 * t + N_EXPERTS * ROW_BLOCK
    n_blocks = n_rows // ROW_BLOCK
    blk_e = jnp.sum(bends[None, :] <= jnp.arange(n_blocks, dtype=I32)[:, None], axis=1)
    blk_e = jnp.minimum(blk_e, N_EXPERTS - 1).astype(I32)
    n_used = bends[-1:].astype(I32)
    slot_tok = jnp.zeros((n_rows,), I32).at[pos2.reshape(-1)].set(
        jnp.arange(2 * t, dtype=I32) // 2, unique_indices=True)

    yb = _experts(u2, slot_tok, blk_e, n_used, wg16, wu16, wd16)
    return pos2, yb, wts, h1, gate2


def kernel(x, c, w_ada, b_ada, norm_mix, w_in, w_gk2, b_gk2, gla_norm, w_a, w_pool, pool_scale, w_b, w_out, norm_ffn, w_rg, b_rg, w_re, b_re, w_gate, w_up, w_down, norm_final):
    bsz, seq, _ = x.shape
    depth = w_ada.shape[0]
    assert depth == 1
    h2d = x.reshape(bsz * seq, D)
    pos2, yb, wts, h1, gate2 = _layer(
        h2d, c, bsz, seq, w_ada[0], b_ada[0], norm_mix[0], w_in[0], w_gk2[0], b_gk2[0], gla_norm[0], w_a[0],
        w_pool[0], pool_scale[0], w_b[0], w_out[0], norm_ffn[0], w_rg[0], b_rg[0], w_re[0], b_re[0],
        w_gate[0], w_up[0], w_down[0])
    out = _combine(pos2, yb, wts, h1, gate2, norm_final.reshape(1, D), seq)
    return out.reshape(bsz, seq, D)
```

```python
import functools

import jax
import jax.numpy as jnp
from jax import lax
from jax.experimental import pallas as pl
from jax.experimental.pallas import tpu as pltpu

F32 = jnp.float32
BF16 = jnp.bfloat16
I32 = jnp.int32

D = 2048
HEADS = 4
DK = 256
DV = 512
KEY_DIM = HEADS * DK
VAL_DIM = HEADS * DV
GATE_RANK = 16
GATE_NORMALIZER = 16.0
CHUNK = 64
POOL_WINDOWS = (2, 4, 8, 16)
POOL_DIM = 1024
POOL_GROUP_DIM = 256
POOL_HALO = 16
N_GROUPS = 8
EXPERTS_PER_GROUP = 8
N_EXPERTS = 64
EXPERT_HIDDEN = 1024
EPS = 1e-6
LANES = 128
ROW_SUB = D // LANES

Z_COLS = 2 * KEY_DIM + 2 * VAL_DIM + 2 * D + POOL_DIM

VMEM_LIMIT = 56 * 1024 * 1024
VMEM_LIMIT_INPROJ = 60 * 1024 * 1024
CAST_STEPS = 8

TM_IN = 1024
TN_IN = 1024
TC_GLA = 512
TM_POOL = 256
TM_MIX = 256
TM_ROUTE = 512
TM_ROWS = 256
ROW_BLOCK = 512
SUB_ROWS = 128


def _sigmoid(x):
    return 1.0 / (1.0 + jnp.exp(-x))


def _params(sem, vmem=VMEM_LIMIT):
    return pltpu.CompilerParams(dimension_semantics=sem, vmem_limit_bytes=vmem)


def _ada_body(c_ref, w_ref, b_ref, o_ref):
    c = c_ref[...]
    s = (c * _sigmoid(c)).astype(BF16)
    o_ref[...] = jnp.dot(s, w_ref[...].astype(BF16), preferred_element_type=F32) + b_ref[...]


def _ada(c, w, b):
    bsz = c.shape[0]
    cp = jnp.zeros((8, D), F32).at[:bsz].set(c)
    n = w.shape[1]
    tn = 1024
    out = pl.pallas_call(
        _ada_body,
        grid=(n // tn,),
        in_specs=[pl.BlockSpec((8, D), lambda j: (0, 0)),
                  pl.BlockSpec((D, tn), lambda j: (0, j)),
                  pl.BlockSpec((1, tn), lambda j: (0, j))],
        out_specs=pl.BlockSpec((8, tn), lambda j: (0, j)),
        out_shape=jax.ShapeDtypeStruct((8, n), F32),
        compiler_params=_params(("arbitrary",)),
        name="ada",
    )(cp, w, b.reshape(1, n))
    return out[:bsz]


def _inproj_body(x_ref, g_ref, sc_ref, sh_ref, w_ref, wgk_ref, eg_ref, eu_ref, ed_ref,
                 z_ref, gk_ref, og_ref, ou_ref, od_ref, u_scr):
    @pl.when(pl.program_id(1) < CAST_STEPS)
    def _():
        og_ref[...] = eg_ref[...].astype(BF16)
        ou_ref[...] = eu_ref[...].astype(BF16)
        od_ref[...] = ed_ref[...].astype(BF16)

    @pl.when(pl.program_id(1) == 0)
    def _():
        x = x_ref[...]
        ms = jnp.mean(x * x, axis=-1, keepdims=True)
        u = x * lax.rsqrt(ms + EPS) * g_ref[...]
        u = (u * (1.0 + sc_ref[0]) + sh_ref[0]).astype(BF16)
        u_scr[...] = u
        gk_ref[...] = jnp.dot(u, wgk_ref[...], preferred_element_type=F32)

    z_ref[...] = jnp.dot(u_scr[...], w_ref[...], preferred_element_type=F32).astype(BF16)


def _inproj(x2, gain, scale, shift, w_main, w_gk, w_gate, w_up, w_down, seq):
    t = x2.shape[0]
    tm = min(TM_IN, seq)
    per_seq = seq // tm
    n_i = t // tm
    n_j = Z_COLS // TN_IN
    assert CAST_STEPS <= n_j
    n_slabs = n_i * CAST_STEPS
    eg = w_gate.reshape(N_EXPERTS * D, EXPERT_HIDDEN)
    eu = w_up.reshape(N_EXPERTS * D, EXPERT_HIDDEN)
    ed = w_down.reshape(N_EXPERTS * EXPERT_HIDDEN, D)
    rg, rd = eg.shape[0] // n_slabs, ed.shape[0] // n_slabs
    assert rg * n_slabs == eg.shape[0] and rd * n_slabs == ed.shape[0] and rd % 16 == 0
    slab = lambda i, j: (i * CAST_STEPS + jnp.minimum(j, CAST_STEPS - 1), 0)
    z, gk, og, ou, od = pl.pallas_call(
        _inproj_body,
        grid=(n_i, n_j),
        in_specs=[pl.BlockSpec((tm, D), lambda i, j: (i, 0)),
                  pl.BlockSpec((1, D), lambda i, j: (0, 0)),
                  pl.BlockSpec((1, 1, D), lambda i, j: (i // per_seq, 0, 0)),
                  pl.BlockSpec((1, 1, D), lambda i, j: (i // per_seq, 0, 0)),
                  pl.BlockSpec((D, TN_IN), lambda i, j: (0, j)),
                  pl.BlockSpec((D, LANES), lambda i, j: (0, 0)),
                  pl.BlockSpec((rg, EXPERT_HIDDEN), slab),
                  pl.BlockSpec((rg, EXPERT_HIDDEN), slab),
                  pl.BlockSpec((rd, D), slab)],
        out_specs=[pl.BlockSpec((tm, TN_IN), lambda i, j: (i, j)),
                   pl.BlockSpec((tm, LANES), lambda i, j: (i, 0)),
                   pl.BlockSpec((rg, EXPERT_HIDDEN), slab),
                   pl.BlockSpec((rg, EXPERT_HIDDEN), slab),
                   pl.BlockSpec((rd, D), slab)],
        out_shape=[jax.ShapeDtypeStruct((t, Z_COLS), BF16),
                   jax.ShapeDtypeStruct((t, LANES), F32),
                   jax.ShapeDtypeStruct(eg.shape, BF16),
                   jax.ShapeDtypeStruct(eu.shape, BF16),
                   jax.ShapeDtypeStruct(ed.shape, BF16)],
        scratch_shapes=[pltpu.VMEM((tm, D), BF16)],
        compiler_params=_params(("arbitrary", "arbitrary"), vmem=VMEM_LIMIT_INPROJ),
        name="inproj",
    )(x2, gain, scale, shift, w_main, w_gk, eg, eu, ed)
    return (z, gk, og.reshape(w_gate.shape), ou.reshape(w_up.shape), od.reshape(w_down.shape))


def _gla_body(q_ref, k_ref, v_ref, g_ref, gk_ref, wgk_ref, bgk_ref, gn_ref, ll_ref, o_ref, st_ref, *, n_chunks):
    @pl.when(pl.program_id(1) == 0)
    def _():
        st_ref[...] = jnp.zeros_like(st_ref)

    row = lax.broadcasted_iota(I32, (CHUNK, CHUNK), 0)
    col = lax.broadcasted_iota(I32, (CHUNK, CHUNK), 1)
    causal = row >= col
    nt = (((1,), (1,)), ((), ()))
    tn = (((0,), (0,)), ((), ()))

    def chunk(ci, carry):
        r0 = pl.multiple_of(ci * CHUNK, CHUNK)
        rows = pl.ds(r0, CHUNK)
        gk = gk_ref[rows, :].astype(BF16)
        xg = jnp.dot(gk, wgk_ref[...], preferred_element_type=F32) + bgk_ref[...]
        la = (jnp.minimum(xg, 0.0) - jnp.log(1.0 + jnp.exp(-jnp.abs(xg)))) * (1.0 / GATE_NORMALIZER)
        la_hi = la.astype(BF16)
        la_lo = (la - la_hi.astype(F32)).astype(BF16)
        ll = ll_ref[...]
        bb = (jnp.dot(ll, la_hi, preferred_element_type=F32)
              + jnp.dot(ll, la_lo, preferred_element_type=F32))
        b = bb[:CHUNK]
        bl = bb[CHUNK:]
        q = q_ref[rows, :].astype(F32)
        k = k_ref[rows, :].astype(F32)
        qe = (q * (DK ** -0.5) * jnp.exp(b)).astype(BF16)
        ke = (k * jnp.exp(-b)).astype(BF16)
        kd = (k * jnp.exp(bl - b)).astype(BF16)
        dec = jnp.exp(bl[0:1, :])
        for h in range(HEADS):
            sk = slice(h * DK, (h + 1) * DK)
            sv = slice(h * DV, (h + 1) * DV)
            att = lax.dot_general(qe[:, sk], ke[:, sk], nt, preferred_element_type=F32)
            att = jnp.where(causal, att, 0.0).astype(BF16)
            vc = v_ref[rows, sv]
            st = st_ref[h]
            o = (jnp.dot(att, vc, preferred_element_type=F32)
                 + lax.dot_general(qe[:, sk], st.astype(BF16), nt, preferred_element_type=F32))
            st_ref[h] = st * dec[:, sk] + lax.dot_general(vc, kd[:, sk], tn, preferred_element_type=F32)
            ms = jnp.mean(o * o, axis=-1, keepdims=True)
            on = o * lax.rsqrt(ms + EPS) * gn_ref[...]
            gg = g_ref[rows, sv].astype(F32)
            o_ref[rows, sv] = (on * (gg * _sigmoid(gg))).astype(BF16)
        return carry

    lax.fori_loop(0, n_chunks, chunk, 0, unroll=4)


def _gla(z, gk, w_gk2, b_gk2, gla_norm, bsz, seq):
    t = z.shape[0]
    tc = min(TC_GLA, seq)
    per_seq = seq // tc
    wgk = jnp.zeros((LANES, KEY_DIM), F32).at[:GATE_RANK].set(w_gk2).astype(BF16)
    r = jnp.arange(2 * CHUNK)[:, None]
    c = jnp.arange(CHUNK)[None, :]
    ll = jnp.where((r >= CHUNK) | (r >= c), 1.0, 0.0).astype(BF16)
    kb = KEY_DIM // KEY_DIM
    return pl.pallas_call(
        functools.partial(_gla_body, n_chunks=tc // CHUNK),
        grid=(bsz, per_seq),
        in_specs=[pl.BlockSpec((tc, KEY_DIM), lambda b, s: (b * per_seq + s, 0)),
                  pl.BlockSpec((tc, KEY_DIM), lambda b, s: (b * per_seq + s, kb)),
                  pl.BlockSpec((tc, VAL_DIM), lambda b, s: (b * per_seq + s, 1)),
                  pl.BlockSpec((tc, VAL_DIM), lambda b, s: (b * per_seq + s, 2)),
                  pl.BlockSpec((tc, LANES), lambda b, s: (b * per_seq + s, 0)),
                  pl.BlockSpec((LANES, KEY_DIM), lambda b, s: (0, 0)),
                  pl.BlockSpec((1, KEY_DIM), lambda b, s: (0, 0)),
                  pl.BlockSpec((1, DV), lambda b, s: (0, 0)),
                  pl.BlockSpec((2 * CHUNK, CHUNK), lambda b, s: (0, 0))],
        out_specs=pl.BlockSpec((tc, VAL_DIM), lambda b, s: (b * per_seq + s, 0)),
        out_shape=jax.ShapeDtypeStruct((t, VAL_DIM), BF16),
        scratch_shapes=[pltpu.VMEM((HEADS, DV, DK), F32)],
        compiler_params=_params(("arbitrary", "arbitrary")),
        name="gla",
    )(z, z, z, z, gk, wgk, b_gk2.reshape(1, KEY_DIM), gla_norm.reshape(1, DV), ll)


def _pool_body(p_ref, halo_ref, gb_ref, wp_ref, ps_ref, wb_ref, o_ref, *, tm, seq):
    base = (pl.program_id(0) * tm) % seq
    pf = p_ref[...].astype(F32)
    hal = halo_ref[...].astype(F32)
    hal = jnp.where(base == 0, 0.0, hal)
    ext = jnp.concatenate([hal, pf], axis=0)
    pos = base + lax.broadcasted_iota(I32, (tm, 1), 0)
    ys = []
    for gi, w in enumerate(POOL_WINDOWS):
        cols = slice(gi * POOL_GROUP_DIM, (gi + 1) * POOL_GROUP_DIM)
        s = ext[:, cols]
        sh = 1
        while sh < w:
            s = s + pltpu.roll(s, sh, 0)
            sh *= 2
        cnt = jnp.minimum(pos + 1, w).astype(F32)
        m = s[POOL_HALO:] / cnt - pf[:, cols]
        ys.append(jnp.dot(m.astype(BF16), wp_ref[gi], preferred_element_type=F32))
    y = jnp.concatenate(ys, axis=1) * ps_ref[...]
    yb = jnp.dot(y.astype(BF16), wb_ref[...], preferred_element_type=F32)
    o_ref[...] = (yb * _sigmoid(gb_ref[...].astype(F32))).astype(BF16)


def _pool(z, w_pool, pool_scale, w_b, seq):
    t = z.shape[0]
    tm = min(TM_POOL, seq)
    p_blk = (Z_COLS - POOL_DIM) // POOL_DIM
    gb_blk = (2 * KEY_DIM + 2 * VAL_DIM + D) // D
    hb = tm // POOL_HALO
    return pl.pallas_call(
        functools.partial(_pool_body, tm=tm, seq=seq),
        grid=(t // tm,),
        in_specs=[pl.BlockSpec((tm, POOL_DIM), lambda i: (i, p_blk)),
                  pl.BlockSpec((POOL_HALO, POOL_DIM), lambda i: (jnp.maximum(i * hb - 1, 0), p_blk)),
                  pl.BlockSpec((tm, D), lambda i: (i, gb_blk)),
                  pl.BlockSpec((len(POOL_WINDOWS), POOL_GROUP_DIM, POOL_GROUP_DIM), lambda i: (0, 0, 0)),
                  pl.BlockSpec((1, POOL_DIM), lambda i: (0, 0)),
                  pl.BlockSpec((POOL_DIM, D), lambda i: (0, 0))],
        out_specs=pl.BlockSpec((tm, D), lambda i: (i, 0)),
        out_shape=jax.ShapeDtypeStruct((t, D), BF16),
        compiler_params=_params(("arbitrary",)),
        name="pool",
    )(z, z, z, w_pool.astype(BF16), pool_scale.reshape(1, POOL_DIM), w_b.astype(BF16))


def _mix_body(o_ref, ga_ref, cb_ref, x_ref, g1_ref, nf_ref, sc_ref, sh_ref, wa_ref, wo_ref, wr_ref,
              h_ref, u_ref, lg_ref):
    ya = jnp.dot(o_ref[...], wa_ref[...], preferred_element_type=F32)
    merged = ya * _sigmoid(ga_ref[...].astype(F32)) + cb_ref[...].astype(F32)
    mo = jnp.dot(merged.astype(BF16), wo_ref[...], preferred_element_type=F32)
    h = x_ref[...] + g1_ref[0] * mo
    h_ref[...] = h
    ms = jnp.mean(h * h, axis=-1, keepdims=True)
    u = h * lax.rsqrt(ms + EPS) * nf_ref[...]
    u = u * (1.0 + sc_ref[0]) + sh_ref[0]
    ub = u.astype(BF16)
    u_ref[...] = ub.reshape(u_ref.shape)
    lg_ref[...] = jnp.dot(ub, wr_ref[...], preferred_element_type=F32)


def _mix(o_gla, z, cb, x2, gate1, norm_ffn, scale2, shift2, w_a, w_out, w_router, seq):
    t = x2.shape[0]
    tm = min(TM_MIX, seq)
    per_seq = seq // tm
    ga_blk = (2 * KEY_DIM + 2 * VAL_DIM) // D
    tok = lambda i: (i, 0)
    const = lambda i: (0, 0)
    per_b = lambda i: (i // per_seq, 0, 0)
    return pl.pallas_call(
        _mix_body,
        grid=(t // tm,),
        in_specs=[pl.BlockSpec((tm, VAL_DIM), tok),
                  pl.BlockSpec((tm, D), lambda i: (i, ga_blk)),
                  pl.BlockSpec((tm, D), tok),
                  pl.BlockSpec((tm, D), tok),
                  pl.BlockSpec((1, 1, D), per_b),
                  pl.BlockSpec((1, D), const),
                  pl.BlockSpec((1, 1, D), per_b),
                  pl.BlockSpec((1, 1, D), per_b),
                  pl.BlockSpec((VAL_DIM, D), const, pipeline_mode=pl.Buffered(1)),
                  pl.BlockSpec((D, D), const, pipeline_mode=pl.Buffered(1)),
                  pl.BlockSpec((D, LANES), const)],
        out_specs=[pl.BlockSpec((tm, D), tok),
                   pl.BlockSpec((tm, ROW_SUB, LANES), lambda i: (i, 0, 0)),
                   pl.BlockSpec((tm, LANES), tok)],
        out_shape=[jax.ShapeDtypeStruct((t, D), F32),
                   jax.ShapeDtypeStruct((t, ROW_SUB, LANES), BF16),
                   jax.ShapeDtypeStruct((t, LANES), F32)],
        compiler_params=_params(("arbitrary",)),
        name="mix",
    )(o_gla, z, cb, x2, gate1, norm_ffn, scale2, shift2, w_a, w_out, w_router)


def _route_topk(gl, tm):
    lane = lax.broadcasted_iota(I32, (tm, LANES), 1)
    lanef = lane.astype(F32)
    neg = float("-inf")
    big = float(LANES)
    gmask = lane < N_GROUPS
    gmax = jnp.max(jnp.where(gmask, gl, neg), axis=-1, keepdims=True)
    gsel = jnp.min(jnp.where(gmask & (gl == gmax), lanef, big), axis=-1, keepdims=True)
    gsum = jnp.sum(jnp.where(gmask, jnp.exp(gl - gmax), 0.0), axis=-1, keepdims=True)
    pg = 1.0 / gsum
    lo = N_GROUPS + gsel * EXPERTS_PER_GROUP
    emask = (lanef >= lo) & (lanef < lo + EXPERTS_PER_GROUP)
    emax = jnp.max(jnp.where(emask, gl, neg), axis=-1, keepdims=True)
    ee = jnp.where(emask, jnp.exp(gl - emax), 0.0)
    eprob = ee / jnp.sum(ee, axis=-1, keepdims=True)
    v1 = jnp.max(jnp.where(emask, eprob, -1.0), axis=-1, keepdims=True)
    i1 = jnp.min(jnp.where(emask & (eprob == v1), lanef, big), axis=-1, keepdims=True)
    m2 = emask & (lanef != i1)
    v2 = jnp.max(jnp.where(m2, eprob, -1.0), axis=-1, keepdims=True)
    i2 = jnp.min(jnp.where(m2 & (eprob == v2), lanef, big), axis=-1, keepdims=True)
    tot = v1 + v2
    w1 = pg * (v1 / tot)
    w2 = pg * (v2 / tot)
    hit1 = lanef == i1 - N_GROUPS
    hit2 = lanef == i2 - N_GROUPS
    return hit1, hit2, w1, w2


def _count_body(lg_ref, bias_ref, cnt_ref, *, tm):
    @pl.when(pl.program_id(0) == 0)
    def _():
        cnt_ref[...] = jnp.zeros_like(cnt_ref)

    hit1, hit2, _, _ = _route_topk(lg_ref[...] + bias_ref[...], tm)
    oh = jnp.where(hit1 | hit2, 1.0, 0.0)
    cnt_ref[...] += jnp.sum(oh, axis=0, keepdims=True)


def _assign_body(lg_ref, bias_ref, cnt_ref, ls_ref, ut_ref, pos_ref, wt_ref, carry_ref, pst_ref, *, tm):
    lane = lax.broadcasted_iota(I32, (tm, LANES), 1)

    @pl.when(pl.program_id(0) == 0)
    def _():
        cnt = cnt_ref[0:1, :]
        nblk = jnp.floor((cnt + (ROW_BLOCK - 1)) * (1.0 / ROW_BLOCK))
        hi = jnp.floor(nblk * (1.0 / 256.0))
        lo = nblk - 256.0 * hi
        ut = ut_ref[...]
        hi8 = jnp.broadcast_to(hi, (8, LANES)).astype(BF16)
        lo8 = jnp.broadcast_to(lo, (8, LANES)).astype(BF16)
        pre = (256.0 * jnp.dot(hi8, ut, preferred_element_type=F32)
               + jnp.dot(lo8, ut, preferred_element_type=F32))
        pst_ref[...] = pre[0:1] * float(ROW_BLOCK)
        carry_ref[...] = jnp.zeros_like(carry_ref)

    hit1, hit2, w1, w2 = _route_topk(lg_ref[...] + bias_ref[...], tm)
    oh = jnp.where(hit1 | hit2, 1.0, 0.0)
    cum = jnp.dot(ls_ref[...], oh.astype(BF16), preferred_element_type=F32)
    slot = cum + carry_ref[...] + pst_ref[...]
    carry_ref[...] += jnp.sum(oh, axis=0, keepdims=True)
    p1 = jnp.sum(jnp.where(hit1, slot, 0.0), axis=-1, keepdims=True)
    p2 = jnp.sum(jnp.where(hit2, slot, 0.0), axis=-1, keepdims=True)
    pos_ref[...] = jnp.where(lane == 0, p1, jnp.where(lane == 1, p2, 0.0)).astype(I32)
    wt_ref[...] = jnp.where(lane == 0, w1, jnp.where(lane == 1, w2, 0.0))


def _route(logits, bias):
    t = logits.shape[0]
    tm = min(TM_ROUTE, t)
    r = jnp.arange(tm)
    ls = jnp.where(r[:, None] > r[None, :], 1.0, 0.0).astype(BF16)
    e = jnp.arange(LANES)
    ut = jnp.where(e[:, None] < e[None, :], 1.0, 0.0).astype(BF16)
    tok = lambda i: (i, 0)
    const = lambda i: (0, 0)
    cnt = pl.pallas_call(
        functools.partial(_count_body, tm=tm),
        grid=(t // tm,),
        in_specs=[pl.BlockSpec((tm, LANES), tok), pl.BlockSpec((1, LANES), const)],
        out_specs=pl.BlockSpec((8, LANES), const),
        out_shape=jax.ShapeDtypeStruct((8, LANES), F32),
        compiler_params=_params(("arbitrary",)),
        name="route_count",
    )(logits, bias)
    pos, wts = pl.pallas_call(
        functools.partial(_assign_body, tm=tm),
        grid=(t // tm,),
        in_specs=[pl.BlockSpec((tm, LANES), tok),
                  pl.BlockSpec((1, LANES), const),
                  pl.BlockSpec((8, LANES), const),
                  pl.BlockSpec((tm, tm), const),
                  pl.BlockSpec((LANES, LANES), const)],
        out_specs=[pl.BlockSpec((tm, LANES), tok), pl.BlockSpec((tm, LANES), tok)],
        out_shape=[jax.ShapeDtypeStruct((t, LANES), I32), jax.ShapeDtypeStruct((t, LANES), F32)],
        scratch_shapes=[pltpu.VMEM((1, LANES), F32), pltpu.VMEM((1, LANES), F32)],
        compiler_params=_params(("arbitrary",)),
        name="route_assign",
    )(logits, bias, cnt, ls, ut)
    return pos, wts, cnt


def _row_copy(src, s, dst, d, sem):
    return pltpu.make_async_copy(src.at[s], dst.at[d], sem)


def _list_copy(src, b, dst, s, sem):
    return pltpu.make_async_copy(src.at[b], dst.at[s], sem.at[s])


def _expert_body(be_ref, nr_ref, tok_ref, u_ref, wg_ref, wu_ref, wd_ref, y_ref, xbuf, tokbuf, sem, tsem, *, n_blocks):
    i = pl.program_id(0)
    slot = i % 2
    nrows = nr_ref[i]
    used = nrows > 0
    nrows_next = nr_ref[jnp.minimum(i + 1, n_blocks - 1)]
    next_used = (i + 1 < n_blocks) & (nrows_next > 0)
    next2_used = (i + 2 < n_blocks) & (nr_ref[jnp.minimum(i + 2, n_blocks - 1)] > 0)

    def pairs(n):
        return (n + 1) // 2

    def gather(s, n):
        def issue(p, carry):
            for k in range(2):
                r = 2 * p + k
                _row_copy(u_ref, tokbuf[s, 0, r], xbuf.at[s], r, sem.at[s]).start(priority=k)
            return carry
        lax.fori_loop(0, pairs(n), issue, 0)

    @pl.when(i == 0)
    def _():
        xbuf[...] = jnp.zeros_like(xbuf)

        @pl.when(used)
        def _():
            _list_copy(tok_ref, 0, tokbuf, 0, tsem).start()
            _list_copy(tok_ref, 0, tokbuf, 0, tsem).wait()
            gather(0, nrows)

        @pl.when(next_used)
        def _():
            _list_copy(tok_ref, 1, tokbuf, 1, tsem).start()

    @pl.when(used)
    def _():
        @pl.when(next2_used)
        def _():
            _list_copy(tok_ref, i + 2, tokbuf, slot, tsem).start()

        @pl.when(next_used)
        def _():
            _list_copy(tok_ref, i + 1, tokbuf, 1 - slot, tsem).wait()
            gather(1 - slot, nrows_next)

        def drain(p, carry):
            for k in range(2):
                _row_copy(u_ref, 0, xbuf.at[slot], 0, sem.at[slot]).wait()
            return carry
        lax.fori_loop(0, pairs(nrows), drain, 0)

    nsub = (nrows + SUB_ROWS - 1) // SUB_ROWS
    for k in range(1, ROW_BLOCK // SUB_ROWS + 1):
        @pl.when(nsub == k)
        def _(k=k):
            m = k * SUB_ROWS
            x = xbuf[slot, 0:m].reshape(m, D)
            a = jnp.dot(x, wg_ref[0], preferred_element_type=F32)
            b = jnp.dot(x, wu_ref[0], preferred_element_type=F32)
            hdn = (a * _sigmoid(a) * b).astype(BF16)
            y = jnp.dot(hdn, wd_ref[0], preferred_element_type=F32)
            y_ref[0:m] = y.astype(BF16).reshape(m, ROW_SUB, LANES)
            if m < ROW_BLOCK:
                y_ref[m:] = jnp.zeros((ROW_BLOCK - m, ROW_SUB, LANES), BF16)

    @pl.when(jnp.logical_not(used))
    def _():
        y_ref[...] = jnp.zeros_like(y_ref)


def _experts(u3, slot_tok, blk_e, blk_rows, w_gate, w_up, w_down):
    n_rows = slot_tok.shape[0]
    nb = n_rows // ROW_BLOCK
    rows = lambda i, be, nr: (i, 0, 0)
    wsel = lambda i, be, nr: (be[i], 0, 0)
    grid_spec = pltpu.PrefetchScalarGridSpec(
        num_scalar_prefetch=2,
        grid=(nb,),
        in_specs=[pl.BlockSpec(memory_space=pl.ANY),
                  pl.BlockSpec(memory_space=pl.ANY),
                  pl.BlockSpec((1, D, EXPERT_HIDDEN), wsel),
                  pl.BlockSpec((1, D, EXPERT_HIDDEN), wsel),
                  pl.BlockSpec((1, EXPERT_HIDDEN, D), wsel)],
        out_specs=pl.BlockSpec((ROW_BLOCK, ROW_SUB, LANES), rows),
        scratch_shapes=[pltpu.VMEM((2, ROW_BLOCK, ROW_SUB, LANES), BF16),
                        pltpu.SMEM((2, 1, ROW_BLOCK), I32),
                        pltpu.SemaphoreType.DMA((2,)),
                        pltpu.SemaphoreType.DMA((2,))],
    )
    return pl.pallas_call(
        functools.partial(_expert_body, n_blocks=nb),
        grid_spec=grid_spec,
        out_shape=jax.ShapeDtypeStruct((n_rows, ROW_SUB, LANES), BF16),
        compiler_params=_params(("arbitrary",)),
        name="experts",
    )(blk_e, blk_rows, slot_tok.reshape(nb, 1, ROW_BLOCK), u3, w_gate, w_up, w_down)


def _combine_body(cur_ref, nxt_ref, yb_ref, wt_ref, h_ref, g2_ref, nf_ref, o_ref, gbuf, sem, *, tm, n_steps):
    i = pl.program_id(0)
    slot = i % 2

    def gather(pos_ref, s):
        def issue(r, carry):
            for k in range(2):
                _row_copy(yb_ref, pos_ref[0, 0, 2 * r + k], gbuf.at[s, k], r, sem.at[s]).start(priority=k)
            return carry
        lax.fori_loop(0, tm, issue, 0, unroll=4)

    @pl.when(i == 0)
    def _():
        gather(cur_ref, 0)

    @pl.when(i + 1 < n_steps)
    def _():
        gather(nxt_ref, 1 - slot)

    def drain(r, carry):
        for k in range(2):
            _row_copy(yb_ref, 0, gbuf.at[slot, k], 0, sem.at[slot]).wait()
        return carry
    lax.fori_loop(0, tm, drain, 0, unroll=4)

    wt = wt_ref[...]
    ya = gbuf[slot, 0].reshape(tm, D).astype(F32)
    yb = gbuf[slot, 1].reshape(tm, D).astype(F32)
    y = wt[:, 0:1] * ya + wt[:, 1:2] * yb
    h = h_ref[...] + g2_ref[0] * y
    ms = jnp.mean(h * h, axis=-1, keepdims=True)
    o_ref[...] = h * lax.rsqrt(ms + EPS) * nf_ref[...]


def _combine(pos2, yb, wts, h1, gate2, norm_final, seq):
    t = h1.shape[0]
    tm = min(TM_ROWS, seq)
    per_seq = seq // tm
    tok = lambda i: (i, 0)
    n_steps = t // tm
    pos3 = pos2.reshape(n_steps, 1, 2 * tm)
    return pl.pallas_call(
        functools.partial(_combine_body, tm=tm, n_steps=n_steps),
        grid=(n_steps,),
        in_specs=[pl.BlockSpec((1, 1, 2 * tm), lambda i: (i, 0, 0), memory_space=pltpu.SMEM),
                  pl.BlockSpec((1, 1, 2 * tm), lambda i: (jnp.minimum(i + 1, n_steps - 1), 0, 0),
                               memory_space=pltpu.SMEM),
                  pl.BlockSpec(memory_space=pl.ANY),
                  pl.BlockSpec((tm, LANES), tok),
                  pl.BlockSpec((tm, D), tok),
                  pl.BlockSpec((1, 1, D), lambda i: (i // per_seq, 0, 0)),
                  pl.BlockSpec((1, D), lambda i: (0, 0))],
        out_specs=pl.BlockSpec((tm, D), tok),
        out_shape=jax.ShapeDtypeStruct((t, D), F32),
        scratch_shapes=[pltpu.VMEM((2, 2, tm, ROW_SUB, LANES), BF16), pltpu.SemaphoreType.DMA((2,))],
        compiler_params=_params(("arbitrary",)),
        name="combine",
    )(pos3, pos3, yb, wts, h1, gate2, norm_final)


def _layer(h2d, c, bsz, seq, w_ada, b_ada, norm_mix, w_in, w_gk2, b_gk2, gla_norm, w_a, w_pool, pool_scale, w_b,
           w_out, norm_ffn, w_rg, b_rg, w_re, b_re, w_gate, w_up, w_down):
    t = bsz * seq
    mod = _ada(c, w_ada, b_ada)
    shift1, scale1, gate1, shift2, scale2, gate2 = [m.reshape(bsz, 1, D) for m in jnp.split(mod, 6, axis=-1)]

    c_gk = 2 * KEY_DIM + 2 * VAL_DIM
    c_p = c_gk + GATE_RANK
    c_ga = c_p + POOL_DIM
    w_main = jnp.concatenate([w_in[:, :c_gk], w_in[:, c_ga:], w_in[:, c_p:c_ga]], axis=1).astype(BF16)
    w_gk = jnp.zeros((D, LANES), F32).at[:, :GATE_RANK].set(w_in[:, c_gk:c_p]).astype(BF16)

    z, gk, wg16, wu16, wd16 = _inproj(h2d, norm_mix.reshape(1, D), scale1, shift1, w_main, w_gk,
                                      w_gate, w_up, w_down, seq)
    o_gla = _gla(z, gk, w_gk2, b_gk2, gla_norm, bsz, seq)
    cb = _pool(z, w_pool, pool_scale, w_b, seq)

    w_router = jnp.zeros((D, LANES), F32).at[:, :N_GROUPS].set(w_rg)
    w_router = w_router.at[:, N_GROUPS:N_GROUPS + N_EXPERTS].set(w_re).astype(BF16)
    h1, u2, logits = _mix(o_gla, z, cb, h2d, gate1, norm_ffn.reshape(1, D), scale2, shift2,
                          w_a.astype(BF16), w_out.astype(BF16), w_router, seq)

    bias = jnp.zeros((1, LANES), F32).at[0, :N_GROUPS].set(b_rg).at[0, N_GROUPS:N_GROUPS + N_EXPERTS].set(b_re)
    pos, wts, cnt = _route(logits, bias)
    pos2 = pos[:, :2]

    counts = cnt[0, :N_EXPERTS].astype(I32)
    nblk = (counts + ROW_BLOCK - 1) // ROW_BLOCK
    bends = jnp.cumsum(nblk)
    n_rows = 2 * t + N_EXPERTS * ROW_BLOCK
    n_blocks = n_rows // ROW_BLOCK
    blk = jnp.arange(n_blocks, dtype=I32)
    blk_e = jnp.sum(bends[None, :] <= blk[:, None], axis=1)
    blk_e = jnp.minimum(blk_e, N_EXPERTS - 1).astype(I32)
    blk_rows = jnp.clip(counts[blk_e] - (blk - (bends - nblk)[blk_e]) * ROW_BLOCK, 0, ROW_BLOCK)
    blk_rows = jnp.where(blk < bends[-1], blk_rows, 0).astype(I32)
    slot_tok = jnp.zeros((n_rows,), I32).at[pos2.reshape(-1)].set(
        jnp.arange(2 * t, dtype=I32) // 2, unique_indices=True)

    yb = _experts(u2, slot_tok, blk_e, blk_rows, wg16, wu16, wd16)
    return pos2, yb, wts, h1, gate2


def kernel(x, c, w_ada, b_ada, norm_mix, w_in, w_gk2, b_gk2, gla_norm, w_a, w_pool, pool_scale, w_b, w_out, norm_ffn, w_rg, b_rg, w_re, b_re, w_gate, w_up, w_down, norm_final):
    bsz, seq, _ = x.shape
    depth = w_ada.shape[0]
    assert depth == 1
    h2d = x.reshape(bsz * seq, D)
    pos2, yb, wts, h1, gate2 = _layer(
        h2d, c, bsz, seq, w_ada[0], b_ada[0], norm_mix[0], w_in[0], w_gk2[0], b_gk2[0], gla_norm[0], w_a[0],
        w_pool[0], pool_scale[0], w_b[0], w_out[0], norm_ffn[0], w_rg[0], b_rg[0], w_re[0], b_re[0],
        w_gate[0], w_up[0], w_down[0])
    out = _combine(pos2, yb, wts, h1, gate2, norm_final.reshape(1, D), seq)
    return out.reshape(bsz, seq, D)
```

```python
import functools

import jax
import jax.numpy as jnp
from jax import lax
from jax.experimental import pallas as pl
from jax.experimental.pallas import tpu as pltpu

F32 = jnp.float32
BF16 = jnp.bfloat16
I32 = jnp.int32

D = 2048
HEADS = 4
DK = 256
DV = 512
KEY_DIM = HEADS * DK
VAL_DIM = HEADS * DV
GATE_RANK = 16
GATE_NORMALIZER = 16.0
CHUNK = 64
POOL_WINDOWS = (2, 4, 8, 16)
POOL_DIM = 1024
POOL_GROUP_DIM = 256
POOL_HALO = 16
N_GROUPS = 8
EXPERTS_PER_GROUP = 8
N_EXPERTS = 64
EXPERT_HIDDEN = 1024
EPS = 1e-6
LANES = 128
ROW_SUB = D // LANES

Z_COLS = 2 * KEY_DIM + 2 * VAL_DIM + 2 * D + POOL_DIM

VMEM_LIMIT = 56 * 1024 * 1024
VMEM_LIMIT_INPROJ = 60 * 1024 * 1024
CAST_STEPS = 8

TM_IN = 1024
TN_IN = 1024
TC_GLA = 512
TM_POOL = 512
TM_MIX = 256
TM_ROUTE = 512
TM_ROWS = 512
ROW_BLOCK = 512
SUB_ROWS = 128


def _sigmoid(x):
    return 1.0 / (1.0 + jnp.exp(-x))


def _params(sem, vmem=VMEM_LIMIT):
    return pltpu.CompilerParams(dimension_semantics=sem, vmem_limit_bytes=vmem)


def _ada_body(c_ref, w_ref, b_ref, o_ref):
    c = c_ref[...]
    s = (c * _sigmoid(c)).astype(BF16)
    o_ref[...] = jnp.dot(s, w_ref[...].astype(BF16), preferred_element_type=F32) + b_ref[...]


def _ada(c, w, b):
    bsz = c.shape[0]
    cp = jnp.zeros((8, D), F32).at[:bsz].set(c)
    n = w.shape[1]
    tn = 1024
    out = pl.pallas_call(
        _ada_body,
        grid=(n // tn,),
        in_specs=[pl.BlockSpec((8, D), lambda j: (0, 0)),
                  pl.BlockSpec((D, tn), lambda j: (0, j)),
                  pl.BlockSpec((1, tn), lambda j: (0, j))],
        out_specs=pl.BlockSpec((8, tn), lambda j: (0, j)),
        out_shape=jax.ShapeDtypeStruct((8, n), F32),
        compiler_params=_params(("arbitrary",)),
        name="ada",
    )(cp, w, b.reshape(1, n))
    return out[:bsz]


def _inproj_body(x_ref, g_ref, sc_ref, sh_ref, w_ref, wgk_ref, eg_ref, eu_ref, ed_ref,
                 z_ref, gk_ref, og_ref, ou_ref, od_ref, u_scr):
    @pl.when(pl.program_id(1) < CAST_STEPS)
    def _():
        og_ref[...] = eg_ref[...].astype(BF16)
        ou_ref[...] = eu_ref[...].astype(BF16)
        od_ref[...] = ed_ref[...].astype(BF16)

    @pl.when(pl.program_id(1) == 0)
    def _():
        x = x_ref[...]
        ms = jnp.mean(x * x, axis=-1, keepdims=True)
        u = x * lax.rsqrt(ms + EPS) * g_ref[...]
        u = (u * (1.0 + sc_ref[0]) + sh_ref[0]).astype(BF16)
        u_scr[...] = u
        gk_ref[...] = jnp.dot(u, wgk_ref[...], preferred_element_type=F32)

    z_ref[...] = jnp.dot(u_scr[...], w_ref[...], preferred_element_type=F32).astype(BF16)


def _inproj(x2, gain, scale, shift, w_main, w_gk, w_gate, w_up, w_down, seq):
    t = x2.shape[0]
    tm = min(TM_IN, seq)
    per_seq = seq // tm
    n_i = t // tm
    n_j = Z_COLS // TN_IN
    assert CAST_STEPS <= n_j
    n_slabs = n_i * CAST_STEPS
    eg = w_gate.reshape(N_EXPERTS * D, EXPERT_HIDDEN)
    eu = w_up.reshape(N_EXPERTS * D, EXPERT_HIDDEN)
    ed = w_down.reshape(N_EXPERTS * EXPERT_HIDDEN, D)
    rg, rd = eg.shape[0] // n_slabs, ed.shape[0] // n_slabs
    assert rg * n_slabs == eg.shape[0] and rd * n_slabs == ed.shape[0] and rd % 16 == 0
    slab = lambda i, j: (i * CAST_STEPS + jnp.minimum(j, CAST_STEPS - 1), 0)
    z, gk, og, ou, od = pl.pallas_call(
        _inproj_body,
        grid=(n_i, n_j),
        in_specs=[pl.BlockSpec((tm, D), lambda i, j: (i, 0)),
                  pl.BlockSpec((1, D), lambda i, j: (0, 0)),
                  pl.BlockSpec((1, 1, D), lambda i, j: (i // per_seq, 0, 0)),
                  pl.BlockSpec((1, 1, D), lambda i, j: (i // per_seq, 0, 0)),
                  pl.BlockSpec((D, TN_IN), lambda i, j: (0, j)),
                  pl.BlockSpec((D, LANES), lambda i, j: (0, 0)),
                  pl.BlockSpec((rg, EXPERT_HIDDEN), slab),
                  pl.BlockSpec((rg, EXPERT_HIDDEN), slab),
                  pl.BlockSpec((rd, D), slab)],
        out_specs=[pl.BlockSpec((tm, TN_IN), lambda i, j: (i, j)),
                   pl.BlockSpec((tm, LANES), lambda i, j: (i, 0)),
                   pl.BlockSpec((rg, EXPERT_HIDDEN), slab),
                   pl.BlockSpec((rg, EXPERT_HIDDEN), slab),
                   pl.BlockSpec((rd, D), slab)],
        out_shape=[jax.ShapeDtypeStruct((t, Z_COLS), BF16),
                   jax.ShapeDtypeStruct((t, LANES), F32),
                   jax.ShapeDtypeStruct(eg.shape, BF16),
                   jax.ShapeDtypeStruct(eu.shape, BF16),
                   jax.ShapeDtypeStruct(ed.shape, BF16)],
        scratch_shapes=[pltpu.VMEM((tm, D), BF16)],
        compiler_params=_params(("arbitrary", "arbitrary"), vmem=VMEM_LIMIT_INPROJ),
        name="inproj",
    )(x2, gain, scale, shift, w_main, w_gk, eg, eu, ed)
    return (z, gk, og.reshape(w_gate.shape), ou.reshape(w_up.shape), od.reshape(w_down.shape))


def _gla_body(q_ref, k_ref, v_ref, g_ref, gk_ref, wgk_ref, bgk_ref, gn_ref, ll_ref, o_ref, st_ref, *, n_chunks):
    @pl.when(pl.program_id(1) == 0)
    def _():
        st_ref[...] = jnp.zeros_like(st_ref)

    row = lax.broadcasted_iota(I32, (CHUNK, CHUNK), 0)
    col = lax.broadcasted_iota(I32, (CHUNK, CHUNK), 1)
    causal = row >= col
    nt = (((1,), (1,)), ((), ()))
    tn = (((0,), (0,)), ((), ()))

    def chunk(ci, carry):
        r0 = pl.multiple_of(ci * CHUNK, CHUNK)
        rows = pl.ds(r0, CHUNK)
        gk = gk_ref[rows, :].astype(BF16)
        xg = jnp.dot(gk, wgk_ref[...], preferred_element_type=F32) + bgk_ref[...]
        la = (jnp.minimum(xg, 0.0) - jnp.log(1.0 + jnp.exp(-jnp.abs(xg)))) * (1.0 / GATE_NORMALIZER)
        la_hi = la.astype(BF16)
        la_lo = (la - la_hi.astype(F32)).astype(BF16)
        ll = ll_ref[...]
        bb = (jnp.dot(ll, la_hi, preferred_element_type=F32)
              + jnp.dot(ll, la_lo, preferred_element_type=F32))
        b = bb[:CHUNK]
        bl = bb[CHUNK:]
        q = q_ref[rows, :].astype(F32)
        k = k_ref[rows, :].astype(F32)
        qe = (q * (DK ** -0.5) * jnp.exp(b)).astype(BF16)
        ke = (k * jnp.exp(-b)).astype(BF16)
        kd = (k * jnp.exp(bl - b)).astype(BF16)
        dec = jnp.exp(bl[0:1, :])
        for h in range(HEADS):
            sk = slice(h * DK, (h + 1) * DK)
            sv = slice(h * DV, (h + 1) * DV)
            att = lax.dot_general(qe[:, sk], ke[:, sk], nt, preferred_element_type=F32)
            att = jnp.where(causal, att, 0.0).astype(BF16)
            vc = v_ref[rows, sv]
            st = st_ref[h]
            o = (jnp.dot(att, vc, preferred_element_type=F32)
                 + lax.dot_general(qe[:, sk], st.astype(BF16), nt, preferred_element_type=F32))
            st_ref[h] = st * dec[:, sk] + lax.dot_general(vc, kd[:, sk], tn, preferred_element_type=F32)
            ms = jnp.mean(o * o, axis=-1, keepdims=True)
            on = o * lax.rsqrt(ms + EPS) * gn_ref[...]
            gg = g_ref[rows, sv].astype(F32)
            o_ref[rows, sv] = (on * (gg * _sigmoid(gg))).astype(BF16)
        return carry

    lax.fori_loop(0, n_chunks, chunk, 0, unroll=4)


def _gla(z, gk, w_gk2, b_gk2, gla_norm, bsz, seq):
    t = z.shape[0]
    tc = min(TC_GLA, seq)
    per_seq = seq // tc
    wgk = jnp.zeros((LANES, KEY_DIM), F32).at[:GATE_RANK].set(w_gk2).astype(BF16)
    r = jnp.arange(2 * CHUNK)[:, None]
    c = jnp.arange(CHUNK)[None, :]
    ll = jnp.where((r >= CHUNK) | (r >= c), 1.0, 0.0).astype(BF16)
    kb = KEY_DIM // KEY_DIM
    return pl.pallas_call(
        functools.partial(_gla_body, n_chunks=tc // CHUNK),
        grid=(bsz, per_seq),
        in_specs=[pl.BlockSpec((tc, KEY_DIM), lambda b, s: (b * per_seq + s, 0)),
                  pl.BlockSpec((tc, KEY_DIM), lambda b, s: (b * per_seq + s, kb)),
                  pl.BlockSpec((tc, VAL_DIM), lambda b, s: (b * per_seq + s, 1)),
                  pl.BlockSpec((tc, VAL_DIM), lambda b, s: (b * per_seq + s, 2)),
                  pl.BlockSpec((tc, LANES), lambda b, s: (b * per_seq + s, 0)),
                  pl.BlockSpec((LANES, KEY_DIM), lambda b, s: (0, 0)),
                  pl.BlockSpec((1, KEY_DIM), lambda b, s: (0, 0)),
                  pl.BlockSpec((1, DV), lambda b, s: (0, 0)),
                  pl.BlockSpec((2 * CHUNK, CHUNK), lambda b, s: (0, 0))],
        out_specs=pl.BlockSpec((tc, VAL_DIM), lambda b, s: (b * per_seq + s, 0)),
        out_shape=jax.ShapeDtypeStruct((t, VAL_DIM), BF16),
        scratch_shapes=[pltpu.VMEM((HEADS, DV, DK), F32)],
        compiler_params=_params(("arbitrary", "arbitrary")),
        name="gla",
    )(z, z, z, z, gk, wgk, b_gk2.reshape(1, KEY_DIM), gla_norm.reshape(1, DV), ll)


def _pool_body(p_ref, halo_ref, gb_ref, wp_ref, ps_ref, wb_ref, o_ref, *, tm, seq):
    base = (pl.program_id(0) * tm) % seq
    pf = p_ref[...].astype(F32)
    hal = halo_ref[...].astype(F32)
    hal = jnp.where(base == 0, 0.0, hal)
    ext = jnp.concatenate([hal, pf], axis=0)
    pos = base + lax.broadcasted_iota(I32, (tm, 1), 0)
    ys = []
    for gi, w in enumerate(POOL_WINDOWS):
        cols = slice(gi * POOL_GROUP_DIM, (gi + 1) * POOL_GROUP_DIM)
        s = ext[:, cols]
        sh = 1
        while sh < w:
            s = s + pltpu.roll(s, sh, 0)
            sh *= 2
        cnt = jnp.minimum(pos + 1, w).astype(F32)
        m = s[POOL_HALO:] / cnt - pf[:, cols]
        ys.append(jnp.dot(m.astype(BF16), wp_ref[gi], preferred_element_type=F32))
    y = jnp.concatenate(ys, axis=1) * ps_ref[...]
    yb = jnp.dot(y.astype(BF16), wb_ref[...], preferred_element_type=F32)
    o_ref[...] = (yb * _sigmoid(gb_ref[...].astype(F32))).astype(BF16)


def _pool(z, w_pool, pool_scale, w_b, seq):
    t = z.shape[0]
    tm = min(TM_POOL, seq)
    p_blk = (Z_COLS - POOL_DIM) // POOL_DIM
    gb_blk = (2 * KEY_DIM + 2 * VAL_DIM + D) // D
    hb = tm // POOL_HALO
    return pl.pallas_call(
        functools.partial(_pool_body, tm=tm, seq=seq),
        grid=(t // tm,),
        in_specs=[pl.BlockSpec((tm, POOL_DIM), lambda i: (i, p_blk)),
                  pl.BlockSpec((POOL_HALO, POOL_DIM), lambda i: (jnp.maximum(i * hb - 1, 0), p_blk)),
                  pl.BlockSpec((tm, D), lambda i: (i, gb_blk)),
                  pl.BlockSpec((len(POOL_WINDOWS), POOL_GROUP_DIM, POOL_GROUP_DIM), lambda i: (0, 0, 0)),
                  pl.BlockSpec((1, POOL_DIM), lambda i: (0, 0)),
                  pl.BlockSpec((POOL_DIM, D), lambda i: (0, 0))],
        out_specs=pl.BlockSpec((tm, D), lambda i: (i, 0)),
        out_shape=jax.ShapeDtypeStruct((t, D), BF16),
        compiler_params=_params(("arbitrary",)),
        name="pool",
    )(z, z, z, w_pool.astype(BF16), pool_scale.reshape(1, POOL_DIM), w_b.astype(BF16))


def _mix_body(o_ref, ga_ref, cb_ref, x_ref, g1_ref, nf_ref, sc_ref, sh_ref, wa_ref, wo_ref, wr_ref,
              h_ref, u_ref, lg_ref):
    ya = jnp.dot(o_ref[...], wa_ref[...], preferred_element_type=F32)
    merged = ya * _sigmoid(ga_ref[...].astype(F32)) + cb_ref[...].astype(F32)
    mo = jnp.dot(merged.astype(BF16), wo_ref[...], preferred_element_type=F32)
    h = x_ref[...] + g1_ref[0] * mo
    h_ref[...] = h
    ms = jnp.mean(h * h, axis=-1, keepdims=True)
    u = h * lax.rsqrt(ms + EPS) * nf_ref[...]
    u = u * (1.0 + sc_ref[0]) + sh_ref[0]
    ub = u.astype(BF16)
    u_ref[...] = ub.reshape(u_ref.shape)
    lg_ref[...] = jnp.dot(ub, wr_ref[...], preferred_element_type=F32)


def _mix(o_gla, z, cb, x2, gate1, norm_ffn, scale2, shift2, w_a, w_out, w_router, seq):
    t = x2.shape[0]
    tm = min(TM_MIX, seq)
    per_seq = seq // tm
    ga_blk = (2 * KEY_DIM + 2 * VAL_DIM) // D
    tok = lambda i: (i, 0)
    const = lambda i: (0, 0)
    per_b = lambda i: (i // per_seq, 0, 0)
    return pl.pallas_call(
        _mix_body,
        grid=(t // tm,),
        in_specs=[pl.BlockSpec((tm, VAL_DIM), tok),
                  pl.BlockSpec((tm, D), lambda i: (i, ga_blk)),
                  pl.BlockSpec((tm, D), tok),
                  pl.BlockSpec((tm, D), tok),
                  pl.BlockSpec((1, 1, D), per_b),
                  pl.BlockSpec((1, D), const),
                  pl.BlockSpec((1, 1, D), per_b),
                  pl.BlockSpec((1, 1, D), per_b),
                  pl.BlockSpec((VAL_DIM, D), const, pipeline_mode=pl.Buffered(1)),
                  pl.BlockSpec((D, D), const, pipeline_mode=pl.Buffered(1)),
                  pl.BlockSpec((D, LANES), const)],
        out_specs=[pl.BlockSpec((tm, D), tok),
                   pl.BlockSpec((tm, ROW_SUB, LANES), lambda i: (i, 0, 0)),
                   pl.BlockSpec((tm, LANES), tok)],
        out_shape=[jax.ShapeDtypeStruct((t, D), F32),
                   jax.ShapeDtypeStruct((t, ROW_SUB, LANES), BF16),
                   jax.ShapeDtypeStruct((t, LANES), F32)],
        compiler_params=_params(("arbitrary",)),
        name="mix",
    )(o_gla, z, cb, x2, gate1, norm_ffn, scale2, shift2, w_a, w_out, w_router)


def _route_topk(gl, tm):
    lane = lax.broadcasted_iota(I32, (tm, LANES), 1)
    lanef = lane.astype(F32)
    neg = float("-inf")
    big = float(LANES)
    gmask = lane < N_GROUPS
    gmax = jnp.max(jnp.where(gmask, gl, neg), axis=-1, keepdims=True)
    gsel = jnp.min(jnp.where(gmask & (gl == gmax), lanef, big), axis=-1, keepdims=True)
    gsum = jnp.sum(jnp.where(gmask, jnp.exp(gl - gmax), 0.0), axis=-1, keepdims=True)
    pg = 1.0 / gsum
    lo = N_GROUPS + gsel * EXPERTS_PER_GROUP
    emask = (lanef >= lo) & (lanef < lo + EXPERTS_PER_GROUP)
    emax = jnp.max(jnp.where(emask, gl, neg), axis=-1, keepdims=True)
    ee = jnp.where(emask, jnp.exp(gl - emax), 0.0)
    eprob = ee / jnp.sum(ee, axis=-1, keepdims=True)
    v1 = jnp.max(jnp.where(emask, eprob, -1.0), axis=-1, keepdims=True)
    i1 = jnp.min(jnp.where(emask & (eprob == v1), lanef, big), axis=-1, keepdims=True)
    m2 = emask & (lanef != i1)
    v2 = jnp.max(jnp.where(m2, eprob, -1.0), axis=-1, keepdims=True)
    i2 = jnp.min(jnp.where(m2 & (eprob == v2), lanef, big), axis=-1, keepdims=True)
    tot = v1 + v2
    w1 = pg * (v1 / tot)
    w2 = pg * (v2 / tot)
    hit1 = lanef == i1 - N_GROUPS
    hit2 = lanef == i2 - N_GROUPS
    return hit1, hit2, w1, w2


def _count_body(lg_ref, bias_ref, cnt_ref, *, tm):
    @pl.when(pl.program_id(0) == 0)
    def _():
        cnt_ref[...] = jnp.zeros_like(cnt_ref)

    hit1, hit2, _, _ = _route_topk(lg_ref[...] + bias_ref[...], tm)
    oh = jnp.where(hit1 | hit2, 1.0, 0.0)
    cnt_ref[...] += jnp.sum(oh, axis=0, keepdims=True)


def _assign_body(lg_ref, bias_ref, cnt_ref, ls_ref, ut_ref, pos_ref, wt_ref, carry_ref, pst_ref, *, tm):
    lane = lax.broadcasted_iota(I32, (tm, LANES), 1)

    @pl.when(pl.program_id(0) == 0)
    def _():
        cnt = cnt_ref[0:1, :]
        nblk = jnp.floor((cnt + (ROW_BLOCK - 1)) * (1.0 / ROW_BLOCK))
        hi = jnp.floor(nblk * (1.0 / 256.0))
        lo = nblk - 256.0 * hi
        ut = ut_ref[...]
        hi8 = jnp.broadcast_to(hi, (8, LANES)).astype(BF16)
        lo8 = jnp.broadcast_to(lo, (8, LANES)).astype(BF16)
        pre = (256.0 * jnp.dot(hi8, ut, preferred_element_type=F32)
               + jnp.dot(lo8, ut, preferred_element_type=F32))
        pst_ref[...] = pre[0:1] * float(ROW_BLOCK)
        carry_ref[...] = jnp.zeros_like(carry_ref)

    hit1, hit2, w1, w2 = _route_topk(lg_ref[...] + bias_ref[...], tm)
    oh = jnp.where(hit1 | hit2, 1.0, 0.0)
    cum = jnp.dot(ls_ref[...], oh.astype(BF16), preferred_element_type=F32)
    slot = cum + carry_ref[...] + pst_ref[...]
    carry_ref[...] += jnp.sum(oh, axis=0, keepdims=True)
    p1 = jnp.sum(jnp.where(hit1, slot, 0.0), axis=-1, keepdims=True)
    p2 = jnp.sum(jnp.where(hit2, slot, 0.0), axis=-1, keepdims=True)
    pos_ref[...] = jnp.where(lane == 0, p1, jnp.where(lane == 1, p2, 0.0)).astype(I32)
    wt_ref[...] = jnp.where(lane == 0, w1, jnp.where(lane == 1, w2, 0.0))


def _route(logits, bias):
    t = logits.shape[0]
    tm = min(TM_ROUTE, t)
    r = jnp.arange(tm)
    ls = jnp.where(r[:, None] > r[None, :], 1.0, 0.0).astype(BF16)
    e = jnp.arange(LANES)
    ut = jnp.where(e[:, None] < e[None, :], 1.0, 0.0).astype(BF16)
    tok = lambda i: (i, 0)
    const = lambda i: (0, 0)
    cnt = pl.pallas_call(
        functools.partial(_count_body, tm=tm),
        grid=(t // tm,),
        in_specs=[pl.BlockSpec((tm, LANES), tok), pl.BlockSpec((1, LANES), const)],
        out_specs=pl.BlockSpec((8, LANES), const),
        out_shape=jax.ShapeDtypeStruct((8, LANES), F32),
        compiler_params=_params(("arbitrary",)),
        name="route_count",
    )(logits, bias)
    pos, wts = pl.pallas_call(
        functools.partial(_assign_body, tm=tm),
        grid=(t // tm,),
        in_specs=[pl.BlockSpec((tm, LANES), tok),
                  pl.BlockSpec((1, LANES), const),
                  pl.BlockSpec((8, LANES), const),
                  pl.BlockSpec((tm, tm), const),
                  pl.BlockSpec((LANES, LANES), const)],
        out_specs=[pl.BlockSpec((tm, LANES), tok), pl.BlockSpec((tm, LANES), tok)],
        out_shape=[jax.ShapeDtypeStruct((t, LANES), I32), jax.ShapeDtypeStruct((t, LANES), F32)],
        scratch_shapes=[pltpu.VMEM((1, LANES), F32), pltpu.VMEM((1, LANES), F32)],
        compiler_params=_params(("arbitrary",)),
        name="route_assign",
    )(logits, bias, cnt, ls, ut)
    return pos, wts, cnt


def _row_copy(src, s, dst, d, sem):
    return pltpu.make_async_copy(src.at[s], dst.at[d], sem)


def _list_copy(src, b, dst, s, sem):
    return pltpu.make_async_copy(src.at[b], dst.at[s], sem.at[s])


def _expert_body(be_ref, nr_ref, tok_ref, u_ref, wg_ref, wu_ref, wd_ref, y_ref, xbuf, tokbuf, sem, tsem, *, n_blocks):
    i = pl.program_id(0)
    slot = i % 2
    nrows = nr_ref[i]
    used = nrows > 0
    nrows_next = nr_ref[jnp.minimum(i + 1, n_blocks - 1)]
    next_used = (i + 1 < n_blocks) & (nrows_next > 0)
    next2_used = (i + 2 < n_blocks) & (nr_ref[jnp.minimum(i + 2, n_blocks - 1)] > 0)

    def pairs(n):
        return (n + 1) // 2

    def gather(s, n):
        def issue(p, carry):
            for k in range(2):
                r = 2 * p + k
                _row_copy(u_ref, tokbuf[s, 0, r], xbuf.at[s], r, sem.at[s]).start(priority=k)
            return carry
        lax.fori_loop(0, pairs(n), issue, 0)

    @pl.when(i == 0)
    def _():
        xbuf[...] = jnp.zeros_like(xbuf)

        @pl.when(used)
        def _():
            _list_copy(tok_ref, 0, tokbuf, 0, tsem).start()
            _list_copy(tok_ref, 0, tokbuf, 0, tsem).wait()
            gather(0, nrows)

        @pl.when(next_used)
        def _():
            _list_copy(tok_ref, 1, tokbuf, 1, tsem).start()

    @pl.when(used)
    def _():
        @pl.when(next2_used)
        def _():
            _list_copy(tok_ref, i + 2, tokbuf, slot, tsem).start()

        @pl.when(next_used)
        def _():
            _list_copy(tok_ref, i + 1, tokbuf, 1 - slot, tsem).wait()
            gather(1 - slot, nrows_next)

        def drain(p, carry):
            for k in range(2):
                _row_copy(u_ref, 0, xbuf.at[slot], 0, sem.at[slot]).wait()
            return carry
        lax.fori_loop(0, pairs(nrows), drain, 0)

    nsub = (nrows + SUB_ROWS - 1) // SUB_ROWS
    for k in range(1, ROW_BLOCK // SUB_ROWS + 1):
        @pl.when(nsub == k)
        def _(k=k):
            m = k * SUB_ROWS
            x = xbuf[slot, 0:m].reshape(m, D)
            a = jnp.dot(x, wg_ref[0], preferred_element_type=F32)
            b = jnp.dot(x, wu_ref[0], preferred_element_type=F32)
            hdn = (a * _sigmoid(a) * b).astype(BF16)
            y = jnp.dot(hdn, wd_ref[0], preferred_element_type=F32)
            y_ref[0:m] = y.astype(BF16).reshape(m, ROW_SUB, LANES)
            if m < ROW_BLOCK:
                y_ref[m:] = jnp.zeros((ROW_BLOCK - m, ROW_SUB, LANES), BF16)

    @pl.when(jnp.logical_not(used))
    def _():
        y_ref[...] = jnp.zeros_like(y_ref)


def _experts(u3, slot_tok, blk_e, blk_rows, w_gate, w_up, w_down):
    n_rows = slot_tok.shape[0]
    nb = n_rows // ROW_BLOCK
    rows = lambda i, be, nr: (i, 0, 0)
    wsel = lambda i, be, nr: (be[i], 0, 0)
    grid_spec = pltpu.PrefetchScalarGridSpec(
        num_scalar_prefetch=2,
        grid=(nb,),
        in_specs=[pl.BlockSpec(memory_space=pl.ANY),
                  pl.BlockSpec(memory_space=pl.ANY),
                  pl.BlockSpec((1, D, EXPERT_HIDDEN), wsel),
                  pl.BlockSpec((1, D, EXPERT_HIDDEN), wsel),
                  pl.BlockSpec((1, EXPERT_HIDDEN, D), wsel)],
        out_specs=pl.BlockSpec((ROW_BLOCK, ROW_SUB, LANES), rows),
        scratch_shapes=[pltpu.VMEM((2, ROW_BLOCK, ROW_SUB, LANES), BF16),
                        pltpu.SMEM((2, 1, ROW_BLOCK), I32),
                        pltpu.SemaphoreType.DMA((2,)),
                        pltpu.SemaphoreType.DMA((2,))],
    )
    return pl.pallas_call(
        functools.partial(_expert_body, n_blocks=nb),
        grid_spec=grid_spec,
        out_shape=jax.ShapeDtypeStruct((n_rows, ROW_SUB, LANES), BF16),
        compiler_params=_params(("arbitrary",)),
        name="experts",
    )(blk_e, blk_rows, slot_tok.reshape(nb, 1, ROW_BLOCK), u3, w_gate, w_up, w_down)


def _combine_body(cur_ref, nxt_ref, yb_ref, wt_ref, h_ref, g2_ref, nf_ref, o_ref, gbuf, sem, *, tm, n_steps):
    i = pl.program_id(0)
    slot = i % 2

    def gather(pos_ref, s):
        def issue(r, carry):
            for k in range(2):
                _row_copy(yb_ref, pos_ref[0, 0, k * tm + r], gbuf.at[s, k], r, sem.at[s]).start(priority=k)
            return carry
        lax.fori_loop(0, tm, issue, 0, unroll=4)

    @pl.when(i == 0)
    def _():
        gather(cur_ref, 0)

    @pl.when(i + 1 < n_steps)
    def _():
        gather(nxt_ref, 1 - slot)

    def drain(r, carry):
        for k in range(2):
            _row_copy(yb_ref, 0, gbuf.at[slot, k], 0, sem.at[slot]).wait()
        return carry
    lax.fori_loop(0, tm, drain, 0, unroll=4)

    wt = wt_ref[...]
    ya = gbuf[slot, 0].reshape(tm, D).astype(F32)
    yb = gbuf[slot, 1].reshape(tm, D).astype(F32)
    y = wt[:, 0:1] * ya + wt[:, 1:2] * yb
    h = h_ref[...] + g2_ref[0] * y
    ms = jnp.mean(h * h, axis=-1, keepdims=True)
    o_ref[...] = h * lax.rsqrt(ms + EPS) * nf_ref[...]


def _combine(pos, yb, wts, h1, gate2, norm_final, seq):
    t = h1.shape[0]
    tm = min(TM_ROWS, seq)
    per_seq = seq // tm
    tok = lambda i: (i, 0)
    n_steps = t // tm
    pos3 = jnp.concatenate([pos[:, 0].reshape(n_steps, 1, tm), pos[:, 1].reshape(n_steps, 1, tm)], axis=2)
    return pl.pallas_call(
        functools.partial(_combine_body, tm=tm, n_steps=n_steps),
        grid=(n_steps,),
        in_specs=[pl.BlockSpec((1, 1, 2 * tm), lambda i: (i, 0, 0), memory_space=pltpu.SMEM),
                  pl.BlockSpec((1, 1, 2 * tm), lambda i: (jnp.minimum(i + 1, n_steps - 1), 0, 0),
                               memory_space=pltpu.SMEM),
                  pl.BlockSpec(memory_space=pl.ANY),
                  pl.BlockSpec((tm, LANES), tok),
                  pl.BlockSpec((tm, D), tok),
                  pl.BlockSpec((1, 1, D), lambda i: (i // per_seq, 0, 0)),
                  pl.BlockSpec((1, D), lambda i: (0, 0))],
        out_specs=pl.BlockSpec((tm, D), tok),
        out_shape=jax.ShapeDtypeStruct((t, D), F32),
        scratch_shapes=[pltpu.VMEM((2, 2, tm, ROW_SUB, LANES), BF16), pltpu.SemaphoreType.DMA((2,))],
        compiler_params=_params(("arbitrary",)),
        name="combine",
    )(pos3, pos3, yb, wts, h1, gate2, norm_final)


def _layer(h2d, c, bsz, seq, w_ada, b_ada, norm_mix, w_in, w_gk2, b_gk2, gla_norm, w_a, w_pool, pool_scale, w_b,
           w_out, norm_ffn, w_rg, b_rg, w_re, b_re, w_gate, w_up, w_down):
    t = bsz * seq
    mod = _ada(c, w_ada, b_ada)
    shift1, scale1, gate1, shift2, scale2, gate2 = [m.reshape(bsz, 1, D) for m in jnp.split(mod, 6, axis=-1)]

    c_gk = 2 * KEY_DIM + 2 * VAL_DIM
    c_p = c_gk + GATE_RANK
    c_ga = c_p + POOL_DIM
    w_main = jnp.concatenate([w_in[:, :c_gk], w_in[:, c_ga:], w_in[:, c_p:c_ga]], axis=1).astype(BF16)
    w_gk = jnp.zeros((D, LANES), F32).at[:, :GATE_RANK].set(w_in[:, c_gk:c_p]).astype(BF16)

    z, gk, wg16, wu16, wd16 = _inproj(h2d, norm_mix.reshape(1, D), scale1, shift1, w_main, w_gk,
                                      w_gate, w_up, w_down, seq)
    o_gla = _gla(z, gk, w_gk2, b_gk2, gla_norm, bsz, seq)
    cb = _pool(z, w_pool, pool_scale, w_b, seq)

    w_router = jnp.zeros((D, LANES), F32).at[:, :N_GROUPS].set(w_rg)
    w_router = w_router.at[:, N_GROUPS:N_GROUPS + N_EXPERTS].set(w_re).astype(BF16)
    h1, u2, logits = _mix(o_gla, z, cb, h2d, gate1, norm_ffn.reshape(1, D), scale2, shift2,
                          w_a.astype(BF16), w_out.astype(BF16), w_router, seq)

    bias = jnp.zeros((1, LANES), F32).at[0, :N_GROUPS].set(b_rg).at[0, N_GROUPS:N_GROUPS + N_EXPERTS].set(b_re)
    pos, wts, cnt = _route(logits, bias)
    counts = cnt[0, :N_EXPERTS].astype(I32)
    nblk = (counts + ROW_BLOCK - 1) // ROW_BLOCK
    bends = jnp.cumsum(nblk)
    n_rows = 2 * t + N_EXPERTS * ROW_BLOCK
    n_blocks = n_rows // ROW_BLOCK
    blk = jnp.arange(n_blocks, dtype=I32)
    blk_e = jnp.sum(bends[None, :] <= blk[:, None], axis=1)
    blk_e = jnp.minimum(blk_e, N_EXPERTS - 1).astype(I32)
    blk_rows = jnp.clip(counts[blk_e] - (blk - (bends - nblk)[blk_e]) * ROW_BLOCK, 0, ROW_BLOCK)
    blk_rows = jnp.where(blk < bends[-1], blk_rows, 0).astype(I32)
    pstart = (bends - nblk) * ROW_BLOCK
    n_fill = nblk * ROW_BLOCK - counts
    m = jnp.arange(ROW_BLOCK, dtype=I32)[None, :]
    e_col = jnp.arange(N_EXPERTS, dtype=I32)[:, None]
    fill_key = jnp.where(m < n_fill[:, None], (pstart + counts)[:, None] + m, n_rows + e_col * ROW_BLOCK + m)
    tok = jnp.arange(t, dtype=I32)
    keys = jnp.concatenate([pos[:, 0], pos[:, 1], fill_key.reshape(-1)])
    vals = jnp.concatenate([tok, tok, jnp.zeros((N_EXPERTS * ROW_BLOCK,), I32)])
    slot_tok = lax.sort((keys, vals), num_keys=1)[1]

    yb = _experts(u2, slot_tok, blk_e, blk_rows, wg16, wu16, wd16)
    return pos, yb, wts, h1, gate2


def kernel(x, c, w_ada, b_ada, norm_mix, w_in, w_gk2, b_gk2, gla_norm, w_a, w_pool, pool_scale, w_b, w_out, norm_ffn, w_rg, b_rg, w_re, b_re, w_gate, w_up, w_down, norm_final):
    bsz, seq, _ = x.shape
    depth = w_ada.shape[0]
    assert depth == 1
    h2d = x.reshape(bsz * seq, D)
    pos, yb, wts, h1, gate2 = _layer(
        h2d, c, bsz, seq, w_ada[0], b_ada[0], norm_mix[0], w_in[0], w_gk2[0], b_gk2[0], gla_norm[0], w_a[0],
        w_pool[0], pool_scale[0], w_b[0], w_out[0], norm_ffn[0], w_rg[0], b_rg[0], w_re[0], b_re[0],
        w_gate[0], w_up[0], w_down[0])
    out = _combine(pos, yb, wts, h1, gate2, norm_final.reshape(1, D), seq)
    return out.reshape(bsz, seq, D)
```

```python
import functools

import jax
import jax.numpy as jnp
from jax import lax
from jax.experimental import pallas as pl
from jax.experimental.pallas import tpu as pltpu

F32 = jnp.float32
BF16 = jnp.bfloat16
I32 = jnp.int32

D = 2048
HEADS = 4
DK = 256
DV = 512
KEY_DIM = HEADS * DK
VAL_DIM = HEADS * DV
GATE_RANK = 16
GATE_NORMALIZER = 16.0
CHUNK = 64
POOL_WINDOWS = (2, 4, 8, 16)
POOL_DIM = 1024
POOL_GROUP_DIM = 256
POOL_HALO = 16
N_GROUPS = 8
EXPERTS_PER_GROUP = 8
N_EXPERTS = 64
EXPERT_HIDDEN = 1024
EPS = 1e-6
LANES = 128
ROW_SUB = D // LANES

Z_COLS = 2 * KEY_DIM + 2 * VAL_DIM + 2 * D + POOL_DIM

VMEM_LIMIT = 56 * 1024 * 1024
VMEM_LIMIT_INPROJ = 60 * 1024 * 1024
CAST_STEPS = 8

TM_IN = 1024
TN_IN = 1024
TC_GLA = 512
TM_POOL = 512
TM_MIX = 256
TM_ROUTE = 1024
TM_ROWS = 256
ROW_BLOCK = 512
SUB_ROWS = 128


def _sigmoid(x):
    return 1.0 / (1.0 + jnp.exp(-x))


def _params(sem, vmem=VMEM_LIMIT):
    return pltpu.CompilerParams(dimension_semantics=sem, vmem_limit_bytes=vmem)


def _ada_body(c_ref, w_ref, b_ref, o_ref):
    c = c_ref[...]
    s = (c * _sigmoid(c)).astype(BF16)
    o_ref[...] = jnp.dot(s, w_ref[...].astype(BF16), preferred_element_type=F32) + b_ref[...]


def _ada(c, w, b):
    bsz = c.shape[0]
    cp = jnp.zeros((8, D), F32).at[:bsz].set(c)
    n = w.shape[1]
    tn = 1024
    out = pl.pallas_call(
        _ada_body,
        grid=(n // tn,),
        in_specs=[pl.BlockSpec((8, D), lambda j: (0, 0)),
                  pl.BlockSpec((D, tn), lambda j: (0, j)),
                  pl.BlockSpec((1, tn), lambda j: (0, j))],
        out_specs=pl.BlockSpec((8, tn), lambda j: (0, j)),
        out_shape=jax.ShapeDtypeStruct((8, n), F32),
        compiler_params=_params(("arbitrary",)),
        name="ada",
    )(cp, w, b.reshape(1, n))
    return out[:bsz]


def _inproj_body(x_ref, g_ref, sc_ref, sh_ref, w_ref, wgk_ref, eg_ref, eu_ref, ed_ref,
                 z_ref, gk_ref, og_ref, ou_ref, od_ref, u_scr):
    @pl.when(pl.program_id(1) < CAST_STEPS)
    def _():
        og_ref[...] = eg_ref[...].astype(BF16)
        ou_ref[...] = eu_ref[...].astype(BF16)
        od_ref[...] = ed_ref[...].astype(BF16)

    @pl.when(pl.program_id(1) == 0)
    def _():
        x = x_ref[...]
        ms = jnp.mean(x * x, axis=-1, keepdims=True)
        u = x * lax.rsqrt(ms + EPS) * g_ref[...]
        u = (u * (1.0 + sc_ref[0]) + sh_ref[0]).astype(BF16)
        u_scr[...] = u
        gk_ref[...] = jnp.dot(u, wgk_ref[...], preferred_element_type=F32)

    z_ref[...] = jnp.dot(u_scr[...], w_ref[...], preferred_element_type=F32).astype(BF16)


def _inproj(x2, gain, scale, shift, w_main, w_gk, w_gate, w_up, w_down, seq):
    t = x2.shape[0]
    tm = min(TM_IN, seq)
    per_seq = seq // tm
    n_i = t // tm
    n_j = Z_COLS // TN_IN
    assert CAST_STEPS <= n_j
    n_slabs = n_i * CAST_STEPS
    eg = w_gate.reshape(N_EXPERTS * D, EXPERT_HIDDEN)
    eu = w_up.reshape(N_EXPERTS * D, EXPERT_HIDDEN)
    ed = w_down.reshape(N_EXPERTS * EXPERT_HIDDEN, D)
    rg, rd = eg.shape[0] // n_slabs, ed.shape[0] // n_slabs
    assert rg * n_slabs == eg.shape[0] and rd * n_slabs == ed.shape[0] and rd % 16 == 0
    slab = lambda i, j: (i * CAST_STEPS + jnp.minimum(j, CAST_STEPS - 1), 0)
    z, gk, og, ou, od = pl.pallas_call(
        _inproj_body,
        grid=(n_i, n_j),
        in_specs=[pl.BlockSpec((tm, D), lambda i, j: (i, 0)),
                  pl.BlockSpec((1, D), lambda i, j: (0, 0)),
                  pl.BlockSpec((1, 1, D), lambda i, j: (i // per_seq, 0, 0)),
                  pl.BlockSpec((1, 1, D), lambda i, j: (i // per_seq, 0, 0)),
                  pl.BlockSpec((D, TN_IN), lambda i, j: (0, j)),
                  pl.BlockSpec((D, LANES), lambda i, j: (0, 0)),
                  pl.BlockSpec((rg, EXPERT_HIDDEN), slab),
                  pl.BlockSpec((rg, EXPERT_HIDDEN), slab),
                  pl.BlockSpec((rd, D), slab)],
        out_specs=[pl.BlockSpec((tm, TN_IN), lambda i, j: (i, j)),
                   pl.BlockSpec((tm, LANES), lambda i, j: (i, 0)),
                   pl.BlockSpec((rg, EXPERT_HIDDEN), slab),
                   pl.BlockSpec((rg, EXPERT_HIDDEN), slab),
                   pl.BlockSpec((rd, D), slab)],
        out_shape=[jax.ShapeDtypeStruct((t, Z_COLS), BF16),
                   jax.ShapeDtypeStruct((t, LANES), F32),
                   jax.ShapeDtypeStruct(eg.shape, BF16),
                   jax.ShapeDtypeStruct(eu.shape, BF16),
                   jax.ShapeDtypeStruct(ed.shape, BF16)],
        scratch_shapes=[pltpu.VMEM((tm, D), BF16)],
        compiler_params=_params(("arbitrary", "arbitrary"), vmem=VMEM_LIMIT_INPROJ),
        name="inproj",
    )(x2, gain, scale, shift, w_main, w_gk, eg, eu, ed)
    return (z, gk, og.reshape(w_gate.shape), ou.reshape(w_up.shape), od.reshape(w_down.shape))


def _gla_body(q_ref, k_ref, v_ref, g_ref, gk_ref, wgk_ref, bgk_ref, gn_ref, ll_ref, o_ref, st_ref, *, n_chunks):
    @pl.when(pl.program_id(1) == 0)
    def _():
        st_ref[...] = jnp.zeros_like(st_ref)

    row = lax.broadcasted_iota(I32, (CHUNK, CHUNK), 0)
    col = lax.broadcasted_iota(I32, (CHUNK, CHUNK), 1)
    causal = row >= col
    nt = (((1,), (1,)), ((), ()))
    tn = (((0,), (0,)), ((), ()))

    def chunk(ci, carry):
        r0 = pl.multiple_of(ci * CHUNK, CHUNK)
        rows = pl.ds(r0, CHUNK)
        gk = gk_ref[rows, :].astype(BF16)
        xg = jnp.dot(gk, wgk_ref[...], preferred_element_type=F32) + bgk_ref[...]
        la = (jnp.minimum(xg, 0.0) - jnp.log(1.0 + jnp.exp(-jnp.abs(xg)))) * (1.0 / GATE_NORMALIZER)
        la_hi = la.astype(BF16)
        la_lo = (la - la_hi.astype(F32)).astype(BF16)
        ll = ll_ref[...]
        bb = (jnp.dot(ll, la_hi, preferred_element_type=F32)
              + jnp.dot(ll, la_lo, preferred_element_type=F32))
        b = bb[:CHUNK]
        bl = bb[CHUNK:]
        q = q_ref[rows, :].astype(F32)
        k = k_ref[rows, :].astype(F32)
        qe = (q * (DK ** -0.5) * jnp.exp(b)).astype(BF16)
        ke = (k * jnp.exp(-b)).astype(BF16)
        kd = (k * jnp.exp(bl - b)).astype(BF16)
        dec = jnp.exp(bl[0:1, :])
        for h in range(HEADS):
            sk = slice(h * DK, (h + 1) * DK)
            sv = slice(h * DV, (h + 1) * DV)
            att = lax.dot_general(qe[:, sk], ke[:, sk], nt, preferred_element_type=F32)
            att = jnp.where(causal, att, 0.0).astype(BF16)
            vc = v_ref[rows, sv]
            st = st_ref[h]
            o = (jnp.dot(att, vc, preferred_element_type=F32)
                 + lax.dot_general(qe[:, sk], st.astype(BF16), nt, preferred_element_type=F32))
            st_ref[h] = st * dec[:, sk] + lax.dot_general(vc, kd[:, sk], tn, preferred_element_type=F32)
            ms = jnp.mean(o * o, axis=-1, keepdims=True)
            on = o * lax.rsqrt(ms + EPS) * gn_ref[...]
            gg = g_ref[rows, sv].astype(F32)
            o_ref[rows, sv] = (on * (gg * _sigmoid(gg))).astype(BF16)
        return carry

    lax.fori_loop(0, n_chunks, chunk, 0, unroll=4)


def _gla(z, gk, w_gk2, b_gk2, gla_norm, bsz, seq):
    t = z.shape[0]
    tc = min(TC_GLA, seq)
    per_seq = seq // tc
    wgk = jnp.zeros((LANES, KEY_DIM), F32).at[:GATE_RANK].set(w_gk2).astype(BF16)
    r = jnp.arange(2 * CHUNK)[:, None]
    c = jnp.arange(CHUNK)[None, :]
    ll = jnp.where((r >= CHUNK) | (r >= c), 1.0, 0.0).astype(BF16)
    kb = KEY_DIM // KEY_DIM
    return pl.pallas_call(
        functools.partial(_gla_body, n_chunks=tc // CHUNK),
        grid=(bsz, per_seq),
        in_specs=[pl.BlockSpec((tc, KEY_DIM), lambda b, s: (b * per_seq + s, 0)),
                  pl.BlockSpec((tc, KEY_DIM), lambda b, s: (b * per_seq + s, kb)),
                  pl.BlockSpec((tc, VAL_DIM), lambda b, s: (b * per_seq + s, 1)),
                  pl.BlockSpec((tc, VAL_DIM), lambda b, s: (b * per_seq + s, 2)),
                  pl.BlockSpec((tc, LANES), lambda b, s: (b * per_seq + s, 0)),
                  pl.BlockSpec((LANES, KEY_DIM), lambda b, s: (0, 0)),
                  pl.BlockSpec((1, KEY_DIM), lambda b, s: (0, 0)),
                  pl.BlockSpec((1, DV), lambda b, s: (0, 0)),
                  pl.BlockSpec((2 * CHUNK, CHUNK), lambda b, s: (0, 0))],
        out_specs=pl.BlockSpec((tc, VAL_DIM), lambda b, s: (b * per_seq + s, 0)),
        out_shape=jax.ShapeDtypeStruct((t, VAL_DIM), BF16),
        scratch_shapes=[pltpu.VMEM((HEADS, DV, DK), F32)],
        compiler_params=_params(("arbitrary", "arbitrary")),
        name="gla",
    )(z, z, z, z, gk, wgk, b_gk2.reshape(1, KEY_DIM), gla_norm.reshape(1, DV), ll)


def _pool_body(p_ref, halo_ref, gb_ref, wp_ref, ps_ref, wb_ref, o_ref, *, tm, seq):
    base = (pl.program_id(0) * tm) % seq
    pf = p_ref[...].astype(F32)
    hal = halo_ref[...].astype(F32)
    hal = jnp.where(base == 0, 0.0, hal)
    ext = jnp.concatenate([hal, pf], axis=0)
    pos = base + lax.broadcasted_iota(I32, (tm, 1), 0)
    ys = []
    for gi, w in enumerate(POOL_WINDOWS):
        cols = slice(gi * POOL_GROUP_DIM, (gi + 1) * POOL_GROUP_DIM)
        s = ext[:, cols]
        sh = 1
        while sh < w:
            s = s + pltpu.roll(s, sh, 0)
            sh *= 2
        cnt = jnp.minimum(pos + 1, w).astype(F32)
        m = s[POOL_HALO:] / cnt - pf[:, cols]
        ys.append(jnp.dot(m.astype(BF16), wp_ref[gi], preferred_element_type=F32))
    y = jnp.concatenate(ys, axis=1) * ps_ref[...]
    yb = jnp.dot(y.astype(BF16), wb_ref[...], preferred_element_type=F32)
    o_ref[...] = (yb * _sigmoid(gb_ref[...].astype(F32))).astype(BF16)


def _pool(z, w_pool, pool_scale, w_b, seq):
    t = z.shape[0]
    tm = min(TM_POOL, seq)
    p_blk = (Z_COLS - POOL_DIM) // POOL_DIM
    gb_blk = (2 * KEY_DIM + 2 * VAL_DIM + D) // D
    hb = tm // POOL_HALO
    return pl.pallas_call(
        functools.partial(_pool_body, tm=tm, seq=seq),
        grid=(t // tm,),
        in_specs=[pl.BlockSpec((tm, POOL_DIM), lambda i: (i, p_blk)),
                  pl.BlockSpec((POOL_HALO, POOL_DIM), lambda i: (jnp.maximum(i * hb - 1, 0), p_blk)),
                  pl.BlockSpec((tm, D), lambda i: (i, gb_blk)),
                  pl.BlockSpec((len(POOL_WINDOWS), POOL_GROUP_DIM, POOL_GROUP_DIM), lambda i: (0, 0, 0)),
                  pl.BlockSpec((1, POOL_DIM), lambda i: (0, 0)),
                  pl.BlockSpec((POOL_DIM, D), lambda i: (0, 0))],
        out_specs=pl.BlockSpec((tm, D), lambda i: (i, 0)),
        out_shape=jax.ShapeDtypeStruct((t, D), BF16),
        compiler_params=_params(("arbitrary",)),
        name="pool",
    )(z, z, z, w_pool.astype(BF16), pool_scale.reshape(1, POOL_DIM), w_b.astype(BF16))


def _mix_body(o_ref, ga_ref, cb_ref, x_ref, g1_ref, nf_ref, sc_ref, sh_ref, wa_ref, wo_ref, wr_ref,
              h_ref, u_ref, lg_ref):
    ya = jnp.dot(o_ref[...], wa_ref[...], preferred_element_type=F32)
    merged = ya * _sigmoid(ga_ref[...].astype(F32)) + cb_ref[...].astype(F32)
    mo = jnp.dot(merged.astype(BF16), wo_ref[...], preferred_element_type=F32)
    h = x_ref[...] + g1_ref[0] * mo
    h_ref[...] = h
    ms = jnp.mean(h * h, axis=-1, keepdims=True)
    u = h * lax.rsqrt(ms + EPS) * nf_ref[...]
    u = u * (1.0 + sc_ref[0]) + sh_ref[0]
    ub = u.astype(BF16)
    u_ref[...] = ub.reshape(u_ref.shape)
    lg_ref[...] = jnp.dot(ub, wr_ref[...], preferred_element_type=F32)


def _mix(o_gla, z, cb, x2, gate1, norm_ffn, scale2, shift2, w_a, w_out, w_router, seq):
    t = x2.shape[0]
    tm = min(TM_MIX, seq)
    per_seq = seq // tm
    ga_blk = (2 * KEY_DIM + 2 * VAL_DIM) // D
    tok = lambda i: (i, 0)
    const = lambda i: (0, 0)
    per_b = lambda i: (i // per_seq, 0, 0)
    return pl.pallas_call(
        _mix_body,
        grid=(t // tm,),
        in_specs=[pl.BlockSpec((tm, VAL_DIM), tok),
                  pl.BlockSpec((tm, D), lambda i: (i, ga_blk)),
                  pl.BlockSpec((tm, D), tok),
                  pl.BlockSpec((tm, D), tok),
                  pl.BlockSpec((1, 1, D), per_b),
                  pl.BlockSpec((1, D), const),
                  pl.BlockSpec((1, 1, D), per_b),
                  pl.BlockSpec((1, 1, D), per_b),
                  pl.BlockSpec((VAL_DIM, D), const, pipeline_mode=pl.Buffered(1)),
                  pl.BlockSpec((D, D), const, pipeline_mode=pl.Buffered(1)),
                  pl.BlockSpec((D, LANES), const)],
        out_specs=[pl.BlockSpec((tm, D), tok),
                   pl.BlockSpec((tm, ROW_SUB, LANES), lambda i: (i, 0, 0)),
                   pl.BlockSpec((tm, LANES), tok)],
        out_shape=[jax.ShapeDtypeStruct((t, D), F32),
                   jax.ShapeDtypeStruct((t, ROW_SUB, LANES), BF16),
                   jax.ShapeDtypeStruct((t, LANES), F32)],
        compiler_params=_params(("arbitrary",)),
        name="mix",
    )(o_gla, z, cb, x2, gate1, norm_ffn, scale2, shift2, w_a, w_out, w_router)


def _route_topk(gl, tm):
    lane = lax.broadcasted_iota(I32, (tm, LANES), 1)
    lanef = lane.astype(F32)
    neg = float("-inf")
    big = float(LANES)
    gmask = lane < N_GROUPS
    gmax = jnp.max(jnp.where(gmask, gl, neg), axis=-1, keepdims=True)
    gsel = jnp.min(jnp.where(gmask & (gl == gmax), lanef, big), axis=-1, keepdims=True)
    gsum = jnp.sum(jnp.where(gmask, jnp.exp(gl - gmax), 0.0), axis=-1, keepdims=True)
    pg = 1.0 / gsum
    lo = N_GROUPS + gsel * EXPERTS_PER_GROUP
    emask = (lanef >= lo) & (lanef < lo + EXPERTS_PER_GROUP)
    emax = jnp.max(jnp.where(emask, gl, neg), axis=-1, keepdims=True)
    ee = jnp.where(emask, jnp.exp(gl - emax), 0.0)
    eprob = ee / jnp.sum(ee, axis=-1, keepdims=True)
    v1 = jnp.max(jnp.where(emask, eprob, -1.0), axis=-1, keepdims=True)
    i1 = jnp.min(jnp.where(emask & (eprob == v1), lanef, big), axis=-1, keepdims=True)
    m2 = emask & (lanef != i1)
    v2 = jnp.max(jnp.where(m2, eprob, -1.0), axis=-1, keepdims=True)
    i2 = jnp.min(jnp.where(m2 & (eprob == v2), lanef, big), axis=-1, keepdims=True)
    tot = v1 + v2
    w1 = pg * (v1 / tot)
    w2 = pg * (v2 / tot)
    hit1 = lanef == i1 - N_GROUPS
    hit2 = lanef == i2 - N_GROUPS
    return hit1, hit2, w1, w2


def _count_body(lg_ref, bias_ref, cnt_ref, *, tm):
    @pl.when(pl.program_id(0) == 0)
    def _():
        cnt_ref[...] = jnp.zeros_like(cnt_ref)

    hit1, hit2, _, _ = _route_topk(lg_ref[...] + bias_ref[...], tm)
    oh = jnp.where(hit1 | hit2, 1.0, 0.0)
    cnt_ref[...] += jnp.sum(oh, axis=0, keepdims=True)


def _assign_body(lg_ref, bias_ref, cnt_ref, ls_ref, ut_ref, pos_ref, wt_ref, carry_ref, pst_ref, *, tm):
    lane = lax.broadcasted_iota(I32, (tm, LANES), 1)

    @pl.when(pl.program_id(0) == 0)
    def _():
        cnt = cnt_ref[0:1, :]
        nblk = jnp.floor((cnt + (ROW_BLOCK - 1)) * (1.0 / ROW_BLOCK))
        hi = jnp.floor(nblk * (1.0 / 256.0))
        lo = nblk - 256.0 * hi
        ut = ut_ref[...]
        hi8 = jnp.broadcast_to(hi, (8, LANES)).astype(BF16)
        lo8 = jnp.broadcast_to(lo, (8, LANES)).astype(BF16)
        pre = (256.0 * jnp.dot(hi8, ut, preferred_element_type=F32)
               + jnp.dot(lo8, ut, preferred_element_type=F32))
        pst_ref[...] = pre[0:1] * float(ROW_BLOCK)
        carry_ref[...] = jnp.zeros_like(carry_ref)

    hit1, hit2, w1, w2 = _route_topk(lg_ref[...] + bias_ref[...], tm)
    oh = jnp.where(hit1 | hit2, 1.0, 0.0)
    cum = jnp.dot(ls_ref[...], oh.astype(BF16), preferred_element_type=F32)
    slot = cum + carry_ref[...] + pst_ref[...]
    carry_ref[...] += jnp.sum(oh, axis=0, keepdims=True)
    p1 = jnp.sum(jnp.where(hit1, slot, 0.0), axis=-1, keepdims=True)
    p2 = jnp.sum(jnp.where(hit2, slot, 0.0), axis=-1, keepdims=True)
    pos_ref[...] = jnp.where(lane == 0, p1, jnp.where(lane == 1, p2, 0.0)).astype(I32)
    wt_ref[...] = jnp.where(lane == 0, w1, jnp.where(lane == 1, w2, 0.0))


def _route(logits, bias):
    t = logits.shape[0]
    tm = min(TM_ROUTE, t)
    r = jnp.arange(tm)
    ls = jnp.where(r[:, None] > r[None, :], 1.0, 0.0).astype(BF16)
    e = jnp.arange(LANES)
    ut = jnp.where(e[:, None] < e[None, :], 1.0, 0.0).astype(BF16)
    tok = lambda i: (i, 0)
    const = lambda i: (0, 0)
    cnt = pl.pallas_call(
        functools.partial(_count_body, tm=tm),
        grid=(t // tm,),
        in_specs=[pl.BlockSpec((tm, LANES), tok), pl.BlockSpec((1, LANES), const)],
        out_specs=pl.BlockSpec((8, LANES), const),
        out_shape=jax.ShapeDtypeStruct((8, LANES), F32),
        compiler_params=_params(("arbitrary",)),
        name="route_count",
    )(logits, bias)
    pos, wts = pl.pallas_call(
        functools.partial(_assign_body, tm=tm),
        grid=(t // tm,),
        in_specs=[pl.BlockSpec((tm, LANES), tok),
                  pl.BlockSpec((1, LANES), const),
                  pl.BlockSpec((8, LANES), const),
                  pl.BlockSpec((tm, tm), const),
                  pl.BlockSpec((LANES, LANES), const)],
        out_specs=[pl.BlockSpec((tm, LANES), tok), pl.BlockSpec((tm, LANES), tok)],
        out_shape=[jax.ShapeDtypeStruct((t, LANES), I32), jax.ShapeDtypeStruct((t, LANES), F32)],
        scratch_shapes=[pltpu.VMEM((1, LANES), F32), pltpu.VMEM((1, LANES), F32)],
        compiler_params=_params(("arbitrary",)),
        name="route_assign",
    )(logits, bias, cnt, ls, ut)
    return pos, wts, cnt


def _row_copy(src, s, dst, d, sem):
    return pltpu.make_async_copy(src.at[s], dst.at[d], sem)


def _list_copy(src, b, dst, s, sem):
    return pltpu.make_async_copy(src.at[b], dst.at[s], sem.at[s])


def _expert_body(be_ref, nr_ref, tok_ref, u_ref, wg_ref, wu_ref, wd_ref, y_ref, xbuf, tokbuf, sem, tsem, *, n_blocks):
    i = pl.program_id(0)
    slot = i % 2
    nrows = nr_ref[i]
    used = nrows > 0
    nrows_next = nr_ref[jnp.minimum(i + 1, n_blocks - 1)]
    next_used = (i + 1 < n_blocks) & (nrows_next > 0)
    next2_used = (i + 2 < n_blocks) & (nr_ref[jnp.minimum(i + 2, n_blocks - 1)] > 0)

    def pairs(n):
        return (n + 1) // 2

    def gather(s, n):
        def issue(p, carry):
            for k in range(2):
                r = 2 * p + k
                _row_copy(u_ref, tokbuf[s, 0, r], xbuf.at[s], r, sem.at[s]).start(priority=k)
            return carry
        lax.fori_loop(0, pairs(n), issue, 0)

    @pl.when(i == 0)
    def _():
        xbuf[...] = jnp.zeros_like(xbuf)

        @pl.when(used)
        def _():
            _list_copy(tok_ref, 0, tokbuf, 0, tsem).start()
            _list_copy(tok_ref, 0, tokbuf, 0, tsem).wait()
            gather(0, nrows)

        @pl.when(next_used)
        def _():
            _list_copy(tok_ref, 1, tokbuf, 1, tsem).start()

    @pl.when(used)
    def _():
        @pl.when(next2_used)
        def _():
            _list_copy(tok_ref, i + 2, tokbuf, slot, tsem).start()

        @pl.when(next_used)
        def _():
            _list_copy(tok_ref, i + 1, tokbuf, 1 - slot, tsem).wait()
            gather(1 - slot, nrows_next)

        def drain(p, carry):
            for k in range(2):
                _row_copy(u_ref, 0, xbuf.at[slot], 0, sem.at[slot]).wait()
            return carry
        lax.fori_loop(0, pairs(nrows), drain, 0)

    nsub = (nrows + SUB_ROWS - 1) // SUB_ROWS
    for k in range(1, ROW_BLOCK // SUB_ROWS + 1):
        @pl.when(nsub == k)
        def _(k=k):
            m = k * SUB_ROWS
            x = xbuf[slot, 0:m].reshape(m, D)
            a = jnp.dot(x, wg_ref[0], preferred_element_type=F32)
            b = jnp.dot(x, wu_ref[0], preferred_element_type=F32)
            hdn = (a * _sigmoid(a) * b).astype(BF16)
            y = jnp.dot(hdn, wd_ref[0], preferred_element_type=F32)
            y_ref[0:m] = y.astype(BF16).reshape(m, ROW_SUB, LANES)
            if m < ROW_BLOCK:
                y_ref[m:] = jnp.zeros((ROW_BLOCK - m, ROW_SUB, LANES), BF16)

    @pl.when(jnp.logical_not(used))
    def _():
        y_ref[...] = jnp.zeros_like(y_ref)


def _experts(u3, slot_tok, blk_e, blk_rows, w_gate, w_up, w_down):
    n_rows = slot_tok.shape[0]
    nb = n_rows // ROW_BLOCK
    rows = lambda i, be, nr: (i, 0, 0)
    wsel = lambda i, be, nr: (be[i], 0, 0)
    grid_spec = pltpu.PrefetchScalarGridSpec(
        num_scalar_prefetch=2,
        grid=(nb,),
        in_specs=[pl.BlockSpec(memory_space=pl.ANY),
                  pl.BlockSpec(memory_space=pl.ANY),
                  pl.BlockSpec((1, D, EXPERT_HIDDEN), wsel),
                  pl.BlockSpec((1, D, EXPERT_HIDDEN), wsel),
                  pl.BlockSpec((1, EXPERT_HIDDEN, D), wsel)],
        out_specs=pl.BlockSpec((ROW_BLOCK, ROW_SUB, LANES), rows),
        scratch_shapes=[pltpu.VMEM((2, ROW_BLOCK, ROW_SUB, LANES), BF16),
                        pltpu.SMEM((2, 1, ROW_BLOCK), I32),
                        pltpu.SemaphoreType.DMA((2,)),
                        pltpu.SemaphoreType.DMA((2,))],
    )
    return pl.pallas_call(
        functools.partial(_expert_body, n_blocks=nb),
        grid_spec=grid_spec,
        out_shape=jax.ShapeDtypeStruct((n_rows, ROW_SUB, LANES), BF16),
        compiler_params=_params(("arbitrary",)),
        name="experts",
    )(blk_e, blk_rows, slot_tok.reshape(nb, 1, ROW_BLOCK), u3, w_gate, w_up, w_down)


def _combine_body(cur_ref, nxt_ref, yb_ref, wt_ref, h_ref, g2_ref, nf_ref, o_ref, gbuf, sem, *, tm, n_steps):
    i = pl.program_id(0)
    slot = i % 2

    def gather(pos_ref, s):
        def issue(r, carry):
            for k in range(2):
                _row_copy(yb_ref, pos_ref[0, 0, k * tm + r], gbuf.at[s, k], r, sem.at[s]).start(priority=k)
            return carry
        lax.fori_loop(0, tm, issue, 0, unroll=4)

    @pl.when(i == 0)
    def _():
        gather(cur_ref, 0)

    @pl.when(i + 1 < n_steps)
    def _():
        gather(nxt_ref, 1 - slot)

    def drain(r, carry):
        for k in range(2):
            _row_copy(yb_ref, 0, gbuf.at[slot, k], 0, sem.at[slot]).wait()
        return carry
    lax.fori_loop(0, tm, drain, 0, unroll=4)

    wt = wt_ref[...]
    ya = gbuf[slot, 0].reshape(tm, D).astype(F32)
    yb = gbuf[slot, 1].reshape(tm, D).astype(F32)
    y = wt[:, 0:1] * ya + wt[:, 1:2] * yb
    h = h_ref[...] + g2_ref[0] * y
    ms = jnp.mean(h * h, axis=-1, keepdims=True)
    o_ref[...] = h * lax.rsqrt(ms + EPS) * nf_ref[...]


def _combine(pos, yb, wts, h1, gate2, norm_final, seq):
    t = h1.shape[0]
    tm = min(TM_ROWS, seq)
    per_seq = seq // tm
    tok = lambda i: (i, 0)
    n_steps = t // tm
    pos3 = jnp.concatenate([pos[:, 0].reshape(n_steps, 1, tm), pos[:, 1].reshape(n_steps, 1, tm)], axis=2)
    return pl.pallas_call(
        functools.partial(_combine_body, tm=tm, n_steps=n_steps),
        grid=(n_steps,),
        in_specs=[pl.BlockSpec((1, 1, 2 * tm), lambda i: (i, 0, 0), memory_space=pltpu.SMEM),
                  pl.BlockSpec((1, 1, 2 * tm), lambda i: (jnp.minimum(i + 1, n_steps - 1), 0, 0),
                               memory_space=pltpu.SMEM),
                  pl.BlockSpec(memory_space=pl.ANY),
                  pl.BlockSpec((tm, LANES), tok),
                  pl.BlockSpec((tm, D), tok),
                  pl.BlockSpec((1, 1, D), lambda i: (i // per_seq, 0, 0)),
                  pl.BlockSpec((1, D), lambda i: (0, 0))],
        out_specs=pl.BlockSpec((tm, D), tok),
        out_shape=jax.ShapeDtypeStruct((t, D), F32),
        scratch_shapes=[pltpu.VMEM((2, 2, tm, ROW_SUB, LANES), BF16), pltpu.SemaphoreType.DMA((2,))],
        compiler_params=_params(("arbitrary",)),
        name="combine",
    )(pos3, pos3, yb, wts, h1, gate2, norm_final)


def _layer(h2d, c, bsz, seq, w_ada, b_ada, norm_mix, w_in, w_gk2, b_gk2, gla_norm, w_a, w_pool, pool_scale, w_b,
           w_out, norm_ffn, w_rg, b_rg, w_re, b_re, w_gate, w_up, w_down):
    t = bsz * seq
    mod = _ada(c, w_ada, b_ada)
    shift1, scale1, gate1, shift2, scale2, gate2 = [m.reshape(bsz, 1, D) for m in jnp.split(mod, 6, axis=-1)]

    c_gk = 2 * KEY_DIM + 2 * VAL_DIM
    c_p = c_gk + GATE_RANK
    c_ga = c_p + POOL_DIM
    w_main = jnp.concatenate([w_in[:, :c_gk], w_in[:, c_ga:], w_in[:, c_p:c_ga]], axis=1).astype(BF16)
    w_gk = jnp.zeros((D, LANES), F32).at[:, :GATE_RANK].set(w_in[:, c_gk:c_p]).astype(BF16)

    z, gk, wg16, wu16, wd16 = _inproj(h2d, norm_mix.reshape(1, D), scale1, shift1, w_main, w_gk,
                                      w_gate, w_up, w_down, seq)
    o_gla = _gla(z, gk, w_gk2, b_gk2, gla_norm, bsz, seq)
    cb = _pool(z, w_pool, pool_scale, w_b, seq)

    w_router = jnp.zeros((D, LANES), F32).at[:, :N_GROUPS].set(w_rg)
    w_router = w_router.at[:, N_GROUPS:N_GROUPS + N_EXPERTS].set(w_re).astype(BF16)
    h1, u2, logits = _mix(o_gla, z, cb, h2d, gate1, norm_ffn.reshape(1, D), scale2, shift2,
                          w_a.astype(BF16), w_out.astype(BF16), w_router, seq)

    bias = jnp.zeros((1, LANES), F32).at[0, :N_GROUPS].set(b_rg).at[0, N_GROUPS:N_GROUPS + N_EXPERTS].set(b_re)
    pos, wts, cnt = _route(logits, bias)
    counts = cnt[0, :N_EXPERTS].astype(I32)
    nblk = (counts + ROW_BLOCK - 1) // ROW_BLOCK
    bends = jnp.cumsum(nblk)
    n_rows = 2 * t + N_EXPERTS * ROW_BLOCK
    n_blocks = n_rows // ROW_BLOCK
    blk = jnp.arange(n_blocks, dtype=I32)
    blk_e = jnp.sum(bends[None, :] <= blk[:, None], axis=1)
    blk_e = jnp.minimum(blk_e, N_EXPERTS - 1).astype(I32)
    blk_rows = jnp.clip(counts[blk_e] - (blk - (bends - nblk)[blk_e]) * ROW_BLOCK, 0, ROW_BLOCK)
    blk_rows = jnp.where(blk < bends[-1], blk_rows, 0).astype(I32)
    pstart = (bends - nblk) * ROW_BLOCK
    n_fill = nblk * ROW_BLOCK - counts
    m = jnp.arange(ROW_BLOCK, dtype=I32)[None, :]
    e_col = jnp.arange(N_EXPERTS, dtype=I32)[:, None]
    fill_key = jnp.where(m < n_fill[:, None], (pstart + counts)[:, None] + m, n_rows + e_col * ROW_BLOCK + m)
    tok = jnp.arange(t, dtype=I32)
    keys = jnp.concatenate([pos[:, 0], pos[:, 1], fill_key.reshape(-1)])
    vals = jnp.concatenate([tok, tok, jnp.zeros((N_EXPERTS * ROW_BLOCK,), I32)])
    slot_tok = lax.sort((keys, vals), num_keys=1)[1]

    yb = _experts(u2, slot_tok, blk_e, blk_rows, wg16, wu16, wd16)
    return pos, yb, wts, h1, gate2


def kernel(x, c, w_ada, b_ada, norm_mix, w_in, w_gk2, b_gk2, gla_norm, w_a, w_pool, pool_scale, w_b, w_out, norm_ffn, w_rg, b_rg, w_re, b_re, w_gate, w_up, w_down, norm_final):
    bsz, seq, _ = x.shape
    depth = w_ada.shape[0]
    assert depth == 1
    h2d = x.reshape(bsz * seq, D)
    pos, yb, wts, h1, gate2 = _layer(
        h2d, c, bsz, seq, w_ada[0], b_ada[0], norm_mix[0], w_in[0], w_gk2[0], b_gk2[0], gla_norm[0], w_a[0],
        w_pool[0], pool_scale[0], w_b[0], w_out[0], norm_ffn[0], w_rg[0], b_rg[0], w_re[0], b_re[0],
        w_gate[0], w_up[0], w_down[0])
    out = _combine(pos, yb, wts, h1, gate2, norm_final.reshape(1, D), seq)
    return out.reshape(bsz, seq, D)
```

```python
import functools

import jax
import jax.numpy as jnp
from jax import lax
from jax.experimental import pallas as pl
from jax.experimental.pallas import tpu as pltpu

F32 = jnp.float32
BF16 = jnp.bfloat16
I32 = jnp.int32

D = 2048
HEADS = 4
DK = 256
DV = 512
KEY_DIM = HEADS * DK
VAL_DIM = HEADS * DV
GATE_RANK = 16
GATE_NORMALIZER = 16.0
CHUNK = 64
POOL_WINDOWS = (2, 4, 8, 16)
POOL_DIM = 1024
POOL_GROUP_DIM = 256
POOL_HALO = 16
N_GROUPS = 8
EXPERTS_PER_GROUP = 8
N_EXPERTS = 64
EXPERT_HIDDEN = 1024
EPS = 1e-6
LANES = 128
ROW_SUB = D // LANES

Z_COLS = 2 * KEY_DIM + 2 * VAL_DIM + 2 * D + POOL_DIM

VMEM_LIMIT = 56 * 1024 * 1024
VMEM_LIMIT_INPROJ = 60 * 1024 * 1024
CAST_STEPS = 8

TM_IN = 1024
TN_IN = 1024
TC_GLA = 512
TM_POOL = 512
TM_MIX = 256
TM_ROUTE = 1024
TM_ROWS = 256
ROW_BLOCK = 512
SUB_ROWS = 128


def _sigmoid(x):
    return 1.0 / (1.0 + jnp.exp(-x))


def _params(sem, vmem=VMEM_LIMIT):
    return pltpu.CompilerParams(dimension_semantics=sem, vmem_limit_bytes=vmem)


def _ada_body(c_ref, w_ref, b_ref, o_ref):
    c = c_ref[...]
    s = (c * _sigmoid(c)).astype(BF16)
    o_ref[...] = jnp.dot(s, w_ref[...].astype(BF16), preferred_element_type=F32) + b_ref[...]


def _ada(c, w, b):
    bsz = c.shape[0]
    cp = jnp.zeros((8, D), F32).at[:bsz].set(c)
    n = w.shape[1]
    tn = 1024
    out = pl.pallas_call(
        _ada_body,
        grid=(n // tn,),
        in_specs=[pl.BlockSpec((8, D), lambda j: (0, 0)),
                  pl.BlockSpec((D, tn), lambda j: (0, j)),
                  pl.BlockSpec((1, tn), lambda j: (0, j))],
        out_specs=pl.BlockSpec((8, tn), lambda j: (0, j)),
        out_shape=jax.ShapeDtypeStruct((8, n), F32),
        compiler_params=_params(("arbitrary",)),
        name="ada",
    )(cp, w, b.reshape(1, n))
    return out[:bsz]


def _inproj_body(x_ref, g_ref, sc_ref, sh_ref, w_ref, wgk_ref, eg_ref, eu_ref, ed_ref,
                 z_ref, gk_ref, og_ref, ou_ref, od_ref, u_scr):
    og_ref[...] = eg_ref[...].astype(BF16)
    ou_ref[...] = eu_ref[...].astype(BF16)
    od_ref[...] = ed_ref[...].astype(BF16)

    @pl.when(pl.program_id(1) == 0)
    def _():
        x = x_ref[...]
        ms = jnp.mean(x * x, axis=-1, keepdims=True)
        u = x * lax.rsqrt(ms + EPS) * g_ref[...]
        u = (u * (1.0 + sc_ref[0]) + sh_ref[0]).astype(BF16)
        u_scr[...] = u
        gk_ref[...] = jnp.dot(u, wgk_ref[...], preferred_element_type=F32)

    z_ref[...] = jnp.dot(u_scr[...], w_ref[...], preferred_element_type=F32).astype(BF16)


def _inproj(x2, gain, scale, shift, w_main, w_gk, w_gate, w_up, w_down, seq):
    t = x2.shape[0]
    tm = min(TM_IN, seq)
    per_seq = seq // tm
    n_i = t // tm
    n_j = Z_COLS // TN_IN
    assert CAST_STEPS <= n_j
    n_slabs = n_i * CAST_STEPS
    eg = w_gate.reshape(N_EXPERTS * D, EXPERT_HIDDEN)
    eu = w_up.reshape(N_EXPERTS * D, EXPERT_HIDDEN)
    ed = w_down.reshape(N_EXPERTS * EXPERT_HIDDEN, D)
    rg, rd = eg.shape[0] // n_slabs, ed.shape[0] // n_slabs
    assert rg * n_slabs == eg.shape[0] and rd * n_slabs == ed.shape[0] and rd % 16 == 0
    slab = lambda i, j: (i * CAST_STEPS + jnp.minimum(j, CAST_STEPS - 1), 0)
    z, gk, og, ou, od = pl.pallas_call(
        _inproj_body,
        grid=(n_i, n_j),
        in_specs=[pl.BlockSpec((tm, D), lambda i, j: (i, 0)),
                  pl.BlockSpec((1, D), lambda i, j: (0, 0)),
                  pl.BlockSpec((1, 1, D), lambda i, j: (i // per_seq, 0, 0)),
                  pl.BlockSpec((1, 1, D), lambda i, j: (i // per_seq, 0, 0)),
                  pl.BlockSpec((D, TN_IN), lambda i, j: (0, j)),
                  pl.BlockSpec((D, LANES), lambda i, j: (0, 0)),
                  pl.BlockSpec((rg, EXPERT_HIDDEN), slab),
                  pl.BlockSpec((rg, EXPERT_HIDDEN), slab),
                  pl.BlockSpec((rd, D), slab)],
        out_specs=[pl.BlockSpec((tm, TN_IN), lambda i, j: (i, j)),
                   pl.BlockSpec((tm, LANES), lambda i, j: (i, 0)),
                   pl.BlockSpec((rg, EXPERT_HIDDEN), slab),
                   pl.BlockSpec((rg, EXPERT_HIDDEN), slab),
                   pl.BlockSpec((rd, D), slab)],
        out_shape=[jax.ShapeDtypeStruct((t, Z_COLS), BF16),
                   jax.ShapeDtypeStruct((t, LANES), F32),
                   jax.ShapeDtypeStruct(eg.shape, BF16),
                   jax.ShapeDtypeStruct(eu.shape, BF16),
                   jax.ShapeDtypeStruct(ed.shape, BF16)],
        scratch_shapes=[pltpu.VMEM((tm, D), BF16)],
        compiler_params=_params(("arbitrary", "arbitrary"), vmem=VMEM_LIMIT_INPROJ),
        name="inproj",
    )(x2, gain, scale, shift, w_main, w_gk, eg, eu, ed)
    return (z, gk, og.reshape(w_gate.shape), ou.reshape(w_up.shape), od.reshape(w_down.shape))


def _gla_body(q_ref, k_ref, v_ref, g_ref, gk_ref, wgk_ref, bgk_ref, gn_ref, ll_ref, o_ref, st_ref, *, n_chunks):
    @pl.when(pl.program_id(1) == 0)
    def _():
        st_ref[...] = jnp.zeros_like(st_ref)

    row = lax.broadcasted_iota(I32, (CHUNK, CHUNK), 0)
    col = lax.broadcasted_iota(I32, (CHUNK, CHUNK), 1)
    causal = row >= col
    nt = (((1,), (1,)), ((), ()))
    tn = (((0,), (0,)), ((), ()))

    def chunk(ci, carry):
        r0 = pl.multiple_of(ci * CHUNK, CHUNK)
        rows = pl.ds(r0, CHUNK)
        gk = gk_ref[rows, :].astype(BF16)
        xg = jnp.dot(gk, wgk_ref[...], preferred_element_type=F32) + bgk_ref[...]
        la = (jnp.minimum(xg, 0.0) - jnp.log(1.0 + jnp.exp(-jnp.abs(xg)))) * (1.0 / GATE_NORMALIZER)
        la_hi = la.astype(BF16)
        la_lo = (la - la_hi.astype(F32)).astype(BF16)
        ll = ll_ref[...]
        bb = (jnp.dot(ll, la_hi, preferred_element_type=F32)
              + jnp.dot(ll, la_lo, preferred_element_type=F32))
        b = bb[:CHUNK]
        bl = bb[CHUNK:]
        q = q_ref[rows, :].astype(F32)
        k = k_ref[rows, :].astype(F32)
        qe = (q * (DK ** -0.5) * jnp.exp(b)).astype(BF16)
        ke = (k * jnp.exp(-b)).astype(BF16)
        kd = (k * jnp.exp(bl - b)).astype(BF16)
        dec = jnp.exp(bl[0:1, :])
        for h in range(HEADS):
            sk = slice(h * DK, (h + 1) * DK)
            sv = slice(h * DV, (h + 1) * DV)
            att = lax.dot_general(qe[:, sk], ke[:, sk], nt, preferred_element_type=F32)
            att = jnp.where(causal, att, 0.0).astype(BF16)
            vc = v_ref[rows, sv]
            st = st_ref[h]
            o = (jnp.dot(att, vc, preferred_element_type=F32)
                 + lax.dot_general(qe[:, sk], st.astype(BF16), nt, preferred_element_type=F32))
            st_ref[h] = st * dec[:, sk] + lax.dot_general(vc, kd[:, sk], tn, preferred_element_type=F32)
            ms = jnp.mean(o * o, axis=-1, keepdims=True)
            on = o * lax.rsqrt(ms + EPS) * gn_ref[...]
            gg = g_ref[rows, sv].astype(F32)
            o_ref[rows, sv] = (on * (gg * _sigmoid(gg))).astype(BF16)
        return carry

    lax.fori_loop(0, n_chunks, chunk, 0, unroll=4)


def _gla(z, gk, w_gk2, b_gk2, gla_norm, bsz, seq):
    t = z.shape[0]
    tc = min(TC_GLA, seq)
    per_seq = seq // tc
    wgk = jnp.zeros((LANES, KEY_DIM), F32).at[:GATE_RANK].set(w_gk2).astype(BF16)
    r = jnp.arange(2 * CHUNK)[:, None]
    c = jnp.arange(CHUNK)[None, :]
    ll = jnp.where((r >= CHUNK) | (r >= c), 1.0, 0.0).astype(BF16)
    kb = KEY_DIM // KEY_DIM
    return pl.pallas_call(
        functools.partial(_gla_body, n_chunks=tc // CHUNK),
        grid=(bsz, per_seq),
        in_specs=[pl.BlockSpec((tc, KEY_DIM), lambda b, s: (b * per_seq + s, 0)),
                  pl.BlockSpec((tc, KEY_DIM), lambda b, s: (b * per_seq + s, kb)),
                  pl.BlockSpec((tc, VAL_DIM), lambda b, s: (b * per_seq + s, 1)),
                  pl.BlockSpec((tc, VAL_DIM), lambda b, s: (b * per_seq + s, 2)),
                  pl.BlockSpec((tc, LANES), lambda b, s: (b * per_seq + s, 0)),
                  pl.BlockSpec((LANES, KEY_DIM), lambda b, s: (0, 0)),
                  pl.BlockSpec((1, KEY_DIM), lambda b, s: (0, 0)),
                  pl.BlockSpec((1, DV), lambda b, s: (0, 0)),
                  pl.BlockSpec((2 * CHUNK, CHUNK), lambda b, s: (0, 0))],
        out_specs=pl.BlockSpec((tc, VAL_DIM), lambda b, s: (b * per_seq + s, 0)),
        out_shape=jax.ShapeDtypeStruct((t, VAL_DIM), BF16),
        scratch_shapes=[pltpu.VMEM((HEADS, DV, DK), F32)],
        compiler_params=_params(("arbitrary", "arbitrary")),
        name="gla",
    )(z, z, z, z, gk, wgk, b_gk2.reshape(1, KEY_DIM), gla_norm.reshape(1, DV), ll)


def _pool_body(p_ref, halo_ref, gb_ref, wp_ref, ps_ref, wb_ref, o_ref, *, tm, seq):
    base = (pl.program_id(0) * tm) % seq
    pf = p_ref[...].astype(F32)
    hal = halo_ref[...].astype(F32)
    hal = jnp.where(base == 0, 0.0, hal)
    ext = jnp.concatenate([hal, pf], axis=0)
    pos = base + lax.broadcasted_iota(I32, (tm, 1), 0)
    ys = []
    for gi, w in enumerate(POOL_WINDOWS):
        cols = slice(gi * POOL_GROUP_DIM, (gi + 1) * POOL_GROUP_DIM)
        s = ext[:, cols]
        sh = 1
        while sh < w:
            s = s + pltpu.roll(s, sh, 0)
            sh *= 2
        cnt = jnp.minimum(pos + 1, w).astype(F32)
        m = s[POOL_HALO:] / cnt - pf[:, cols]
        ys.append(jnp.dot(m.astype(BF16), wp_ref[gi], preferred_element_type=F32))
    y = jnp.concatenate(ys, axis=1) * ps_ref[...]
    yb = jnp.dot(y.astype(BF16), wb_ref[...], preferred_element_type=F32)
    o_ref[...] = (yb * _sigmoid(gb_ref[...].astype(F32))).astype(BF16)


def _pool(z, w_pool, pool_scale, w_b, seq):
    t = z.shape[0]
    tm = min(TM_POOL, seq)
    p_blk = (Z_COLS - POOL_DIM) // POOL_DIM
    gb_blk = (2 * KEY_DIM + 2 * VAL_DIM + D) // D
    hb = tm // POOL_HALO
    return pl.pallas_call(
        functools.partial(_pool_body, tm=tm, seq=seq),
        grid=(t // tm,),
        in_specs=[pl.BlockSpec((tm, POOL_DIM), lambda i: (i, p_blk)),
                  pl.BlockSpec((POOL_HALO, POOL_DIM), lambda i: (jnp.maximum(i * hb - 1, 0), p_blk)),
                  pl.BlockSpec((tm, D), lambda i: (i, gb_blk)),
                  pl.BlockSpec((len(POOL_WINDOWS), POOL_GROUP_DIM, POOL_GROUP_DIM), lambda i: (0, 0, 0)),
                  pl.BlockSpec((1, POOL_DIM), lambda i: (0, 0)),
                  pl.BlockSpec((POOL_DIM, D), lambda i: (0, 0))],
        out_specs=pl.BlockSpec((tm, D), lambda i: (i, 0)),
        out_shape=jax.ShapeDtypeStruct((t, D), BF16),
        compiler_params=_params(("arbitrary",)),
        name="pool",
    )(z, z, z, w_pool.astype(BF16), pool_scale.reshape(1, POOL_DIM), w_b.astype(BF16))


def _mix_body(o_ref, ga_ref, cb_ref, x_ref, g1_ref, nf_ref, sc_ref, sh_ref, wa_ref, wo_ref, wr_ref,
              h_ref, u_ref, lg_ref):
    ya = jnp.dot(o_ref[...], wa_ref[...], preferred_element_type=F32)
    merged = ya * _sigmoid(ga_ref[...].astype(F32)) + cb_ref[...].astype(F32)
    mo = jnp.dot(merged.astype(BF16), wo_ref[...], preferred_element_type=F32)
    h = x_ref[...] + g1_ref[0] * mo
    h_ref[...] = h
    ms = jnp.mean(h * h, axis=-1, keepdims=True)
    u = h * lax.rsqrt(ms + EPS) * nf_ref[...]
    u = u * (1.0 + sc_ref[0]) + sh_ref[0]
    ub = u.astype(BF16)
    u_ref[...] = ub.reshape(u_ref.shape)
    lg_ref[...] = jnp.dot(ub, wr_ref[...], preferred_element_type=F32)


def _mix(o_gla, z, cb, x2, gate1, norm_ffn, scale2, shift2, w_a, w_out, w_router, seq):
    t = x2.shape[0]
    tm = min(TM_MIX, seq)
    per_seq = seq // tm
    ga_blk = (2 * KEY_DIM + 2 * VAL_DIM) // D
    tok = lambda i: (i, 0)
    const = lambda i: (0, 0)
    per_b = lambda i: (i // per_seq, 0, 0)
    return pl.pallas_call(
        _mix_body,
        grid=(t // tm,),
        in_specs=[pl.BlockSpec((tm, VAL_DIM), tok),
                  pl.BlockSpec((tm, D), lambda i: (i, ga_blk)),
                  pl.BlockSpec((tm, D), tok),
                  pl.BlockSpec((tm, D), tok),
                  pl.BlockSpec((1, 1, D), per_b),
                  pl.BlockSpec((1, D), const),
                  pl.BlockSpec((1, 1, D), per_b),
                  pl.BlockSpec((1, 1, D), per_b),
                  pl.BlockSpec((VAL_DIM, D), const, pipeline_mode=pl.Buffered(1)),
                  pl.BlockSpec((D, D), const, pipeline_mode=pl.Buffered(1)),
                  pl.BlockSpec((D, LANES), const)],
        out_specs=[pl.BlockSpec((tm, D), tok),
                   pl.BlockSpec((tm, ROW_SUB, LANES), lambda i: (i, 0, 0)),
                   pl.BlockSpec((tm, LANES), tok)],
        out_shape=[jax.ShapeDtypeStruct((t, D), F32),
                   jax.ShapeDtypeStruct((t, ROW_SUB, LANES), BF16),
                   jax.ShapeDtypeStruct((t, LANES), F32)],
        compiler_params=_params(("arbitrary",)),
        name="mix",
    )(o_gla, z, cb, x2, gate1, norm_ffn, scale2, shift2, w_a, w_out, w_router)


def _route_topk(gl, tm):
    lane = lax.broadcasted_iota(I32, (tm, LANES), 1)
    lanef = lane.astype(F32)
    neg = float("-inf")
    big = float(LANES)
    gmask = lane < N_GROUPS
    gmax = jnp.max(jnp.where(gmask, gl, neg), axis=-1, keepdims=True)
    gsel = jnp.min(jnp.where(gmask & (gl == gmax), lanef, big), axis=-1, keepdims=True)
    gsum = jnp.sum(jnp.where(gmask, jnp.exp(gl - gmax), 0.0), axis=-1, keepdims=True)
    pg = 1.0 / gsum
    lo = N_GROUPS + gsel * EXPERTS_PER_GROUP
    emask = (lanef >= lo) & (lanef < lo + EXPERTS_PER_GROUP)
    emax = jnp.max(jnp.where(emask, gl, neg), axis=-1, keepdims=True)
    ee = jnp.where(emask, jnp.exp(gl - emax), 0.0)
    eprob = ee / jnp.sum(ee, axis=-1, keepdims=True)
    v1 = jnp.max(jnp.where(emask, eprob, -1.0), axis=-1, keepdims=True)
    i1 = jnp.min(jnp.where(emask & (eprob == v1), lanef, big), axis=-1, keepdims=True)
    m2 = emask & (lanef != i1)
    v2 = jnp.max(jnp.where(m2, eprob, -1.0), axis=-1, keepdims=True)
    i2 = jnp.min(jnp.where(m2 & (eprob == v2), lanef, big), axis=-1, keepdims=True)
    tot = v1 + v2
    w1 = pg * (v1 / tot)
    w2 = pg * (v2 / tot)
    hit1 = lanef == i1 - N_GROUPS
    hit2 = lanef == i2 - N_GROUPS
    return hit1, hit2, w1, w2


def _count_body(lg_ref, bias_ref, cnt_ref, *, tm):
    @pl.when(pl.program_id(0) == 0)
    def _():
        cnt_ref[...] = jnp.zeros_like(cnt_ref)

    hit1, hit2, _, _ = _route_topk(lg_ref[...] + bias_ref[...], tm)
    oh = jnp.where(hit1 | hit2, 1.0, 0.0)
    cnt_ref[...] += jnp.sum(oh, axis=0, keepdims=True)


def _assign_body(lg_ref, bias_ref, cnt_ref, ls_ref, ut_ref, pos_ref, wt_ref, carry_ref, pst_ref, *, tm):
    lane = lax.broadcasted_iota(I32, (tm, LANES), 1)

    @pl.when(pl.program_id(0) == 0)
    def _():
        cnt = cnt_ref[0:1, :]
        nblk = jnp.floor((cnt + (ROW_BLOCK - 1)) * (1.0 / ROW_BLOCK))
        hi = jnp.floor(nblk * (1.0 / 256.0))
        lo = nblk - 256.0 * hi
        ut = ut_ref[...]
        hi8 = jnp.broadcast_to(hi, (8, LANES)).astype(BF16)
        lo8 = jnp.broadcast_to(lo, (8, LANES)).astype(BF16)
        pre = (256.0 * jnp.dot(hi8, ut, preferred_element_type=F32)
               + jnp.dot(lo8, ut, preferred_element_type=F32))
        pst_ref[...] = pre[0:1] * float(ROW_BLOCK)
        carry_ref[...] = jnp.zeros_like(carry_ref)

    hit1, hit2, w1, w2 = _route_topk(lg_ref[...] + bias_ref[...], tm)
    oh = jnp.where(hit1 | hit2, 1.0, 0.0)
    cum = jnp.dot(ls_ref[...], oh.astype(BF16), preferred_element_type=F32)
    slot = cum + carry_ref[...] + pst_ref[...]
    carry_ref[...] += jnp.sum(oh, axis=0, keepdims=True)
    p1 = jnp.sum(jnp.where(hit1, slot, 0.0), axis=-1, keepdims=True)
    p2 = jnp.sum(jnp.where(hit2, slot, 0.0), axis=-1, keepdims=True)
    pos_ref[...] = jnp.where(lane == 0, p1, jnp.where(lane == 1, p2, 0.0)).astype(I32)
    wt_ref[...] = jnp.where(lane == 0, w1, jnp.where(lane == 1, w2, 0.0))


def _route(logits, bias):
    t = logits.shape[0]
    tm = min(TM_ROUTE, t)
    r = jnp.arange(tm)
    ls = jnp.where(r[:, None] > r[None, :], 1.0, 0.0).astype(BF16)
    e = jnp.arange(LANES)
    ut = jnp.where(e[:, None] < e[None, :], 1.0, 0.0).astype(BF16)
    tok = lambda i: (i, 0)
    const = lambda i: (0, 0)
    cnt = pl.pallas_call(
        functools.partial(_count_body, tm=tm),
        grid=(t // tm,),
        in_specs=[pl.BlockSpec((tm, LANES), tok), pl.BlockSpec((1, LANES), const)],
        out_specs=pl.BlockSpec((8, LANES), const),
        out_shape=jax.ShapeDtypeStruct((8, LANES), F32),
        compiler_params=_params(("arbitrary",)),
        name="route_count",
    )(logits, bias)
    pos, wts = pl.pallas_call(
        functools.partial(_assign_body, tm=tm),
        grid=(t // tm,),
        in_specs=[pl.BlockSpec((tm, LANES), tok),
                  pl.BlockSpec((1, LANES), const),
                  pl.BlockSpec((8, LANES), const),
                  pl.BlockSpec((tm, tm), const),
                  pl.BlockSpec((LANES, LANES), const)],
        out_specs=[pl.BlockSpec((tm, LANES), tok), pl.BlockSpec((tm, LANES), tok)],
        out_shape=[jax.ShapeDtypeStruct((t, LANES), I32), jax.ShapeDtypeStruct((t, LANES), F32)],
        scratch_shapes=[pltpu.VMEM((1, LANES), F32), pltpu.VMEM((1, LANES), F32)],
        compiler_params=_params(("arbitrary",)),
        name="route_assign",
    )(logits, bias, cnt, ls, ut)
    return pos, wts, cnt


def _row_copy(src, s, dst, d, sem):
    return pltpu.make_async_copy(src.at[s], dst.at[d], sem)


def _list_copy(src, b, dst, s, sem):
    return pltpu.make_async_copy(src.at[b], dst.at[s], sem.at[s])


def _expert_body(be_ref, nr_ref, tok_ref, u_ref, wg_ref, wu_ref, wd_ref, y_ref, xbuf, tokbuf, sem, tsem, *, n_blocks):
    i = pl.program_id(0)
    slot = i % 2
    nrows = nr_ref[i]
    used = nrows > 0
    nrows_next = nr_ref[jnp.minimum(i + 1, n_blocks - 1)]
    next_used = (i + 1 < n_blocks) & (nrows_next > 0)
    next2_used = (i + 2 < n_blocks) & (nr_ref[jnp.minimum(i + 2, n_blocks - 1)] > 0)

    def pairs(n):
        return (n + 1) // 2

    def gather(s, n):
        def issue(p, carry):
            for k in range(2):
                r = 2 * p + k
                _row_copy(u_ref, tokbuf[s, 0, r], xbuf.at[s], r, sem.at[s]).start(priority=k)
            return carry
        lax.fori_loop(0, pairs(n), issue, 0)

    @pl.when(i == 0)
    def _():
        xbuf[...] = jnp.zeros_like(xbuf)

        @pl.when(used)
        def _():
            _list_copy(tok_ref, 0, tokbuf, 0, tsem).start()
            _list_copy(tok_ref, 0, tokbuf, 0, tsem).wait()
            gather(0, nrows)

        @pl.when(next_used)
        def _():
            _list_copy(tok_ref, 1, tokbuf, 1, tsem).start()

    @pl.when(used)
    def _():
        @pl.when(next2_used)
        def _():
            _list_copy(tok_ref, i + 2, tokbuf, slot, tsem).start()

        @pl.when(next_used)
        def _():
            _list_copy(tok_ref, i + 1, tokbuf, 1 - slot, tsem).wait()
            gather(1 - slot, nrows_next)

        def drain(p, carry):
            for k in range(2):
                _row_copy(u_ref, 0, xbuf.at[slot], 0, sem.at[slot]).wait()
            return carry
        lax.fori_loop(0, pairs(nrows), drain, 0)

    nsub = (nrows + SUB_ROWS - 1) // SUB_ROWS
    for k in range(1, ROW_BLOCK // SUB_ROWS + 1):
        @pl.when(nsub == k)
        def _(k=k):
            m = k * SUB_ROWS
            x = xbuf[slot, 0:m].reshape(m, D)
            a = jnp.dot(x, wg_ref[0], preferred_element_type=F32)
            b = jnp.dot(x, wu_ref[0], preferred_element_type=F32)
            hdn = (a * _sigmoid(a) * b).astype(BF16)
            y = jnp.dot(hdn, wd_ref[0], preferred_element_type=F32)
            y_ref[0:m] = y.astype(BF16).reshape(m, ROW_SUB, LANES)
            if m < ROW_BLOCK:
                y_ref[m:] = jnp.zeros((ROW_BLOCK - m, ROW_SUB, LANES), BF16)

    @pl.when(jnp.logical_not(used))
    def _():
        y_ref[...] = jnp.zeros_like(y_ref)


def _experts(u3, slot_tok, blk_e, blk_rows, w_gate, w_up, w_down):
    n_rows = slot_tok.shape[0]
    nb = n_rows // ROW_BLOCK
    rows = lambda i, be, nr: (i, 0, 0)
    wsel = lambda i, be, nr: (be[i], 0, 0)
    grid_spec = pltpu.PrefetchScalarGridSpec(
        num_scalar_prefetch=2,
        grid=(nb,),
        in_specs=[pl.BlockSpec(memory_space=pl.ANY),
                  pl.BlockSpec(memory_space=pl.ANY),
                  pl.BlockSpec((1, D, EXPERT_HIDDEN), wsel),
                  pl.BlockSpec((1, D, EXPERT_HIDDEN), wsel),
                  pl.BlockSpec((1, EXPERT_HIDDEN, D), wsel)],
        out_specs=pl.BlockSpec((ROW_BLOCK, ROW_SUB, LANES), rows),
        scratch_shapes=[pltpu.VMEM((2, ROW_BLOCK, ROW_SUB, LANES), BF16),
                        pltpu.SMEM((2, 1, ROW_BLOCK), I32),
                        pltpu.SemaphoreType.DMA((2,)),
                        pltpu.SemaphoreType.DMA((2,))],
    )
    return pl.pallas_call(
        functools.partial(_expert_body, n_blocks=nb),
        grid_spec=grid_spec,
        out_shape=jax.ShapeDtypeStruct((n_rows, ROW_SUB, LANES), BF16),
        compiler_params=_params(("arbitrary",)),
        name="experts",
    )(blk_e, blk_rows, slot_tok.reshape(nb, 1, ROW_BLOCK), u3, w_gate, w_up, w_down)


def _combine_body(cur_ref, nxt_ref, yb_ref, wt_ref, h_ref, g2_ref, nf_ref, o_ref, gbuf, sem, *, tm, n_steps):
    i = pl.program_id(0)
    slot = i % 2

    def gather(pos_ref, s):
        def issue(r, carry):
            for k in range(2):
                _row_copy(yb_ref, pos_ref[0, 0, k * tm + r], gbuf.at[s, k], r, sem.at[s]).start(priority=k)
            return carry
        lax.fori_loop(0, tm, issue, 0, unroll=4)

    @pl.when(i == 0)
    def _():
        gather(cur_ref, 0)

    @pl.when(i + 1 < n_steps)
    def _():
        gather(nxt_ref, 1 - slot)

    def drain(r, carry):
        for k in range(2):
            _row_copy(yb_ref, 0, gbuf.at[slot, k], 0, sem.at[slot]).wait()
        return carry
    lax.fori_loop(0, tm, drain, 0, unroll=4)

    wt = wt_ref[...]
    ya = gbuf[slot, 0].reshape(tm, D).astype(F32)
    yb = gbuf[slot, 1].reshape(tm, D).astype(F32)
    y = wt[:, 0:1] * ya + wt[:, 1:2] * yb
    h = h_ref[...] + g2_ref[0] * y
    ms = jnp.mean(h * h, axis=-1, keepdims=True)
    o_ref[...] = h * lax.rsqrt(ms + EPS) * nf_ref[...]


def _combine(pos, yb, wts, h1, gate2, norm_final, seq):
    t = h1.shape[0]
    tm = min(TM_ROWS, seq)
    per_seq = seq // tm
    tok = lambda i: (i, 0)
    n_steps = t // tm
    pos3 = jnp.concatenate([pos[:, 0].reshape(n_steps, 1, tm), pos[:, 1].reshape(n_steps, 1, tm)], axis=2)
    return pl.pallas_call(
        functools.partial(_combine_body, tm=tm, n_steps=n_steps),
        grid=(n_steps,),
        in_specs=[pl.BlockSpec((1, 1, 2 * tm), lambda i: (i, 0, 0), memory_space=pltpu.SMEM),
                  pl.BlockSpec((1, 1, 2 * tm), lambda i: (jnp.minimum(i + 1, n_steps - 1), 0, 0),
                               memory_space=pltpu.SMEM),
                  pl.BlockSpec(memory_space=pl.ANY),
                  pl.BlockSpec((tm, LANES), tok),
                  pl.BlockSpec((tm, D), tok),
                  pl.BlockSpec((1, 1, D), lambda i: (i // per_seq, 0, 0)),
                  pl.BlockSpec((1, D), lambda i: (0, 0))],
        out_specs=pl.BlockSpec((tm, D), tok),
        out_shape=jax.ShapeDtypeStruct((t, D), F32),
        scratch_shapes=[pltpu.VMEM((2, 2, tm, ROW_SUB, LANES), BF16), pltpu.SemaphoreType.DMA((2,))],
        compiler_params=_params(("arbitrary",)),
        name="combine",
    )(pos3, pos3, yb, wts, h1, gate2, norm_final)


def _layer(h2d, c, bsz, seq, w_ada, b_ada, norm_mix, w_in, w_gk2, b_gk2, gla_norm, w_a, w_pool, pool_scale, w_b,
           w_out, norm_ffn, w_rg, b_rg, w_re, b_re, w_gate, w_up, w_down):
    t = bsz * seq
    mod = _ada(c, w_ada, b_ada)
    shift1, scale1, gate1, shift2, scale2, gate2 = [m.reshape(bsz, 1, D) for m in jnp.split(mod, 6, axis=-1)]

    c_gk = 2 * KEY_DIM + 2 * VAL_DIM
    c_p = c_gk + GATE_RANK
    c_ga = c_p + POOL_DIM
    w_main = jnp.concatenate([w_in[:, :c_gk], w_in[:, c_ga:], w_in[:, c_p:c_ga]], axis=1).astype(BF16)
    w_gk = jnp.zeros((D, LANES), F32).at[:, :GATE_RANK].set(w_in[:, c_gk:c_p]).astype(BF16)

    z, gk, wg16, wu16, wd16 = _inproj(h2d, norm_mix.reshape(1, D), scale1, shift1, w_main, w_gk,
                                      w_gate, w_up, w_down, seq)
    o_gla = _gla(z, gk, w_gk2, b_gk2, gla_norm, bsz, seq)
    cb = _pool(z, w_pool, pool_scale, w_b, seq)

    w_router = jnp.zeros((D, LANES), F32).at[:, :N_GROUPS].set(w_rg)
    w_router = w_router.at[:, N_GROUPS:N_GROUPS + N_EXPERTS].set(w_re).astype(BF16)
    h1, u2, logits = _mix(o_gla, z, cb, h2d, gate1, norm_ffn.reshape(1, D), scale2, shift2,
                          w_a.astype(BF16), w_out.astype(BF16), w_router, seq)

    bias = jnp.zeros((1, LANES), F32).at[0, :N_GROUPS].set(b_rg).at[0, N_GROUPS:N_GROUPS + N_EXPERTS].set(b_re)
    pos, wts, cnt = _route(logits, bias)
    counts = cnt[0, :N_EXPERTS].astype(I32)
    nblk = (counts + ROW_BLOCK - 1) // ROW_BLOCK
    bends = jnp.cumsum(nblk)
    n_rows = 2 * t + N_EXPERTS * ROW_BLOCK
    n_blocks = n_rows // ROW_BLOCK
    blk = jnp.arange(n_blocks, dtype=I32)
    blk_e = jnp.sum(bends[None, :] <= blk[:, None], axis=1)
    blk_e = jnp.minimum(blk_e, N_EXPERTS - 1).astype(I32)
    blk_rows = jnp.clip(counts[blk_e] - (blk - (bends - nblk)[blk_e]) * ROW_BLOCK, 0, ROW_BLOCK)
    blk_rows = jnp.where(blk < bends[-1], blk_rows, 0).astype(I32)
    pstart = (bends - nblk) * ROW_BLOCK
    n_fill = nblk * ROW_BLOCK - counts
    m = jnp.arange(ROW_BLOCK, dtype=I32)[None, :]
    e_col = jnp.arange(N_EXPERTS, dtype=I32)[:, None]
    fill_key = jnp.where(m < n_fill[:, None], (pstart + counts)[:, None] + m, n_rows + e_col * ROW_BLOCK + m)
    tok = jnp.arange(t, dtype=I32)
    keys = jnp.concatenate([pos[:, 0], pos[:, 1], fill_key.reshape(-1)])
    vals = jnp.concatenate([tok, tok, jnp.zeros((N_EXPERTS * ROW_BLOCK,), I32)])
    slot_tok = lax.sort((keys, vals), num_keys=1)[1]

    yb = _experts(u2, slot_tok, blk_e, blk_rows, wg16, wu16, wd16)
    return pos, yb, wts, h1, gate2


def kernel(x, c, w_ada, b_ada, norm_mix, w_in, w_gk2, b_gk2, gla_norm, w_a, w_pool, pool_scale, w_b, w_out, norm_ffn, w_rg, b_rg, w_re, b_re, w_gate, w_up, w_down, norm_final):
    bsz, seq, _ = x.shape
    depth = w_ada.shape[0]
    assert depth == 1
    h2d = x.reshape(bsz * seq, D)
    pos, yb, wts, h1, gate2 = _layer(
        h2d, c, bsz, seq, w_ada[0], b_ada[0], norm_mix[0], w_in[0], w_gk2[0], b_gk2[0], gla_norm[0], w_a[0],
        w_pool[0], pool_scale[0], w_b[0], w_out[0], norm_ffn[0], w_rg[0], b_rg[0], w_re[0], b_re[0],
        w_gate[0], w_up[0], w_down[0])
    out = _combine(pos, yb, wts, h1, gate2, norm_final.reshape(1, D), seq)
    return out.reshape(bsz, seq, D)
```

```python
import functools

import jax
import jax.numpy as jnp
from jax import lax
from jax.experimental import pallas as pl
from jax.experimental.pallas import tpu as pltpu

F32 = jnp.float32
BF16 = jnp.bfloat16
I32 = jnp.int32

D = 2048
HEADS = 4
DK = 256
DV = 512
KEY_DIM = HEADS * DK
VAL_DIM = HEADS * DV
GATE_RANK = 16
GATE_NORMALIZER = 16.0
CHUNK = 64
POOL_WINDOWS = (2, 4, 8, 16)
POOL_DIM = 1024
POOL_GROUP_DIM = 256
POOL_HALO = 16
N_GROUPS = 8
EXPERTS_PER_GROUP = 8
N_EXPERTS = 64
EXPERT_HIDDEN = 1024
EPS = 1e-6
LANES = 128
ROW_SUB = D // LANES

Z_COLS = 2 * KEY_DIM + 2 * VAL_DIM + 2 * D + POOL_DIM

VMEM_LIMIT = 56 * 1024 * 1024
VMEM_LIMIT_INPROJ = 60 * 1024 * 1024
CAST_STEPS = 8

TM_IN = 1024
TN_IN = 1024
TC_GLA = 256
TM_POOL = 512
TM_MIX = 256
TM_ROUTE = 1024
TM_ROWS = 256
ROW_BLOCK = 512
SUB_ROWS = 128


def _sigmoid(x):
    return 1.0 / (1.0 + jnp.exp(-x))


def _params(sem, vmem=VMEM_LIMIT):
    return pltpu.CompilerParams(dimension_semantics=sem, vmem_limit_bytes=vmem)


def _ada_body(c_ref, w_ref, b_ref, o_ref):
    c = c_ref[...]
    s = (c * _sigmoid(c)).astype(BF16)
    o_ref[...] = jnp.dot(s, w_ref[...].astype(BF16), preferred_element_type=F32) + b_ref[...]


def _ada(c, w, b):
    bsz = c.shape[0]
    cp = jnp.zeros((8, D), F32).at[:bsz].set(c)
    n = w.shape[1]
    tn = 1024
    out = pl.pallas_call(
        _ada_body,
        grid=(n // tn,),
        in_specs=[pl.BlockSpec((8, D), lambda j: (0, 0)),
                  pl.BlockSpec((D, tn), lambda j: (0, j)),
                  pl.BlockSpec((1, tn), lambda j: (0, j))],
        out_specs=pl.BlockSpec((8, tn), lambda j: (0, j)),
        out_shape=jax.ShapeDtypeStruct((8, n), F32),
        compiler_params=_params(("arbitrary",)),
        name="ada",
    )(cp, w, b.reshape(1, n))
    return out[:bsz]


def _inproj_body(x_ref, g_ref, sc_ref, sh_ref, w_ref, wgk_ref, eg_ref, z_ref, gk_ref, og_ref, u_scr):
    @pl.when(pl.program_id(1) < CAST_STEPS)
    def _():
        og_ref[...] = eg_ref[...].astype(BF16)

    @pl.when(pl.program_id(1) == 0)
    def _():
        x = x_ref[...]
        ms = jnp.mean(x * x, axis=-1, keepdims=True)
        u = x * lax.rsqrt(ms + EPS) * g_ref[...]
        u = (u * (1.0 + sc_ref[0]) + sh_ref[0]).astype(BF16)
        u_scr[...] = u
        gk_ref[...] = jnp.dot(u, wgk_ref[...], preferred_element_type=F32)

    z_ref[...] = jnp.dot(u_scr[...], w_ref[...], preferred_element_type=F32).astype(BF16)


def _inproj(x2, gain, scale, shift, w_main, w_gk, w_gate, seq):
    t = x2.shape[0]
    tm = min(TM_IN, seq)
    per_seq = seq // tm
    n_i = t // tm
    n_j = Z_COLS // TN_IN
    assert CAST_STEPS <= n_j
    n_slabs = n_i * CAST_STEPS
    eg = w_gate.reshape(N_EXPERTS * D, EXPERT_HIDDEN)
    rg = eg.shape[0] // n_slabs
    assert rg * n_slabs == eg.shape[0] and rg % 16 == 0
    slab = lambda i, j: (i * CAST_STEPS + jnp.minimum(j, CAST_STEPS - 1), 0)
    z, gk, og = pl.pallas_call(
        _inproj_body,
        grid=(n_i, n_j),
        in_specs=[pl.BlockSpec((tm, D), lambda i, j: (i, 0)),
                  pl.BlockSpec((1, D), lambda i, j: (0, 0)),
                  pl.BlockSpec((1, 1, D), lambda i, j: (i // per_seq, 0, 0)),
                  pl.BlockSpec((1, 1, D), lambda i, j: (i // per_seq, 0, 0)),
                  pl.BlockSpec((D, TN_IN), lambda i, j: (0, j)),
                  pl.BlockSpec((D, LANES), lambda i, j: (0, 0)),
                  pl.BlockSpec((rg, EXPERT_HIDDEN), slab)],
        out_specs=[pl.BlockSpec((tm, TN_IN), lambda i, j: (i, j)),
                   pl.BlockSpec((tm, LANES), lambda i, j: (i, 0)),
                   pl.BlockSpec((rg, EXPERT_HIDDEN), slab)],
        out_shape=[jax.ShapeDtypeStruct((t, Z_COLS), BF16),
                   jax.ShapeDtypeStruct((t, LANES), F32),
                   jax.ShapeDtypeStruct(eg.shape, BF16)],
        scratch_shapes=[pltpu.VMEM((tm, D), BF16)],
        compiler_params=_params(("arbitrary", "arbitrary"), vmem=VMEM_LIMIT_INPROJ),
        name="inproj",
    )(x2, gain, scale, shift, w_main, w_gk, eg)
    return z, gk, og.reshape(w_gate.shape)


def _gla_body(q_ref, k_ref, v_ref, g_ref, gk_ref, wgk_ref, bgk_ref, gn_ref, ll_ref, eu_ref, ed_ref,
              o_ref, ou_ref, od_ref, st_ref, *, n_chunks):
    ou_ref[...] = eu_ref[...].astype(BF16)
    od_ref[...] = ed_ref[...].astype(BF16)

    @pl.when(pl.program_id(1) == 0)
    def _():
        st_ref[...] = jnp.zeros_like(st_ref)

    row = lax.broadcasted_iota(I32, (CHUNK, CHUNK), 0)
    col = lax.broadcasted_iota(I32, (CHUNK, CHUNK), 1)
    causal = row >= col
    nt = (((1,), (1,)), ((), ()))
    tn = (((0,), (0,)), ((), ()))

    def chunk(ci, carry):
        r0 = pl.multiple_of(ci * CHUNK, CHUNK)
        rows = pl.ds(r0, CHUNK)
        gk = gk_ref[rows, :].astype(BF16)
        xg = jnp.dot(gk, wgk_ref[...], preferred_element_type=F32) + bgk_ref[...]
        la = (jnp.minimum(xg, 0.0) - jnp.log(1.0 + jnp.exp(-jnp.abs(xg)))) * (1.0 / GATE_NORMALIZER)
        la_hi = la.astype(BF16)
        la_lo = (la - la_hi.astype(F32)).astype(BF16)
        ll = ll_ref[...]
        bb = (jnp.dot(ll, la_hi, preferred_element_type=F32)
              + jnp.dot(ll, la_lo, preferred_element_type=F32))
        b = bb[:CHUNK]
        bl = bb[CHUNK:]
        q = q_ref[rows, :].astype(F32)
        k = k_ref[rows, :].astype(F32)
        qe = (q * (DK ** -0.5) * jnp.exp(b)).astype(BF16)
        ke = (k * jnp.exp(-b)).astype(BF16)
        kd = (k * jnp.exp(bl - b)).astype(BF16)
        dec = jnp.exp(bl[0:1, :])
        for h in range(HEADS):
            sk = slice(h * DK, (h + 1) * DK)
            sv = slice(h * DV, (h + 1) * DV)
            att = lax.dot_general(qe[:, sk], ke[:, sk], nt, preferred_element_type=F32)
            att = jnp.where(causal, att, 0.0).astype(BF16)
            vc = v_ref[rows, sv]
            st = st_ref[h]
            o = (jnp.dot(att, vc, preferred_element_type=F32)
                 + lax.dot_general(qe[:, sk], st.astype(BF16), nt, preferred_element_type=F32))
            st_ref[h] = st * dec[:, sk] + lax.dot_general(vc, kd[:, sk], tn, preferred_element_type=F32)
            ms = jnp.mean(o * o, axis=-1, keepdims=True)
            on = o * lax.rsqrt(ms + EPS) * gn_ref[...]
            gg = g_ref[rows, sv].astype(F32)
            o_ref[rows, sv] = (on * (gg * _sigmoid(gg))).astype(BF16)
        return carry

    lax.fori_loop(0, n_chunks, chunk, 0, unroll=4)


def _gla(z, gk, w_gk2, b_gk2, gla_norm, w_up, w_down, bsz, seq):
    t = z.shape[0]
    tc = min(TC_GLA, seq)
    per_seq = seq // tc
    wgk = jnp.zeros((LANES, KEY_DIM), F32).at[:GATE_RANK].set(w_gk2).astype(BF16)
    r = jnp.arange(2 * CHUNK)[:, None]
    c = jnp.arange(CHUNK)[None, :]
    ll = jnp.where((r >= CHUNK) | (r >= c), 1.0, 0.0).astype(BF16)
    kb = KEY_DIM // KEY_DIM
    n_slabs = bsz * per_seq
    eu = w_up.reshape(N_EXPERTS * D, EXPERT_HIDDEN)
    ed = w_down.reshape(N_EXPERTS * EXPERT_HIDDEN, D)
    ru, rd = eu.shape[0] // n_slabs, ed.shape[0] // n_slabs
    assert ru * n_slabs == eu.shape[0] and rd * n_slabs == ed.shape[0] and rd % 16 == 0
    slab = lambda b, s: (b * per_seq + s, 0)
    o, ou, od = pl.pallas_call(
        functools.partial(_gla_body, n_chunks=tc // CHUNK),
        grid=(bsz, per_seq),
        in_specs=[pl.BlockSpec((tc, KEY_DIM), lambda b, s: (b * per_seq + s, 0)),
                  pl.BlockSpec((tc, KEY_DIM), lambda b, s: (b * per_seq + s, kb)),
                  pl.BlockSpec((tc, VAL_DIM), lambda b, s: (b * per_seq + s, 1)),
                  pl.BlockSpec((tc, VAL_DIM), lambda b, s: (b * per_seq + s, 2)),
                  pl.BlockSpec((tc, LANES), lambda b, s: (b * per_seq + s, 0)),
                  pl.BlockSpec((LANES, KEY_DIM), lambda b, s: (0, 0)),
                  pl.BlockSpec((1, KEY_DIM), lambda b, s: (0, 0)),
                  pl.BlockSpec((1, DV), lambda b, s: (0, 0)),
                  pl.BlockSpec((2 * CHUNK, CHUNK), lambda b, s: (0, 0)),
                  pl.BlockSpec((ru, EXPERT_HIDDEN), slab),
                  pl.BlockSpec((rd, D), slab)],
        out_specs=[pl.BlockSpec((tc, VAL_DIM), lambda b, s: (b * per_seq + s, 0)),
                   pl.BlockSpec((ru, EXPERT_HIDDEN), slab),
                   pl.BlockSpec((rd, D), slab)],
        out_shape=[jax.ShapeDtypeStruct((t, VAL_DIM), BF16),
                   jax.ShapeDtypeStruct(eu.shape, BF16),
                   jax.ShapeDtypeStruct(ed.shape, BF16)],
        scratch_shapes=[pltpu.VMEM((HEADS, DV, DK), F32)],
        compiler_params=_params(("arbitrary", "arbitrary")),
        name="gla",
    )(z, z, z, z, gk, wgk, b_gk2.reshape(1, KEY_DIM), gla_norm.reshape(1, DV), ll, eu, ed)
    return o, ou.reshape(w_up.shape), od.reshape(w_down.shape)


def _pool_body(p_ref, halo_ref, gb_ref, wp_ref, ps_ref, wb_ref, o_ref, *, tm, seq):
    base = (pl.program_id(0) * tm) % seq
    pf = p_ref[...].astype(F32)
    hal = halo_ref[...].astype(F32)
    hal = jnp.where(base == 0, 0.0, hal)
    ext = jnp.concatenate([hal, pf], axis=0)
    pos = base + lax.broadcasted_iota(I32, (tm, 1), 0)
    ys = []
    for gi, w in enumerate(POOL_WINDOWS):
        cols = slice(gi * POOL_GROUP_DIM, (gi + 1) * POOL_GROUP_DIM)
        s = ext[:, cols]
        sh = 1
        while sh < w:
            s = s + pltpu.roll(s, sh, 0)
            sh *= 2
        cnt = jnp.minimum(pos + 1, w).astype(F32)
        m = s[POOL_HALO:] / cnt - pf[:, cols]
        ys.append(jnp.dot(m.astype(BF16), wp_ref[gi], preferred_element_type=F32))
    y = jnp.concatenate(ys, axis=1) * ps_ref[...]
    yb = jnp.dot(y.astype(BF16), wb_ref[...], preferred_element_type=F32)
    o_ref[...] = (yb * _sigmoid(gb_ref[...].astype(F32))).astype(BF16)


def _pool(z, w_pool, pool_scale, w_b, seq):
    t = z.shape[0]
    tm = min(TM_POOL, seq)
    p_blk = (Z_COLS - POOL_DIM) // POOL_DIM
    gb_blk = (2 * KEY_DIM + 2 * VAL_DIM + D) // D
    hb = tm // POOL_HALO
    return pl.pallas_call(
        functools.partial(_pool_body, tm=tm, seq=seq),
        grid=(t // tm,),
        in_specs=[pl.BlockSpec((tm, POOL_DIM), lambda i: (i, p_blk)),
                  pl.BlockSpec((POOL_HALO, POOL_DIM), lambda i: (jnp.maximum(i * hb - 1, 0), p_blk)),
                  pl.BlockSpec((tm, D), lambda i: (i, gb_blk)),
                  pl.BlockSpec((len(POOL_WINDOWS), POOL_GROUP_DIM, POOL_GROUP_DIM), lambda i: (0, 0, 0)),
                  pl.BlockSpec((1, POOL_DIM), lambda i: (0, 0)),
                  pl.BlockSpec((POOL_DIM, D), lambda i: (0, 0))],
        out_specs=pl.BlockSpec((tm, D), lambda i: (i, 0)),
        out_shape=jax.ShapeDtypeStruct((t, D), BF16),
        compiler_params=_params(("arbitrary",)),
        name="pool",
    )(z, z, z, w_pool.astype(BF16), pool_scale.reshape(1, POOL_DIM), w_b.astype(BF16))


def _mix_body(o_ref, ga_ref, cb_ref, x_ref, g1_ref, nf_ref, sc_ref, sh_ref, wa_ref, wo_ref, wr_ref,
              h_ref, u_ref, lg_ref):
    ya = jnp.dot(o_ref[...], wa_ref[...], preferred_element_type=F32)
    merged = ya * _sigmoid(ga_ref[...].astype(F32)) + cb_ref[...].astype(F32)
    mo = jnp.dot(merged.astype(BF16), wo_ref[...], preferred_element_type=F32)
    h = x_ref[...] + g1_ref[0] * mo
    h_ref[...] = h
    ms = jnp.mean(h * h, axis=-1, keepdims=True)
    u = h * lax.rsqrt(ms + EPS) * nf_ref[...]
    u = u * (1.0 + sc_ref[0]) + sh_ref[0]
    ub = u.astype(BF16)
    u_ref[...] = ub.reshape(u_ref.shape)
    lg_ref[...] = jnp.dot(ub, wr_ref[...], preferred_element_type=F32)


def _mix(o_gla, z, cb, x2, gate1, norm_ffn, scale2, shift2, w_a, w_out, w_router, seq):
    t = x2.shape[0]
    tm = min(TM_MIX, seq)
    per_seq = seq // tm
    ga_blk = (2 * KEY_DIM + 2 * VAL_DIM) // D
    tok = lambda i: (i, 0)
    const = lambda i: (0, 0)
    per_b = lambda i: (i // per_seq, 0, 0)
    return pl.pallas_call(
        _mix_body,
        grid=(t // tm,),
        in_specs=[pl.BlockSpec((tm, VAL_DIM), tok),
                  pl.BlockSpec((tm, D), lambda i: (i, ga_blk)),
                  pl.BlockSpec((tm, D), tok),
                  pl.BlockSpec((tm, D), tok),
                  pl.BlockSpec((1, 1, D), per_b),
                  pl.BlockSpec((1, D), const),
                  pl.BlockSpec((1, 1, D), per_b),
                  pl.BlockSpec((1, 1, D), per_b),
                  pl.BlockSpec((VAL_DIM, D), const, pipeline_mode=pl.Buffered(1)),
                  pl.BlockSpec((D, D), const, pipeline_mode=pl.Buffered(1)),
                  pl.BlockSpec((D, LANES), const)],
        out_specs=[pl.BlockSpec((tm, D), tok),
                   pl.BlockSpec((tm, ROW_SUB, LANES), lambda i: (i, 0, 0)),
                   pl.BlockSpec((tm, LANES), tok)],
        out_shape=[jax.ShapeDtypeStruct((t, D), F32),
                   jax.ShapeDtypeStruct((t, ROW_SUB, LANES), BF16),
                   jax.ShapeDtypeStruct((t, LANES), F32)],
        compiler_params=_params(("arbitrary",)),
        name="mix",
    )(o_gla, z, cb, x2, gate1, norm_ffn, scale2, shift2, w_a, w_out, w_router)


def _route_topk(gl, tm):
    lane = lax.broadcasted_iota(I32, (tm, LANES), 1)
    lanef = lane.astype(F32)
    neg = float("-inf")
    big = float(LANES)
    gmask = lane < N_GROUPS
    gmax = jnp.max(jnp.where(gmask, gl, neg), axis=-1, keepdims=True)
    gsel = jnp.min(jnp.where(gmask & (gl == gmax), lanef, big), axis=-1, keepdims=True)
    gsum = jnp.sum(jnp.where(gmask, jnp.exp(gl - gmax), 0.0), axis=-1, keepdims=True)
    pg = 1.0 / gsum
    lo = N_GROUPS + gsel * EXPERTS_PER_GROUP
    emask = (lanef >= lo) & (lanef < lo + EXPERTS_PER_GROUP)
    emax = jnp.max(jnp.where(emask, gl, neg), axis=-1, keepdims=True)
    ee = jnp.where(emask, jnp.exp(gl - emax), 0.0)
    eprob = ee / jnp.sum(ee, axis=-1, keepdims=True)
    v1 = jnp.max(jnp.where(emask, eprob, -1.0), axis=-1, keepdims=True)
    i1 = jnp.min(jnp.where(emask & (eprob == v1), lanef, big), axis=-1, keepdims=True)
    m2 = emask & (lanef != i1)
    v2 = jnp.max(jnp.where(m2, eprob, -1.0), axis=-1, keepdims=True)
    i2 = jnp.min(jnp.where(m2 & (eprob == v2), lanef, big), axis=-1, keepdims=True)
    tot = v1 + v2
    w1 = pg * (v1 / tot)
    w2 = pg * (v2 / tot)
    hit1 = lanef == i1 - N_GROUPS
    hit2 = lanef == i2 - N_GROUPS
    return hit1, hit2, w1, w2


def _count_body(lg_ref, bias_ref, cnt_ref, *, tm):
    @pl.when(pl.program_id(0) == 0)
    def _():
        cnt_ref[...] = jnp.zeros_like(cnt_ref)

    hit1, hit2, _, _ = _route_topk(lg_ref[...] + bias_ref[...], tm)
    oh = jnp.where(hit1 | hit2, 1.0, 0.0)
    cnt_ref[...] += jnp.sum(oh, axis=0, keepdims=True)


def _assign_body(lg_ref, bias_ref, cnt_ref, ls_ref, ut_ref, pos_ref, wt_ref, carry_ref, pst_ref, *, tm):
    lane = lax.broadcasted_iota(I32, (tm, LANES), 1)

    @pl.when(pl.program_id(0) == 0)
    def _():
        cnt = cnt_ref[0:1, :]
        nblk = jnp.floor((cnt + (ROW_BLOCK - 1)) * (1.0 / ROW_BLOCK))
        hi = jnp.floor(nblk * (1.0 / 256.0))
        lo = nblk - 256.0 * hi
        ut = ut_ref[...]
        hi8 = jnp.broadcast_to(hi, (8, LANES)).astype(BF16)
        lo8 = jnp.broadcast_to(lo, (8, LANES)).astype(BF16)
        pre = (256.0 * jnp.dot(hi8, ut, preferred_element_type=F32)
               + jnp.dot(lo8, ut, preferred_element_type=F32))
        pst_ref[...] = pre[0:1] * float(ROW_BLOCK)
        carry_ref[...] = jnp.zeros_like(carry_ref)

    hit1, hit2, w1, w2 = _route_topk(lg_ref[...] + bias_ref[...], tm)
    oh = jnp.where(hit1 | hit2, 1.0, 0.0)
    cum = jnp.dot(ls_ref[...], oh.astype(BF16), preferred_element_type=F32)
    slot = cum + carry_ref[...] + pst_ref[...]
    carry_ref[...] += jnp.sum(oh, axis=0, keepdims=True)
    p1 = jnp.sum(jnp.where(hit1, slot, 0.0), axis=-1, keepdims=True)
    p2 = jnp.sum(jnp.where(hit2, slot, 0.0), axis=-1, keepdims=True)
    pos_ref[...] = jnp.where(lane == 0, p1, jnp.where(lane == 1, p2, 0.0)).astype(I32)
    wt_ref[...] = jnp.where(lane == 0, w1, jnp.where(lane == 1, w2, 0.0))


def _route(logits, bias):
    t = logits.shape[0]
    tm = min(TM_ROUTE, t)
    r = jnp.arange(tm)
    ls = jnp.where(r[:, None] > r[None, :], 1.0, 0.0).astype(BF16)
    e = jnp.arange(LANES)
    ut = jnp.where(e[:, None] < e[None, :], 1.0, 0.0).astype(BF16)
    tok = lambda i: (i, 0)
    const = lambda i: (0, 0)
    cnt = pl.pallas_call(
        functools.partial(_count_body, tm=tm),
        grid=(t // tm,),
        in_specs=[pl.BlockSpec((tm, LANES), tok), pl.BlockSpec((1, LANES), const)],
        out_specs=pl.BlockSpec((8, LANES), const),
        out_shape=jax.ShapeDtypeStruct((8, LANES), F32),
        compiler_params=_params(("arbitrary",)),
        name="route_count",
    )(logits, bias)
    pos, wts = pl.pallas_call(
        functools.partial(_assign_body, tm=tm),
        grid=(t // tm,),
        in_specs=[pl.BlockSpec((tm, LANES), tok),
                  pl.BlockSpec((1, LANES), const),
                  pl.BlockSpec((8, LANES), const),
                  pl.BlockSpec((tm, tm), const),
                  pl.BlockSpec((LANES, LANES), const)],
        out_specs=[pl.BlockSpec((tm, LANES), tok), pl.BlockSpec((tm, LANES), tok)],
        out_shape=[jax.ShapeDtypeStruct((t, LANES), I32), jax.ShapeDtypeStruct((t, LANES), F32)],
        scratch_shapes=[pltpu.VMEM((1, LANES), F32), pltpu.VMEM((1, LANES), F32)],
        compiler_params=_params(("arbitrary",)),
        name="route_assign",
    )(logits, bias, cnt, ls, ut)
    return pos, wts, cnt


def _row_copy(src, s, dst, d, sem):
    return pltpu.make_async_copy(src.at[s], dst.at[d], sem)


def _list_copy(src, b, dst, s, sem):
    return pltpu.make_async_copy(src.at[b], dst.at[s], sem.at[s])


def _expert_body(be_ref, nr_ref, tok_ref, u_ref, wg_ref, wu_ref, wd_ref, y_ref, xbuf, tokbuf, sem, tsem, *, n_blocks):
    i = pl.program_id(0)
    slot = i % 2
    nrows = nr_ref[i]
    used = nrows > 0
    nrows_next = nr_ref[jnp.minimum(i + 1, n_blocks - 1)]
    next_used = (i + 1 < n_blocks) & (nrows_next > 0)
    next2_used = (i + 2 < n_blocks) & (nr_ref[jnp.minimum(i + 2, n_blocks - 1)] > 0)

    def pairs(n):
        return (n + 1) // 2

    def gather(s, n):
        def issue(p, carry):
            for k in range(2):
                r = 2 * p + k
                _row_copy(u_ref, tokbuf[s, 0, r], xbuf.at[s], r, sem.at[s]).start(priority=k)
            return carry
        lax.fori_loop(0, pairs(n), issue, 0)

    @pl.when(i == 0)
    def _():
        xbuf[...] = jnp.zeros_like(xbuf)

        @pl.when(used)
        def _():
            _list_copy(tok_ref, 0, tokbuf, 0, tsem).start()
            _list_copy(tok_ref, 0, tokbuf, 0, tsem).wait()
            gather(0, nrows)

        @pl.when(next_used)
        def _():
            _list_copy(tok_ref, 1, tokbuf, 1, tsem).start()

    @pl.when(used)
    def _():
        @pl.when(next2_used)
        def _():
            _list_copy(tok_ref, i + 2, tokbuf, slot, tsem).start()

        @pl.when(next_used)
        def _():
            _list_copy(tok_ref, i + 1, tokbuf, 1 - slot, tsem).wait()
            gather(1 - slot, nrows_next)

        def drain(p, carry):
            for k in range(2):
                _row_copy(u_ref, 0, xbuf.at[slot], 0, sem.at[slot]).wait()
            return carry
        lax.fori_loop(0, pairs(nrows), drain, 0)

    nsub = (nrows + SUB_ROWS - 1) // SUB_ROWS
    for k in range(1, ROW_BLOCK // SUB_ROWS + 1):
        @pl.when(nsub == k)
        def _(k=k):
            m = k * SUB_ROWS
            x = xbuf[slot, 0:m].reshape(m, D)
            a = jnp.dot(x, wg_ref[0], preferred_element_type=F32)
            b = jnp.dot(x, wu_ref[0], preferred_element_type=F32)
            hdn = (a * _sigmoid(a) * b).astype(BF16)
            y = jnp.dot(hdn, wd_ref[0], preferred_element_type=F32)
            y_ref[0:m] = y.astype(BF16).reshape(m, ROW_SUB, LANES)
            if m < ROW_BLOCK:
                y_ref[m:] = jnp.zeros((ROW_BLOCK - m, ROW_SUB, LANES), BF16)

    @pl.when(jnp.logical_not(used))
    def _():
        y_ref[...] = jnp.zeros_like(y_ref)


def _experts(u3, slot_tok, blk_e, blk_rows, w_gate, w_up, w_down):
    n_rows = slot_tok.shape[0]
    nb = n_rows // ROW_BLOCK
    rows = lambda i, be, nr: (i, 0, 0)
    wsel = lambda i, be, nr: (be[i], 0, 0)
    grid_spec = pltpu.PrefetchScalarGridSpec(
        num_scalar_prefetch=2,
        grid=(nb,),
        in_specs=[pl.BlockSpec(memory_space=pl.ANY),
                  pl.BlockSpec(memory_space=pl.ANY),
                  pl.BlockSpec((1, D, EXPERT_HIDDEN), wsel),
                  pl.BlockSpec((1, D, EXPERT_HIDDEN), wsel),
                  pl.BlockSpec((1, EXPERT_HIDDEN, D), wsel)],
        out_specs=pl.BlockSpec((ROW_BLOCK, ROW_SUB, LANES), rows),
        scratch_shapes=[pltpu.VMEM((2, ROW_BLOCK, ROW_SUB, LANES), BF16),
                        pltpu.SMEM((2, 1, ROW_BLOCK), I32),
                        pltpu.SemaphoreType.DMA((2,)),
                        pltpu.SemaphoreType.DMA((2,))],
    )
    return pl.pallas_call(
        functools.partial(_expert_body, n_blocks=nb),
        grid_spec=grid_spec,
        out_shape=jax.ShapeDtypeStruct((n_rows, ROW_SUB, LANES), BF16),
        compiler_params=_params(("arbitrary",)),
        name="experts",
    )(blk_e, blk_rows, slot_tok.reshape(nb, 1, ROW_BLOCK), u3, w_gate, w_up, w_down)


def _combine_body(cur_ref, nxt_ref, yb_ref, wt_ref, h_ref, g2_ref, nf_ref, o_ref, gbuf, sem, *, tm, n_steps):
    i = pl.program_id(0)
    slot = i % 2

    def gather(pos_ref, s):
        def issue(r, carry):
            for k in range(2):
                _row_copy(yb_ref, pos_ref[0, 0, k * tm + r], gbuf.at[s, k], r, sem.at[s]).start(priority=k)
            return carry
        lax.fori_loop(0, tm, issue, 0, unroll=4)

    @pl.when(i == 0)
    def _():
        gather(cur_ref, 0)

    @pl.when(i + 1 < n_steps)
    def _():
        gather(nxt_ref, 1 - slot)

    def drain(r, carry):
        for k in range(2):
            _row_copy(yb_ref, 0, gbuf.at[slot, k], 0, sem.at[slot]).wait()
        return carry
    lax.fori_loop(0, tm, drain, 0, unroll=4)

    wt = wt_ref[...]
    ya = gbuf[slot, 0].reshape(tm, D).astype(F32)
    yb = gbuf[slot, 1].reshape(tm, D).astype(F32)
    y = wt[:, 0:1] * ya + wt[:, 1:2] * yb
    h = h_ref[...] + g2_ref[0] * y
    ms = jnp.mean(h * h, axis=-1, keepdims=True)
    o_ref[...] = h * lax.rsqrt(ms + EPS) * nf_ref[...]


def _combine(pos, yb, wts, h1, gate2, norm_final, seq):
    t = h1.shape[0]
    tm = min(TM_ROWS, seq)
    per_seq = seq // tm
    tok = lambda i: (i, 0)
    n_steps = t // tm
    pos3 = jnp.concatenate([pos[:, 0].reshape(n_steps, 1, tm), pos[:, 1].reshape(n_steps, 1, tm)], axis=2)
    return pl.pallas_call(
        functools.partial(_combine_body, tm=tm, n_steps=n_steps),
        grid=(n_steps,),
        in_specs=[pl.BlockSpec((1, 1, 2 * tm), lambda i: (i, 0, 0), memory_space=pltpu.SMEM),
                  pl.BlockSpec((1, 1, 2 * tm), lambda i: (jnp.minimum(i + 1, n_steps - 1), 0, 0),
                               memory_space=pltpu.SMEM),
                  pl.BlockSpec(memory_space=pl.ANY),
                  pl.BlockSpec((tm, LANES), tok),
                  pl.BlockSpec((tm, D), tok),
                  pl.BlockSpec((1, 1, D), lambda i: (i // per_seq, 0, 0)),
                  pl.BlockSpec((1, D), lambda i: (0, 0))],
        out_specs=pl.BlockSpec((tm, D), tok),
        out_shape=jax.ShapeDtypeStruct((t, D), F32),
        scratch_shapes=[pltpu.VMEM((2, 2, tm, ROW_SUB, LANES), BF16), pltpu.SemaphoreType.DMA((2,))],
        compiler_params=_params(("arbitrary",)),
        name="combine",
    )(pos3, pos3, yb, wts, h1, gate2, norm_final)


def _layer(h2d, c, bsz, seq, w_ada, b_ada, norm_mix, w_in, w_gk2, b_gk2, gla_norm, w_a, w_pool, pool_scale, w_b,
           w_out, norm_ffn, w_rg, b_rg, w_re, b_re, w_gate, w_up, w_down):
    t = bsz * seq
    mod = _ada(c, w_ada, b_ada)
    shift1, scale1, gate1, shift2, scale2, gate2 = [m.reshape(bsz, 1, D) for m in jnp.split(mod, 6, axis=-1)]

    c_gk = 2 * KEY_DIM + 2 * VAL_DIM
    c_p = c_gk + GATE_RANK
    c_ga = c_p + POOL_DIM
    w_main = jnp.concatenate([w_in[:, :c_gk], w_in[:, c_ga:], w_in[:, c_p:c_ga]], axis=1).astype(BF16)
    w_gk = jnp.zeros((D, LANES), F32).at[:, :GATE_RANK].set(w_in[:, c_gk:c_p]).astype(BF16)

    z, gk, wg16 = _inproj(h2d, norm_mix.reshape(1, D), scale1, shift1, w_main, w_gk, w_gate, seq)
    o_gla, wu16, wd16 = _gla(z, gk, w_gk2, b_gk2, gla_norm, w_up, w_down, bsz, seq)
    cb = _pool(z, w_pool, pool_scale, w_b, seq)

    w_router = jnp.zeros((D, LANES), F32).at[:, :N_GROUPS].set(w_rg)
    w_router = w_router.at[:, N_GROUPS:N_GROUPS + N_EXPERTS].set(w_re).astype(BF16)
    h1, u2, logits = _mix(o_gla, z, cb, h2d, gate1, norm_ffn.reshape(1, D), scale2, shift2,
                          w_a.astype(BF16), w_out.astype(BF16), w_router, seq)

    bias = jnp.zeros((1, LANES), F32).at[0, :N_GROUPS].set(b_rg).at[0, N_GROUPS:N_GROUPS + N_EXPERTS].set(b_re)
    pos, wts, cnt = _route(logits, bias)
    counts = cnt[0, :N_EXPERTS].astype(I32)
    nblk = (counts + ROW_BLOCK - 1) // ROW_BLOCK
    bends = jnp.cumsum(nblk)
    n_rows = 2 * t + N_EXPERTS * ROW_BLOCK
    n_blocks = n_rows // ROW_BLOCK
    blk = jnp.arange(n_blocks, dtype=I32)
    blk_e = jnp.sum(bends[None, :] <= blk[:, None], axis=1)
    blk_e = jnp.minimum(blk_e, N_EXPERTS - 1).astype(I32)
    blk_rows = jnp.clip(counts[blk_e] - (blk - (bends - nblk)[blk_e]) * ROW_BLOCK, 0, ROW_BLOCK)
    blk_rows = jnp.where(blk < bends[-1], blk_rows, 0).astype(I32)
    pstart = (bends - nblk) * ROW_BLOCK
    n_fill = nblk * ROW_BLOCK - counts
    m = jnp.arange(ROW_BLOCK, dtype=I32)[None, :]
    e_col = jnp.arange(N_EXPERTS, dtype=I32)[:, None]
    fill_key = jnp.where(m < n_fill[:, None], (pstart + counts)[:, None] + m, n_rows + e_col * ROW_BLOCK + m)
    tok = jnp.arange(t, dtype=I32)
    keys = jnp.concatenate([pos[:, 0], pos[:, 1], fill_key.reshape(-1)])
    vals = jnp.concatenate([tok, tok, jnp.zeros((N_EXPERTS * ROW_BLOCK,), I32)])
    slot_tok = lax.sort((keys, vals), num_keys=1)[1]

    yb = _experts(u2, slot_tok, blk_e, blk_rows, wg16, wu16, wd16)
    return pos, yb, wts, h1, gate2


def kernel(x, c, w_ada, b_ada, norm_mix, w_in, w_gk2, b_gk2, gla_norm, w_a, w_pool, pool_scale, w_b, w_out, norm_ffn, w_rg, b_rg, w_re, b_re, w_gate, w_up, w_down, norm_final):
    bsz, seq, _ = x.shape
    depth = w_ada.shape[0]
    assert depth == 1
    h2d = x.reshape(bsz * seq, D)
    pos, yb, wts, h1, gate2 = _layer(
        h2d, c, bsz, seq, w_ada[0], b_ada[0], norm_mix[0], w_in[0], w_gk2[0], b_gk2[0], gla_norm[0], w_a[0],
        w_pool[0], pool_scale[0], w_b[0], w_out[0], norm_ffn[0], w_rg[0], b_rg[0], w_re[0], b_re[0],
        w_gate[0], w_up[0], w_down[0])
    out = _combine(pos, yb, wts, h1, gate2, norm_final.reshape(1, D), seq)
    return out.reshape(bsz, seq, D)
```

```python
import functools

import jax
import jax.numpy as jnp
from jax import lax
from jax.experimental import pallas as pl
from jax.experimental.pallas import tpu as pltpu

F32 = jnp.float32
BF16 = jnp.bfloat16
I32 = jnp.int32

D = 2048
HEADS = 4
DK = 256
DV = 512
KEY_DIM = HEADS * DK
VAL_DIM = HEADS * DV
GATE_RANK = 16
GATE_NORMALIZER = 16.0
CHUNK = 64
POOL_WINDOWS = (2, 4, 8, 16)
POOL_DIM = 1024
POOL_GROUP_DIM = 256
POOL_HALO = 16
N_GROUPS = 8
EXPERTS_PER_GROUP = 8
N_EXPERTS = 64
EXPERT_HIDDEN = 1024
EPS = 1e-6
LANES = 128
ROW_SUB = D // LANES

Z_COLS = 2 * KEY_DIM + 2 * VAL_DIM + 2 * D + POOL_DIM

VMEM_LIMIT = 56 * 1024 * 1024
VMEM_LIMIT_INPROJ = 60 * 1024 * 1024
CAST_STEPS = 8

TM_IN = 1024
TN_IN = 1408
TC_GLA = 256
TM_POOL = 512
TM_MIX = 256
TM_ROUTE = 1024
TM_ROWS = 256
ROW_BLOCK = 512
SUB_ROWS = 128


def _sigmoid(x):
    return 1.0 / (1.0 + jnp.exp(-x))


def _params(sem, vmem=VMEM_LIMIT):
    return pltpu.CompilerParams(dimension_semantics=sem, vmem_limit_bytes=vmem)


def _ada_body(c_ref, w_ref, b_ref, o_ref):
    c = c_ref[...]
    s = (c * _sigmoid(c)).astype(BF16)
    o_ref[...] = jnp.dot(s, w_ref[...].astype(BF16), preferred_element_type=F32) + b_ref[...]


def _ada(c, w, b):
    bsz = c.shape[0]
    cp = jnp.zeros((8, D), F32).at[:bsz].set(c)
    n = w.shape[1]
    tn = 1024
    out = pl.pallas_call(
        _ada_body,
        grid=(n // tn,),
        in_specs=[pl.BlockSpec((8, D), lambda j: (0, 0)),
                  pl.BlockSpec((D, tn), lambda j: (0, j)),
                  pl.BlockSpec((1, tn), lambda j: (0, j))],
        out_specs=pl.BlockSpec((8, tn), lambda j: (0, j)),
        out_shape=jax.ShapeDtypeStruct((8, n), F32),
        compiler_params=_params(("arbitrary",)),
        name="ada",
    )(cp, w, b.reshape(1, n))
    return out[:bsz]


def _inproj_body(x_ref, g_ref, sc_ref, sh_ref, w_ref, wgk_ref, eg_ref, z_ref, gk_ref, og_ref, u_scr):
    @pl.when(pl.program_id(1) < CAST_STEPS)
    def _():
        og_ref[...] = eg_ref[...].astype(BF16)

    @pl.when(pl.program_id(1) == 0)
    def _():
        x = x_ref[...]
        ms = jnp.mean(x * x, axis=-1, keepdims=True)
        u = x * lax.rsqrt(ms + EPS) * g_ref[...]
        u = (u * (1.0 + sc_ref[0]) + sh_ref[0]).astype(BF16)
        u_scr[...] = u
        gk_ref[...] = jnp.dot(u, wgk_ref[...], preferred_element_type=F32)

    z_ref[...] = jnp.dot(u_scr[...], w_ref[...], preferred_element_type=F32).astype(BF16)


def _inproj(x2, gain, scale, shift, w_main, w_gk, w_gate, seq):
    t = x2.shape[0]
    tm = min(TM_IN, seq)
    per_seq = seq // tm
    n_i = t // tm
    n_j = Z_COLS // TN_IN
    assert CAST_STEPS <= n_j
    n_slabs = n_i * CAST_STEPS
    eg = w_gate.reshape(N_EXPERTS * D, EXPERT_HIDDEN)
    rg = eg.shape[0] // n_slabs
    assert rg * n_slabs == eg.shape[0] and rg % 16 == 0
    slab = lambda i, j: (i * CAST_STEPS + jnp.minimum(j, CAST_STEPS - 1), 0)
    z, gk, og = pl.pallas_call(
        _inproj_body,
        grid=(n_i, n_j),
        in_specs=[pl.BlockSpec((tm, D), lambda i, j: (i, 0)),
                  pl.BlockSpec((1, D), lambda i, j: (0, 0)),
                  pl.BlockSpec((1, 1, D), lambda i, j: (i // per_seq, 0, 0)),
                  pl.BlockSpec((1, 1, D), lambda i, j: (i // per_seq, 0, 0)),
                  pl.BlockSpec((D, TN_IN), lambda i, j: (0, j)),
                  pl.BlockSpec((D, LANES), lambda i, j: (0, 0)),
                  pl.BlockSpec((rg, EXPERT_HIDDEN), slab)],
        out_specs=[pl.BlockSpec((tm, TN_IN), lambda i, j: (i, j)),
                   pl.BlockSpec((tm, LANES), lambda i, j: (i, 0)),
                   pl.BlockSpec((rg, EXPERT_HIDDEN), slab)],
        out_shape=[jax.ShapeDtypeStruct((t, Z_COLS), BF16),
                   jax.ShapeDtypeStruct((t, LANES), F32),
                   jax.ShapeDtypeStruct(eg.shape, BF16)],
        scratch_shapes=[pltpu.VMEM((tm, D), BF16)],
        compiler_params=_params(("arbitrary", "arbitrary"), vmem=VMEM_LIMIT_INPROJ),
        name="inproj",
    )(x2, gain, scale, shift, w_main, w_gk, eg)
    return z, gk, og.reshape(w_gate.shape)


def _gla_body(q_ref, k_ref, v_ref, g_ref, gk_ref, wgk_ref, bgk_ref, gn_ref, ll_ref, eu_ref, ed_ref,
              o_ref, ou_ref, od_ref, st_ref, *, n_chunks):
    @pl.when(pl.program_id(1) == 0)
    def _():
        st_ref[...] = jnp.zeros_like(st_ref)

    assert n_chunks <= 4
    row = lax.broadcasted_iota(I32, (CHUNK, CHUNK), 0)
    col = lax.broadcasted_iota(I32, (CHUNK, CHUNK), 1)
    causal = row >= col
    nt = (((1,), (1,)), ((), ()))
    tn = (((0,), (0,)), ((), ()))

    def chunk(ci):
        rows = pl.ds(ci * CHUNK, CHUNK)
        gk = gk_ref[rows, :].astype(BF16)
        xg = jnp.dot(gk, wgk_ref[...], preferred_element_type=F32) + bgk_ref[...]
        la = (jnp.minimum(xg, 0.0) - jnp.log(1.0 + jnp.exp(-jnp.abs(xg)))) * (1.0 / GATE_NORMALIZER)
        la_hi = la.astype(BF16)
        la_lo = (la - la_hi.astype(F32)).astype(BF16)
        ll = ll_ref[...]
        bb = (jnp.dot(ll, la_hi, preferred_element_type=F32)
              + jnp.dot(ll, la_lo, preferred_element_type=F32))
        b = bb[:CHUNK]
        bl = bb[CHUNK:]
        q = q_ref[rows, :].astype(F32)
        k = k_ref[rows, :].astype(F32)
        qe = (q * (DK ** -0.5) * jnp.exp(b)).astype(BF16)
        ke = (k * jnp.exp(-b)).astype(BF16)
        kd = (k * jnp.exp(bl - b)).astype(BF16)
        dec = jnp.exp(bl[0:1, :])
        for h in range(HEADS):
            sk = slice(h * DK, (h + 1) * DK)
            sv = slice(h * DV, (h + 1) * DV)
            att = lax.dot_general(qe[:, sk], ke[:, sk], nt, preferred_element_type=F32)
            att = jnp.where(causal, att, 0.0).astype(BF16)
            vc = v_ref[rows, sv]
            st = st_ref[h]
            o = (jnp.dot(att, vc, preferred_element_type=F32)
                 + lax.dot_general(qe[:, sk], st.astype(BF16), nt, preferred_element_type=F32))
            st_ref[h] = st * dec[:, sk] + lax.dot_general(vc, kd[:, sk], tn, preferred_element_type=F32)
            ms = jnp.mean(o * o, axis=-1, keepdims=True)
            on = o * lax.rsqrt(ms + EPS) * gn_ref[...]
            gg = g_ref[rows, sv].astype(F32)
            o_ref[rows, sv] = (on * (gg * _sigmoid(gg))).astype(BF16)

    ou_ref[...] = eu_ref[...].astype(BF16)
    od_ref[...] = ed_ref[...].astype(BF16)
    for ci in range(n_chunks):
        chunk(ci)


def _gla(z, gk, w_gk2, b_gk2, gla_norm, w_up, w_down, bsz, seq):
    t = z.shape[0]
    tc = min(TC_GLA, seq)
    per_seq = seq // tc
    wgk = jnp.zeros((LANES, KEY_DIM), F32).at[:GATE_RANK].set(w_gk2).astype(BF16)
    r = jnp.arange(2 * CHUNK)[:, None]
    c = jnp.arange(CHUNK)[None, :]
    ll = jnp.where((r >= CHUNK) | (r >= c), 1.0, 0.0).astype(BF16)
    kb = KEY_DIM // KEY_DIM
    n_slabs = bsz * per_seq
    eu = w_up.reshape(N_EXPERTS * D, EXPERT_HIDDEN)
    ed = w_down.reshape(N_EXPERTS * EXPERT_HIDDEN, D)
    ru, rd = eu.shape[0] // n_slabs, ed.shape[0] // n_slabs
    assert ru * n_slabs == eu.shape[0] and rd * n_slabs == ed.shape[0] and rd % 16 == 0
    slab = lambda b, s: (b * per_seq + s, 0)
    o, ou, od = pl.pallas_call(
        functools.partial(_gla_body, n_chunks=tc // CHUNK),
        grid=(bsz, per_seq),
        in_specs=[pl.BlockSpec((tc, KEY_DIM), lambda b, s: (b * per_seq + s, 0)),
                  pl.BlockSpec((tc, KEY_DIM), lambda b, s: (b * per_seq + s, kb)),
                  pl.BlockSpec((tc, VAL_DIM), lambda b, s: (b * per_seq + s, 1)),
                  pl.BlockSpec((tc, VAL_DIM), lambda b, s: (b * per_seq + s, 2)),
                  pl.BlockSpec((tc, LANES), lambda b, s: (b * per_seq + s, 0)),
                  pl.BlockSpec((LANES, KEY_DIM), lambda b, s: (0, 0)),
                  pl.BlockSpec((1, KEY_DIM), lambda b, s: (0, 0)),
                  pl.BlockSpec((1, DV), lambda b, s: (0, 0)),
                  pl.BlockSpec((2 * CHUNK, CHUNK), lambda b, s: (0, 0)),
                  pl.BlockSpec((ru, EXPERT_HIDDEN), slab),
                  pl.BlockSpec((rd, D), slab)],
        out_specs=[pl.BlockSpec((tc, VAL_DIM), lambda b, s: (b * per_seq + s, 0)),
                   pl.BlockSpec((ru, EXPERT_HIDDEN), slab),
                   pl.BlockSpec((rd, D), slab)],
        out_shape=[jax.ShapeDtypeStruct((t, VAL_DIM), BF16),
                   jax.ShapeDtypeStruct(eu.shape, BF16),
                   jax.ShapeDtypeStruct(ed.shape, BF16)],
        scratch_shapes=[pltpu.VMEM((HEADS, DV, DK), F32)],
        compiler_params=_params(("arbitrary", "arbitrary")),
        name="gla",
    )(z, z, z, z, gk, wgk, b_gk2.reshape(1, KEY_DIM), gla_norm.reshape(1, DV), ll, eu, ed)
    return o, ou.reshape(w_up.shape), od.reshape(w_down.shape)


def _pool_body(p_ref, halo_ref, gb_ref, wp_ref, ps_ref, wb_ref, o_ref, *, tm, seq):
    base = (pl.program_id(0) * tm) % seq
    pf = p_ref[...].astype(F32)
    hal = halo_ref[...].astype(F32)
    hal = jnp.where(base == 0, 0.0, hal)
    ext = jnp.concatenate([hal, pf], axis=0)
    pos = base + lax.broadcasted_iota(I32, (tm, 1), 0)
    ys = []
    for gi, w in enumerate(POOL_WINDOWS):
        cols = slice(gi * POOL_GROUP_DIM, (gi + 1) * POOL_GROUP_DIM)
        s = ext[:, cols]
        sh = 1
        while sh < w:
            s = s + pltpu.roll(s, sh, 0)
            sh *= 2
        cnt = jnp.minimum(pos + 1, w).astype(F32)
        m = s[POOL_HALO:] / cnt - pf[:, cols]
        ys.append(jnp.dot(m.astype(BF16), wp_ref[gi], preferred_element_type=F32))
    y = jnp.concatenate(ys, axis=1) * ps_ref[...]
    yb = jnp.dot(y.astype(BF16), wb_ref[...], preferred_element_type=F32)
    o_ref[...] = (yb * _sigmoid(gb_ref[...].astype(F32))).astype(BF16)


def _pool(z, w_pool, pool_scale, w_b, seq):
    t = z.shape[0]
    tm = min(TM_POOL, seq)
    p_blk = (Z_COLS - POOL_DIM) // POOL_DIM
    gb_blk = (2 * KEY_DIM + 2 * VAL_DIM + D) // D
    hb = tm // POOL_HALO
    return pl.pallas_call(
        functools.partial(_pool_body, tm=tm, seq=seq),
        grid=(t // tm,),
        in_specs=[pl.BlockSpec((tm, POOL_DIM), lambda i: (i, p_blk)),
                  pl.BlockSpec((POOL_HALO, POOL_DIM), lambda i: (jnp.maximum(i * hb - 1, 0), p_blk)),
                  pl.BlockSpec((tm, D), lambda i: (i, gb_blk)),
                  pl.BlockSpec((len(POOL_WINDOWS), POOL_GROUP_DIM, POOL_GROUP_DIM), lambda i: (0, 0, 0)),
                  pl.BlockSpec((1, POOL_DIM), lambda i: (0, 0)),
                  pl.BlockSpec((POOL_DIM, D), lambda i: (0, 0))],
        out_specs=pl.BlockSpec((tm, D), lambda i: (i, 0)),
        out_shape=jax.ShapeDtypeStruct((t, D), BF16),
        compiler_params=_params(("arbitrary",)),
        name="pool",
    )(z, z, z, w_pool.astype(BF16), pool_scale.reshape(1, POOL_DIM), w_b.astype(BF16))


def _mix_body(o_ref, ga_ref, cb_ref, x_ref, g1_ref, nf_ref, sc_ref, sh_ref, wa_ref, wo_ref, wr_ref,
              h_ref, u_ref, lg_ref):
    ya = jnp.dot(o_ref[...], wa_ref[...], preferred_element_type=F32)
    merged = ya * _sigmoid(ga_ref[...].astype(F32)) + cb_ref[...].astype(F32)
    mo = jnp.dot(merged.astype(BF16), wo_ref[...], preferred_element_type=F32)
    h = x_ref[...] + g1_ref[0] * mo
    h_ref[...] = h
    ms = jnp.mean(h * h, axis=-1, keepdims=True)
    u = h * lax.rsqrt(ms + EPS) * nf_ref[...]
    u = u * (1.0 + sc_ref[0]) + sh_ref[0]
    ub = u.astype(BF16)
    u_ref[...] = ub.reshape(u_ref.shape)
    lg_ref[...] = jnp.dot(ub, wr_ref[...], preferred_element_type=F32)


def _mix(o_gla, z, cb, x2, gate1, norm_ffn, scale2, shift2, w_a, w_out, w_router, seq):
    t = x2.shape[0]
    tm = min(TM_MIX, seq)
    per_seq = seq // tm
    ga_blk = (2 * KEY_DIM + 2 * VAL_DIM) // D
    tok = lambda i: (i, 0)
    const = lambda i: (0, 0)
    per_b = lambda i: (i // per_seq, 0, 0)
    return pl.pallas_call(
        _mix_body,
        grid=(t // tm,),
        in_specs=[pl.BlockSpec((tm, VAL_DIM), tok),
                  pl.BlockSpec((tm, D), lambda i: (i, ga_blk)),
                  pl.BlockSpec((tm, D), tok),
                  pl.BlockSpec((tm, D), tok),
                  pl.BlockSpec((1, 1, D), per_b),
                  pl.BlockSpec((1, D), const),
                  pl.BlockSpec((1, 1, D), per_b),
                  pl.BlockSpec((1, 1, D), per_b),
                  pl.BlockSpec((VAL_DIM, D), const, pipeline_mode=pl.Buffered(1)),
                  pl.BlockSpec((D, D), const, pipeline_mode=pl.Buffered(1)),
                  pl.BlockSpec((D, LANES), const)],
        out_specs=[pl.BlockSpec((tm, D), tok),
                   pl.BlockSpec((tm, ROW_SUB, LANES), lambda i: (i, 0, 0)),
                   pl.BlockSpec((tm, LANES), tok)],
        out_shape=[jax.ShapeDtypeStruct((t, D), F32),
                   jax.ShapeDtypeStruct((t, ROW_SUB, LANES), BF16),
                   jax.ShapeDtypeStruct((t, LANES), F32)],
        compiler_params=_params(("arbitrary",)),
        name="mix",
    )(o_gla, z, cb, x2, gate1, norm_ffn, scale2, shift2, w_a, w_out, w_router)


def _route_topk(gl, tm):
    lane = lax.broadcasted_iota(I32, (tm, LANES), 1)
    lanef = lane.astype(F32)
    neg = float("-inf")
    big = float(LANES)
    gmask = lane < N_GROUPS
    gmax = jnp.max(jnp.where(gmask, gl, neg), axis=-1, keepdims=True)
    gsel = jnp.min(jnp.where(gmask & (gl == gmax), lanef, big), axis=-1, keepdims=True)
    gsum = jnp.sum(jnp.where(gmask, jnp.exp(gl - gmax), 0.0), axis=-1, keepdims=True)
    pg = 1.0 / gsum
    lo = N_GROUPS + gsel * EXPERTS_PER_GROUP
    emask = (lanef >= lo) & (lanef < lo + EXPERTS_PER_GROUP)
    emax = jnp.max(jnp.where(emask, gl, neg), axis=-1, keepdims=True)
    ee = jnp.where(emask, jnp.exp(gl - emax), 0.0)
    eprob = ee / jnp.sum(ee, axis=-1, keepdims=True)
    v1 = jnp.max(jnp.where(emask, eprob, -1.0), axis=-1, keepdims=True)
    i1 = jnp.min(jnp.where(emask & (eprob == v1), lanef, big), axis=-1, keepdims=True)
    m2 = emask & (lanef != i1)
    v2 = jnp.max(jnp.where(m2, eprob, -1.0), axis=-1, keepdims=True)
    i2 = jnp.min(jnp.where(m2 & (eprob == v2), lanef, big), axis=-1, keepdims=True)
    tot = v1 + v2
    w1 = pg * (v1 / tot)
    w2 = pg * (v2 / tot)
    hit1 = lanef == i1 - N_GROUPS
    hit2 = lanef == i2 - N_GROUPS
    return hit1, hit2, w1, w2


def _count_body(lg_ref, bias_ref, cnt_ref, *, tm):
    @pl.when(pl.program_id(0) == 0)
    def _():
        cnt_ref[...] = jnp.zeros_like(cnt_ref)

    hit1, hit2, _, _ = _route_topk(lg_ref[...] + bias_ref[...], tm)
    oh = jnp.where(hit1 | hit2, 1.0, 0.0)
    cnt_ref[...] += jnp.sum(oh, axis=0, keepdims=True)


def _assign_body(lg_ref, bias_ref, cnt_ref, ls_ref, ut_ref, pos_ref, wt_ref, carry_ref, pst_ref, *, tm):
    lane = lax.broadcasted_iota(I32, (tm, LANES), 1)

    @pl.when(pl.program_id(0) == 0)
    def _():
        cnt = cnt_ref[0:1, :]
        nblk = jnp.floor((cnt + (ROW_BLOCK - 1)) * (1.0 / ROW_BLOCK))
        hi = jnp.floor(nblk * (1.0 / 256.0))
        lo = nblk - 256.0 * hi
        ut = ut_ref[...]
        hi8 = jnp.broadcast_to(hi, (8, LANES)).astype(BF16)
        lo8 = jnp.broadcast_to(lo, (8, LANES)).astype(BF16)
        pre = (256.0 * jnp.dot(hi8, ut, preferred_element_type=F32)
               + jnp.dot(lo8, ut, preferred_element_type=F32))
        pst_ref[...] = pre[0:1] * float(ROW_BLOCK)
        carry_ref[...] = jnp.zeros_like(carry_ref)

    hit1, hit2, w1, w2 = _route_topk(lg_ref[...] + bias_ref[...], tm)
    oh = jnp.where(hit1 | hit2, 1.0, 0.0)
    cum = jnp.dot(ls_ref[...], oh.astype(BF16), preferred_element_type=F32)
    slot = cum + carry_ref[...] + pst_ref[...]
    carry_ref[...] += jnp.sum(oh, axis=0, keepdims=True)
    p1 = jnp.sum(jnp.where(hit1, slot, 0.0), axis=-1, keepdims=True)
    p2 = jnp.sum(jnp.where(hit2, slot, 0.0), axis=-1, keepdims=True)
    pos_ref[...] = jnp.where(lane == 0, p1, jnp.where(lane == 1, p2, 0.0)).astype(I32)
    wt_ref[...] = jnp.where(lane == 0, w1, jnp.where(lane == 1, w2, 0.0))


def _route(logits, bias):
    t = logits.shape[0]
    tm = min(TM_ROUTE, t)
    r = jnp.arange(tm)
    ls = jnp.where(r[:, None] > r[None, :], 1.0, 0.0).astype(BF16)
    e = jnp.arange(LANES)
    ut = jnp.where(e[:, None] < e[None, :], 1.0, 0.0).astype(BF16)
    tok = lambda i: (i, 0)
    const = lambda i: (0, 0)
    cnt = pl.pallas_call(
        functools.partial(_count_body, tm=tm),
        grid=(t // tm,),
        in_specs=[pl.BlockSpec((tm, LANES), tok), pl.BlockSpec((1, LANES), const)],
        out_specs=pl.BlockSpec((8, LANES), const),
        out_shape=jax.ShapeDtypeStruct((8, LANES), F32),
        compiler_params=_params(("arbitrary",)),
        name="route_count",
    )(logits, bias)
    pos, wts = pl.pallas_call(
        functools.partial(_assign_body, tm=tm),
        grid=(t // tm,),
        in_specs=[pl.BlockSpec((tm, LANES), tok),
                  pl.BlockSpec((1, LANES), const),
                  pl.BlockSpec((8, LANES), const),
                  pl.BlockSpec((tm, tm), const),
                  pl.BlockSpec((LANES, LANES), const)],
        out_specs=[pl.BlockSpec((tm, LANES), tok), pl.BlockSpec((tm, LANES), tok)],
        out_shape=[jax.ShapeDtypeStruct((t, LANES), I32), jax.ShapeDtypeStruct((t, LANES), F32)],
        scratch_shapes=[pltpu.VMEM((1, LANES), F32), pltpu.VMEM((1, LANES), F32)],
        compiler_params=_params(("arbitrary",)),
        name="route_assign",
    )(logits, bias, cnt, ls, ut)
    return pos, wts, cnt


def _row_copy(src, s, dst, d, sem):
    return pltpu.make_async_copy(src.at[s], dst.at[d], sem)


def _list_copy(src, b, dst, s, sem):
    return pltpu.make_async_copy(src.at[b], dst.at[s], sem.at[s])


def _expert_body(be_ref, nr_ref, tok_ref, u_ref, wg_ref, wu_ref, wd_ref, y_ref, xbuf, tokbuf, sem, tsem, *, n_blocks):
    i = pl.program_id(0)
    slot = i % 2
    nrows = nr_ref[i]
    used = nrows > 0
    nrows_next = nr_ref[jnp.minimum(i + 1, n_blocks - 1)]
    next_used = (i + 1 < n_blocks) & (nrows_next > 0)
    next2_used = (i + 2 < n_blocks) & (nr_ref[jnp.minimum(i + 2, n_blocks - 1)] > 0)

    def pairs(n):
        return (n + 1) // 2

    def gather(s, n):
        def issue(p, carry):
            for k in range(2):
                r = 2 * p + k
                _row_copy(u_ref, tokbuf[s, 0, r], xbuf.at[s], r, sem.at[s]).start(priority=k)
            return carry
        lax.fori_loop(0, pairs(n), issue, 0)

    @pl.when(i == 0)
    def _():
        xbuf[...] = jnp.zeros_like(xbuf)

        @pl.when(used)
        def _():
            _list_copy(tok_ref, 0, tokbuf, 0, tsem).start()
            _list_copy(tok_ref, 0, tokbuf, 0, tsem).wait()
            gather(0, nrows)

        @pl.when(next_used)
        def _():
            _list_copy(tok_ref, 1, tokbuf, 1, tsem).start()

    @pl.when(used)
    def _():
        @pl.when(next2_used)
        def _():
            _list_copy(tok_ref, i + 2, tokbuf, slot, tsem).start()

        @pl.when(next_used)
        def _():
            _list_copy(tok_ref, i + 1, tokbuf, 1 - slot, tsem).wait()
            gather(1 - slot, nrows_next)

        def drain(p, carry):
            for k in range(2):
                _row_copy(u_ref, 0, xbuf.at[slot], 0, sem.at[slot]).wait()
            return carry
        lax.fori_loop(0, pairs(nrows), drain, 0)

    nsub = (nrows + SUB_ROWS - 1) // SUB_ROWS
    for k in range(1, ROW_BLOCK // SUB_ROWS + 1):
        @pl.when(nsub == k)
        def _(k=k):
            m = k * SUB_ROWS
            x = xbuf[slot, 0:m].reshape(m, D)
            a = jnp.dot(x, wg_ref[0], preferred_element_type=F32)
            b = jnp.dot(x, wu_ref[0], preferred_element_type=F32)
            hdn = (a * _sigmoid(a) * b).astype(BF16)
            y = jnp.dot(hdn, wd_ref[0], preferred_element_type=F32)
            y_ref[0:m] = y.astype(BF16).reshape(m, ROW_SUB, LANES)
            if m < ROW_BLOCK:
                y_ref[m:] = jnp.zeros((ROW_BLOCK - m, ROW_SUB, LANES), BF16)

    @pl.when(jnp.logical_not(used))
    def _():
        y_ref[...] = jnp.zeros_like(y_ref)


def _experts(u3, slot_tok, blk_e, blk_rows, w_gate, w_up, w_down):
    n_rows = slot_tok.shape[0]
    nb = n_rows // ROW_BLOCK
    rows = lambda i, be, nr: (i, 0, 0)
    wsel = lambda i, be, nr: (be[i], 0, 0)
    grid_spec = pltpu.PrefetchScalarGridSpec(
        num_scalar_prefetch=2,
        grid=(nb,),
        in_specs=[pl.BlockSpec(memory_space=pl.ANY),
                  pl.BlockSpec(memory_space=pl.ANY),
                  pl.BlockSpec((1, D, EXPERT_HIDDEN), wsel),
                  pl.BlockSpec((1, D, EXPERT_HIDDEN), wsel),
                  pl.BlockSpec((1, EXPERT_HIDDEN, D), wsel)],
        out_specs=pl.BlockSpec((ROW_BLOCK, ROW_SUB, LANES), rows),
        scratch_shapes=[pltpu.VMEM((2, ROW_BLOCK, ROW_SUB, LANES), BF16),
                        pltpu.SMEM((2, 1, ROW_BLOCK), I32),
                        pltpu.SemaphoreType.DMA((2,)),
                        pltpu.SemaphoreType.DMA((2,))],
    )
    return pl.pallas_call(
        functools.partial(_expert_body, n_blocks=nb),
        grid_spec=grid_spec,
        out_shape=jax.ShapeDtypeStruct((n_rows, ROW_SUB, LANES), BF16),
        compiler_params=_params(("arbitrary",)),
        name="experts",
    )(blk_e, blk_rows, slot_tok.reshape(nb, 1, ROW_BLOCK), u3, w_gate, w_up, w_down)


def _combine_body(cur_ref, nxt_ref, yb_ref, wt_ref, h_ref, g2_ref, nf_ref, o_ref, gbuf, sem, *, tm, n_steps):
    i = pl.program_id(0)
    slot = i % 2

    def gather(pos_ref, s):
        def issue(r, carry):
            for k in range(2):
                _row_copy(yb_ref, pos_ref[0, 0, k * tm + r], gbuf.at[s, k], r, sem.at[s]).start(priority=k)
            return carry
        lax.fori_loop(0, tm, issue, 0, unroll=4)

    @pl.when(i == 0)
    def _():
        gather(cur_ref, 0)

    @pl.when(i + 1 < n_steps)
    def _():
        gather(nxt_ref, 1 - slot)

    def drain(r, carry):
        for k in range(2):
            _row_copy(yb_ref, 0, gbuf.at[slot, k], 0, sem.at[slot]).wait()
        return carry
    lax.fori_loop(0, tm, drain, 0, unroll=4)

    wt = wt_ref[...]
    ya = gbuf[slot, 0].reshape(tm, D).astype(F32)
    yb = gbuf[slot, 1].reshape(tm, D).astype(F32)
    y = wt[:, 0:1] * ya + wt[:, 1:2] * yb
    h = h_ref[...] + g2_ref[0] * y
    ms = jnp.mean(h * h, axis=-1, keepdims=True)
    o_ref[...] = h * lax.rsqrt(ms + EPS) * nf_ref[...]


def _combine(pos, yb, wts, h1, gate2, norm_final, seq):
    t = h1.shape[0]
    tm = min(TM_ROWS, seq)
    per_seq = seq // tm
    tok = lambda i: (i, 0)
    n_steps = t // tm
    pos3 = jnp.concatenate([pos[:, 0].reshape(n_steps, 1, tm), pos[:, 1].reshape(n_steps, 1, tm)], axis=2)
    return pl.pallas_call(
        functools.partial(_combine_body, tm=tm, n_steps=n_steps),
        grid=(n_steps,),
        in_specs=[pl.BlockSpec((1, 1, 2 * tm), lambda i: (i, 0, 0), memory_space=pltpu.SMEM),
                  pl.BlockSpec((1, 1, 2 * tm), lambda i: (jnp.minimum(i + 1, n_steps - 1), 0, 0),
                               memory_space=pltpu.SMEM),
                  pl.BlockSpec(memory_space=pl.ANY),
                  pl.BlockSpec((tm, LANES), tok),
                  pl.BlockSpec((tm, D), tok),
                  pl.BlockSpec((1, 1, D), lambda i: (i // per_seq, 0, 0)),
                  pl.BlockSpec((1, D), lambda i: (0, 0))],
        out_specs=pl.BlockSpec((tm, D), tok),
        out_shape=jax.ShapeDtypeStruct((t, D), F32),
        scratch_shapes=[pltpu.VMEM((2, 2, tm, ROW_SUB, LANES), BF16), pltpu.SemaphoreType.DMA((2,))],
        compiler_params=_params(("arbitrary",)),
        name="combine",
    )(pos3, pos3, yb, wts, h1, gate2, norm_final)


def _layer(h2d, c, bsz, seq, w_ada, b_ada, norm_mix, w_in, w_gk2, b_gk2, gla_norm, w_a, w_pool, pool_scale, w_b,
           w_out, norm_ffn, w_rg, b_rg, w_re, b_re, w_gate, w_up, w_down):
    t = bsz * seq
    mod = _ada(c, w_ada, b_ada)
    shift1, scale1, gate1, shift2, scale2, gate2 = [m.reshape(bsz, 1, D) for m in jnp.split(mod, 6, axis=-1)]

    c_gk = 2 * KEY_DIM + 2 * VAL_DIM
    c_p = c_gk + GATE_RANK
    c_ga = c_p + POOL_DIM
    w_main = jnp.concatenate([w_in[:, :c_gk], w_in[:, c_ga:], w_in[:, c_p:c_ga]], axis=1).astype(BF16)
    w_gk = jnp.zeros((D, LANES), F32).at[:, :GATE_RANK].set(w_in[:, c_gk:c_p]).astype(BF16)

    z, gk, wg16 = _inproj(h2d, norm_mix.reshape(1, D), scale1, shift1, w_main, w_gk, w_gate, seq)
    o_gla, wu16, wd16 = _gla(z, gk, w_gk2, b_gk2, gla_norm, w_up, w_down, bsz, seq)
    cb = _pool(z, w_pool, pool_scale, w_b, seq)

    w_router = jnp.zeros((D, LANES), F32).at[:, :N_GROUPS].set(w_rg)
    w_router = w_router.at[:, N_GROUPS:N_GROUPS + N_EXPERTS].set(w_re).astype(BF16)
    h1, u2, logits = _mix(o_gla, z, cb, h2d, gate1, norm_ffn.reshape(1, D), scale2, shift2,
                          w_a.astype(BF16), w_out.astype(BF16), w_router, seq)

    bias = jnp.zeros((1, LANES), F32).at[0, :N_GROUPS].set(b_rg).at[0, N_GROUPS:N_GROUPS + N_EXPERTS].set(b_re)
    pos, wts, cnt = _route(logits, bias)
    counts = cnt[0, :N_EXPERTS].astype(I32)
    nblk = (counts + ROW_BLOCK - 1) // ROW_BLOCK
    bends = jnp.cumsum(nblk)
    n_rows = 2 * t + N_EXPERTS * ROW_BLOCK
    n_blocks = n_rows // ROW_BLOCK
    blk = jnp.arange(n_blocks, dtype=I32)
    blk_e = jnp.sum(bends[None, :] <= blk[:, None], axis=1)
    blk_e = jnp.minimum(blk_e, N_EXPERTS - 1).astype(I32)
    blk_rows = jnp.clip(counts[blk_e] - (blk - (bends - nblk)[blk_e]) * ROW_BLOCK, 0, ROW_BLOCK)
    blk_rows = jnp.where(blk < bends[-1], blk_rows, 0).astype(I32)
    pstart = (bends - nblk) * ROW_BLOCK
    n_fill = nblk * ROW_BLOCK - counts
    m = jnp.arange(ROW_BLOCK, dtype=I32)[None, :]
    e_col = jnp.arange(N_EXPERTS, dtype=I32)[:, None]
    fill_key = jnp.where(m < n_fill[:, None], (pstart + counts)[:, None] + m, n_rows + e_col * ROW_BLOCK + m)
    tok = jnp.arange(t, dtype=I32)
    keys = jnp.concatenate([pos[:, 0], pos[:, 1], fill_key.reshape(-1)])
    vals = jnp.concatenate([tok, tok, jnp.zeros((N_EXPERTS * ROW_BLOCK,), I32)])
    slot_tok = lax.sort((keys, vals), num_keys=1)[1]

    yb = _experts(u2, slot_tok, blk_e, blk_rows, wg16, wu16, wd16)
    return pos, yb, wts, h1, gate2


def kernel(x, c, w_ada, b_ada, norm_mix, w_in, w_gk2, b_gk2, gla_norm, w_a, w_pool, pool_scale, w_b, w_out, norm_ffn, w_rg, b_rg, w_re, b_re, w_gate, w_up, w_down, norm_final):
    bsz, seq, _ = x.shape
    depth = w_ada.shape[0]
    assert depth == 1
    h2d = x.reshape(bsz * seq, D)
    pos, yb, wts, h1, gate2 = _layer(
        h2d, c, bsz, seq, w_ada[0], b_ada[0], norm_mix[0], w_in[0], w_gk2[0], b_gk2[0], gla_norm[0], w_a[0],
        w_pool[0], pool_scale[0], w_b[0], w_out[0], norm_ffn[0], w_rg[0], b_rg[0], w_re[0], b_re[0],
        w_gate[0], w_up[0], w_down[0])
    out = _combine(pos, yb, wts, h1, gate2, norm_final.reshape(1, D), seq)
    return out.reshape(bsz, seq, D)
```

```python
import functools

import jax
import jax.numpy as jnp
from jax import lax
from jax.experimental import pallas as pl
from jax.experimental.pallas import tpu as pltpu

F32 = jnp.float32
BF16 = jnp.bfloat16
I32 = jnp.int32

D = 2048
HEADS = 4
DK = 256
DV = 512
KEY_DIM = HEADS * DK
VAL_DIM = HEADS * DV
GATE_RANK = 16
GATE_NORMALIZER = 16.0
CHUNK = 64
POOL_WINDOWS = (2, 4, 8, 16)
POOL_DIM = 1024
POOL_GROUP_DIM = 256
POOL_HALO = 16
N_GROUPS = 8
EXPERTS_PER_GROUP = 8
N_EXPERTS = 64
EXPERT_HIDDEN = 1024
EPS = 1e-6
LANES = 128
ROW_SUB = D // LANES

Z_COLS = 2 * KEY_DIM + 2 * VAL_DIM + 2 * D + POOL_DIM

VMEM_LIMIT = 56 * 1024 * 1024
VMEM_LIMIT_INPROJ = 60 * 1024 * 1024
CAST_STEPS = 8

TM_IN = 1024
TN_IN = 1024
TC_GLA = 256
TM_POOL = 512
TM_MIX = 256
TM_ROUTE = 1024
TM_ROWS = 256
ROW_BLOCK = 512
SUB_ROWS = 128


def _sigmoid(x):
    return 1.0 / (1.0 + jnp.exp(-x))


def _params(sem, vmem=VMEM_LIMIT):
    return pltpu.CompilerParams(dimension_semantics=sem, vmem_limit_bytes=vmem)


def _ada_body(c_ref, w_ref, b_ref, o_ref):
    c = c_ref[...]
    s = (c * _sigmoid(c)).astype(BF16)
    o_ref[...] = jnp.dot(s, w_ref[...].astype(BF16), preferred_element_type=F32) + b_ref[...]


def _ada(c, w, b):
    bsz = c.shape[0]
    cp = jnp.zeros((8, D), F32).at[:bsz].set(c)
    n = w.shape[1]
    tn = 1024
    out = pl.pallas_call(
        _ada_body,
        grid=(n // tn,),
        in_specs=[pl.BlockSpec((8, D), lambda j: (0, 0)),
                  pl.BlockSpec((D, tn), lambda j: (0, j)),
                  pl.BlockSpec((1, tn), lambda j: (0, j))],
        out_specs=pl.BlockSpec((8, tn), lambda j: (0, j)),
        out_shape=jax.ShapeDtypeStruct((8, n), F32),
        compiler_params=_params(("arbitrary",)),
        name="ada",
    )(cp, w, b.reshape(1, n))
    return out[:bsz]


def _inproj_body(x_ref, g_ref, sc_ref, sh_ref, w_ref, wgk_ref, eg_ref, z_ref, gk_ref, og_ref, u_scr):
    @pl.when(pl.program_id(1) < CAST_STEPS)
    def _():
        og_ref[...] = eg_ref[...].astype(BF16)

    @pl.when(pl.program_id(1) == 0)
    def _():
        x = x_ref[...]
        ms = jnp.mean(x * x, axis=-1, keepdims=True)
        u = x * lax.rsqrt(ms + EPS) * g_ref[...]
        u = (u * (1.0 + sc_ref[0]) + sh_ref[0]).astype(BF16)
        u_scr[...] = u
        gk_ref[...] = jnp.dot(u, wgk_ref[...], preferred_element_type=F32)

    z_ref[...] = jnp.dot(u_scr[...], w_ref[...], preferred_element_type=F32).astype(BF16)


def _inproj(x2, gain, scale, shift, w_main, w_gk, w_gate, seq):
    t = x2.shape[0]
    tm = min(TM_IN, seq)
    per_seq = seq // tm
    n_i = t // tm
    n_j = Z_COLS // TN_IN
    assert CAST_STEPS <= n_j
    n_slabs = n_i * CAST_STEPS
    eg = w_gate.reshape(N_EXPERTS * D, EXPERT_HIDDEN)
    rg = eg.shape[0] // n_slabs
    assert rg * n_slabs == eg.shape[0] and rg % 16 == 0
    slab = lambda i, j: (i * CAST_STEPS + jnp.minimum(j, CAST_STEPS - 1), 0)
    z, gk, og = pl.pallas_call(
        _inproj_body,
        grid=(n_i, n_j),
        in_specs=[pl.BlockSpec((tm, D), lambda i, j: (i, 0)),
                  pl.BlockSpec((1, D), lambda i, j: (0, 0)),
                  pl.BlockSpec((1, 1, D), lambda i, j: (i // per_seq, 0, 0)),
                  pl.BlockSpec((1, 1, D), lambda i, j: (i // per_seq, 0, 0)),
                  pl.BlockSpec((D, TN_IN), lambda i, j: (0, j)),
                  pl.BlockSpec((D, LANES), lambda i, j: (0, 0)),
                  pl.BlockSpec((rg, EXPERT_HIDDEN), slab)],
        out_specs=[pl.BlockSpec((tm, TN_IN), lambda i, j: (i, j)),
                   pl.BlockSpec((tm, LANES), lambda i, j: (i, 0)),
                   pl.BlockSpec((rg, EXPERT_HIDDEN), slab)],
        out_shape=[jax.ShapeDtypeStruct((t, Z_COLS), BF16),
                   jax.ShapeDtypeStruct((t, LANES), F32),
                   jax.ShapeDtypeStruct(eg.shape, BF16)],
        scratch_shapes=[pltpu.VMEM((tm, D), BF16)],
        compiler_params=_params(("arbitrary", "arbitrary"), vmem=VMEM_LIMIT_INPROJ),
        name="inproj",
    )(x2, gain, scale, shift, w_main, w_gk, eg)
    return z, gk, og.reshape(w_gate.shape)


def _gla_body(q_ref, k_ref, v_ref, g_ref, gk_ref, wgk_ref, bgk_ref, gn_ref, ll_ref, eu_ref, ed_ref,
              o_ref, ou_ref, od_ref, st_ref, *, n_chunks):
    ou_ref[...] = eu_ref[...].astype(BF16)
    od_ref[...] = ed_ref[...].astype(BF16)

    @pl.when(pl.program_id(1) == 0)
    def _():
        st_ref[...] = jnp.zeros_like(st_ref)

    row = lax.broadcasted_iota(I32, (CHUNK, CHUNK), 0)
    col = lax.broadcasted_iota(I32, (CHUNK, CHUNK), 1)
    causal = row >= col
    row2 = lax.broadcasted_iota(I32, (CHUNK, 2 * CHUNK), 0)
    col2 = lax.broadcasted_iota(I32, (CHUNK, 2 * CHUNK), 1)
    causal2 = (col2 < CHUNK) | (row2 >= col2 - CHUNK)
    nt = (((1,), (1,)), ((), ()))
    tn = (((0,), (0,)), ((), ()))

    def gates(rows):
        gk = gk_ref[rows, :].astype(BF16)
        xg = jnp.dot(gk, wgk_ref[...], preferred_element_type=F32) + bgk_ref[...]
        la = (jnp.minimum(xg, 0.0) - jnp.log(1.0 + jnp.exp(-jnp.abs(xg)))) * (1.0 / GATE_NORMALIZER)
        la_hi = la.astype(BF16)
        la_lo = (la - la_hi.astype(F32)).astype(BF16)
        ll = ll_ref[...]
        bb = (jnp.dot(ll, la_hi, preferred_element_type=F32)
              + jnp.dot(ll, la_lo, preferred_element_type=F32))
        b = bb[:CHUNK]
        bl = bb[CHUNK:]
        q = q_ref[rows, :].astype(F32)
        k = k_ref[rows, :].astype(F32)
        qe = q * (DK ** -0.5) * jnp.exp(b)
        ke = (k * jnp.exp(-b)).astype(BF16)
        kd = k * jnp.exp(bl - b)
        return qe, ke, kd, jnp.exp(bl[0:1, :])

    def pair(pi, carry):
        r0 = pl.multiple_of(pi * 2 * CHUNK, 2 * CHUNK)
        rows0 = pl.ds(r0, CHUNK)
        rows1 = pl.ds(r0 + CHUNK, CHUNK)
        rows01 = pl.ds(r0, 2 * CHUNK)
        qe0f, ke0, kd0f, dec0 = gates(rows0)
        qe1f, ke1, kd1f, dec1 = gates(rows1)
        qe0 = qe0f.astype(BF16)
        qe1 = qe1f.astype(BF16)
        q_state = jnp.concatenate([qe0, (qe1f * dec0).astype(BF16)], axis=0)
        k_for1 = jnp.concatenate([kd0f.astype(BF16), ke1], axis=0)
        k_end = jnp.concatenate([(kd0f * dec1).astype(BF16), kd1f.astype(BF16)], axis=0)
        dec01 = dec0 * dec1
        for h in range(HEADS):
            sk = slice(h * DK, (h + 1) * DK)
            sv = slice(h * DV, (h + 1) * DV)
            att0 = lax.dot_general(qe0[:, sk], ke0[:, sk], nt, preferred_element_type=F32)
            att0 = jnp.where(causal, att0, 0.0).astype(BF16)
            att1 = lax.dot_general(qe1[:, sk], k_for1[:, sk], nt, preferred_element_type=F32)
            att1 = jnp.where(causal2, att1, 0.0).astype(BF16)
            v0 = v_ref[rows0, sv]
            v01 = v_ref[rows01, sv]
            st = st_ref[h]
            o_state = lax.dot_general(q_state[:, sk], st.astype(BF16), nt, preferred_element_type=F32)
            o0 = jnp.dot(att0, v0, preferred_element_type=F32) + o_state[:CHUNK]
            o1 = jnp.dot(att1, v01, preferred_element_type=F32) + o_state[CHUNK:]
            st_ref[h] = st * dec01[:, sk] + lax.dot_general(v01, k_end[:, sk], tn, preferred_element_type=F32)
            for o, rows in ((o0, rows0), (o1, rows1)):
                ms = jnp.mean(o * o, axis=-1, keepdims=True)
                on = o * lax.rsqrt(ms + EPS) * gn_ref[...]
                gg = g_ref[rows, sv].astype(F32)
                o_ref[rows, sv] = (on * (gg * _sigmoid(gg))).astype(BF16)
        return carry

    assert n_chunks % 2 == 0
    lax.fori_loop(0, n_chunks // 2, pair, 0, unroll=2)


def _gla(z, gk, w_gk2, b_gk2, gla_norm, w_up, w_down, bsz, seq):
    t = z.shape[0]
    tc = min(TC_GLA, seq)
    per_seq = seq // tc
    wgk = jnp.zeros((LANES, KEY_DIM), F32).at[:GATE_RANK].set(w_gk2).astype(BF16)
    r = jnp.arange(2 * CHUNK)[:, None]
    c = jnp.arange(CHUNK)[None, :]
    ll = jnp.where((r >= CHUNK) | (r >= c), 1.0, 0.0).astype(BF16)
    kb = KEY_DIM // KEY_DIM
    n_slabs = bsz * per_seq
    eu = w_up.reshape(N_EXPERTS * D, EXPERT_HIDDEN)
    ed = w_down.reshape(N_EXPERTS * EXPERT_HIDDEN, D)
    ru, rd = eu.shape[0] // n_slabs, ed.shape[0] // n_slabs
    assert ru * n_slabs == eu.shape[0] and rd * n_slabs == ed.shape[0] and rd % 16 == 0
    slab = lambda b, s: (b * per_seq + s, 0)
    o, ou, od = pl.pallas_call(
        functools.partial(_gla_body, n_chunks=tc // CHUNK),
        grid=(bsz, per_seq),
        in_specs=[pl.BlockSpec((tc, KEY_DIM), lambda b, s: (b * per_seq + s, 0)),
                  pl.BlockSpec((tc, KEY_DIM), lambda b, s: (b * per_seq + s, kb)),
                  pl.BlockSpec((tc, VAL_DIM), lambda b, s: (b * per_seq + s, 1)),
                  pl.BlockSpec((tc, VAL_DIM), lambda b, s: (b * per_seq + s, 2)),
                  pl.BlockSpec((tc, LANES), lambda b, s: (b * per_seq + s, 0)),
                  pl.BlockSpec((LANES, KEY_DIM), lambda b, s: (0, 0)),
                  pl.BlockSpec((1, KEY_DIM), lambda b, s: (0, 0)),
                  pl.BlockSpec((1, DV), lambda b, s: (0, 0)),
                  pl.BlockSpec((2 * CHUNK, CHUNK), lambda b, s: (0, 0)),
                  pl.BlockSpec((ru, EXPERT_HIDDEN), slab),
                  pl.BlockSpec((rd, D), slab)],
        out_specs=[pl.BlockSpec((tc, VAL_DIM), lambda b, s: (b * per_seq + s, 0)),
                   pl.BlockSpec((ru, EXPERT_HIDDEN), slab),
                   pl.BlockSpec((rd, D), slab)],
        out_shape=[jax.ShapeDtypeStruct((t, VAL_DIM), BF16),
                   jax.ShapeDtypeStruct(eu.shape, BF16),
                   jax.ShapeDtypeStruct(ed.shape, BF16)],
        scratch_shapes=[pltpu.VMEM((HEADS, DV, DK), F32)],
        compiler_params=_params(("arbitrary", "arbitrary")),
        name="gla",
    )(z, z, z, z, gk, wgk, b_gk2.reshape(1, KEY_DIM), gla_norm.reshape(1, DV), ll, eu, ed)
    return o, ou.reshape(w_up.shape), od.reshape(w_down.shape)


def _pool_body(p_ref, halo_ref, gb_ref, wp_ref, ps_ref, wb_ref, o_ref, *, tm, seq):
    base = (pl.program_id(0) * tm) % seq
    pf = p_ref[...].astype(F32)
    hal = halo_ref[...].astype(F32)
    hal = jnp.where(base == 0, 0.0, hal)
    ext = jnp.concatenate([hal, pf], axis=0)
    pos = base + lax.broadcasted_iota(I32, (tm, 1), 0)
    ys = []
    for gi, w in enumerate(POOL_WINDOWS):
        cols = slice(gi * POOL_GROUP_DIM, (gi + 1) * POOL_GROUP_DIM)
        s = ext[:, cols]
        sh = 1
        while sh < w:
            s = s + pltpu.roll(s, sh, 0)
            sh *= 2
        cnt = jnp.minimum(pos + 1, w).astype(F32)
        m = s[POOL_HALO:] / cnt - pf[:, cols]
        ys.append(jnp.dot(m.astype(BF16), wp_ref[gi], preferred_element_type=F32))
    y = jnp.concatenate(ys, axis=1) * ps_ref[...]
    yb = jnp.dot(y.astype(BF16), wb_ref[...], preferred_element_type=F32)
    o_ref[...] = (yb * _sigmoid(gb_ref[...].astype(F32))).astype(BF16)


def _pool(z, w_pool, pool_scale, w_b, seq):
    t = z.shape[0]
    tm = min(TM_POOL, seq)
    p_blk = (Z_COLS - POOL_DIM) // POOL_DIM
    gb_blk = (2 * KEY_DIM + 2 * VAL_DIM + D) // D
    hb = tm // POOL_HALO
    return pl.pallas_call(
        functools.partial(_pool_body, tm=tm, seq=seq),
        grid=(t // tm,),
        in_specs=[pl.BlockSpec((tm, POOL_DIM), lambda i: (i, p_blk)),
                  pl.BlockSpec((POOL_HALO, POOL_DIM), lambda i: (jnp.maximum(i * hb - 1, 0), p_blk)),
                  pl.BlockSpec((tm, D), lambda i: (i, gb_blk)),
                  pl.BlockSpec((len(POOL_WINDOWS), POOL_GROUP_DIM, POOL_GROUP_DIM), lambda i: (0, 0, 0)),
                  pl.BlockSpec((1, POOL_DIM), lambda i: (0, 0)),
                  pl.BlockSpec((POOL_DIM, D), lambda i: (0, 0))],
        out_specs=pl.BlockSpec((tm, D), lambda i: (i, 0)),
        out_shape=jax.ShapeDtypeStruct((t, D), BF16),
        compiler_params=_params(("arbitrary",)),
        name="pool",
    )(z, z, z, w_pool.astype(BF16), pool_scale.reshape(1, POOL_DIM), w_b.astype(BF16))


def _mix_body(o_ref, ga_ref, cb_ref, x_ref, g1_ref, nf_ref, sc_ref, sh_ref, wa_ref, wo_ref, wr_ref,
              h_ref, u_ref, lg_ref):
    ya = jnp.dot(o_ref[...], wa_ref[...], preferred_element_type=F32)
    merged = ya * _sigmoid(ga_ref[...].astype(F32)) + cb_ref[...].astype(F32)
    mo = jnp.dot(merged.astype(BF16), wo_ref[...], preferred_element_type=F32)
    h = x_ref[...] + g1_ref[0] * mo
    h_ref[...] = h
    ms = jnp.mean(h * h, axis=-1, keepdims=True)
    u = h * lax.rsqrt(ms + EPS) * nf_ref[...]
    u = u * (1.0 + sc_ref[0]) + sh_ref[0]
    ub = u.astype(BF16)
    u_ref[...] = ub.reshape(u_ref.shape)
    lg_ref[...] = jnp.dot(ub, wr_ref[...], preferred_element_type=F32)


def _mix(o_gla, z, cb, x2, gate1, norm_ffn, scale2, shift2, w_a, w_out, w_router, seq):
    t = x2.shape[0]
    tm = min(TM_MIX, seq)
    per_seq = seq // tm
    ga_blk = (2 * KEY_DIM + 2 * VAL_DIM) // D
    tok = lambda i: (i, 0)
    const = lambda i: (0, 0)
    per_b = lambda i: (i // per_seq, 0, 0)
    return pl.pallas_call(
        _mix_body,
        grid=(t // tm,),
        in_specs=[pl.BlockSpec((tm, VAL_DIM), tok),
                  pl.BlockSpec((tm, D), lambda i: (i, ga_blk)),
                  pl.BlockSpec((tm, D), tok),
                  pl.BlockSpec((tm, D), tok),
                  pl.BlockSpec((1, 1, D), per_b),
                  pl.BlockSpec((1, D), const),
                  pl.BlockSpec((1, 1, D), per_b),
                  pl.BlockSpec((1, 1, D), per_b),
                  pl.BlockSpec((VAL_DIM, D), const, pipeline_mode=pl.Buffered(1)),
                  pl.BlockSpec((D, D), const, pipeline_mode=pl.Buffered(1)),
                  pl.BlockSpec((D, LANES), const)],
        out_specs=[pl.BlockSpec((tm, D), tok),
                   pl.BlockSpec((tm, ROW_SUB, LANES), lambda i: (i, 0, 0)),
                   pl.BlockSpec((tm, LANES), tok)],
        out_shape=[jax.ShapeDtypeStruct((t, D), F32),
                   jax.ShapeDtypeStruct((t, ROW_SUB, LANES), BF16),
                   jax.ShapeDtypeStruct((t, LANES), F32)],
        compiler_params=_params(("arbitrary",)),
        name="mix",
    )(o_gla, z, cb, x2, gate1, norm_ffn, scale2, shift2, w_a, w_out, w_router)


def _route_topk(gl, tm):
    lane = lax.broadcasted_iota(I32, (tm, LANES), 1)
    lanef = lane.astype(F32)
    neg = float("-inf")
    big = float(LANES)
    gmask = lane < N_GROUPS
    gmax = jnp.max(jnp.where(gmask, gl, neg), axis=-1, keepdims=True)
    gsel = jnp.min(jnp.where(gmask & (gl == gmax), lanef, big), axis=-1, keepdims=True)
    gsum = jnp.sum(jnp.where(gmask, jnp.exp(gl - gmax), 0.0), axis=-1, keepdims=True)
    pg = 1.0 / gsum
    lo = N_GROUPS + gsel * EXPERTS_PER_GROUP
    emask = (lanef >= lo) & (lanef < lo + EXPERTS_PER_GROUP)
    emax = jnp.max(jnp.where(emask, gl, neg), axis=-1, keepdims=True)
    ee = jnp.where(emask, jnp.exp(gl - emax), 0.0)
    eprob = ee / jnp.sum(ee, axis=-1, keepdims=True)
    v1 = jnp.max(jnp.where(emask, eprob, -1.0), axis=-1, keepdims=True)
    i1 = jnp.min(jnp.where(emask & (eprob == v1), lanef, big), axis=-1, keepdims=True)
    m2 = emask & (lanef != i1)
    v2 = jnp.max(jnp.where(m2, eprob, -1.0), axis=-1, keepdims=True)
    i2 = jnp.min(jnp.where(m2 & (eprob == v2), lanef, big), axis=-1, keepdims=True)
    tot = v1 + v2
    w1 = pg * (v1 / tot)
    w2 = pg * (v2 / tot)
    hit1 = lanef == i1 - N_GROUPS
    hit2 = lanef == i2 - N_GROUPS
    return hit1, hit2, w1, w2


def _count_body(lg_ref, bias_ref, cnt_ref, *, tm):
    @pl.when(pl.program_id(0) == 0)
    def _():
        cnt_ref[...] = jnp.zeros_like(cnt_ref)

    hit1, hit2, _, _ = _route_topk(lg_ref[...] + bias_ref[...], tm)
    oh = jnp.where(hit1 | hit2, 1.0, 0.0)
    cnt_ref[...] += jnp.sum(oh, axis=0, keepdims=True)


def _assign_body(lg_ref, bias_ref, cnt_ref, ls_ref, ut_ref, pos_ref, wt_ref, carry_ref, pst_ref, *, tm):
    lane = lax.broadcasted_iota(I32, (tm, LANES), 1)

    @pl.when(pl.program_id(0) == 0)
    def _():
        cnt = cnt_ref[0:1, :]
        nblk = jnp.floor((cnt + (ROW_BLOCK - 1)) * (1.0 / ROW_BLOCK))
        hi = jnp.floor(nblk * (1.0 / 256.0))
        lo = nblk - 256.0 * hi
        ut = ut_ref[...]
        hi8 = jnp.broadcast_to(hi, (8, LANES)).astype(BF16)
        lo8 = jnp.broadcast_to(lo, (8, LANES)).astype(BF16)
        pre = (256.0 * jnp.dot(hi8, ut, preferred_element_type=F32)
               + jnp.dot(lo8, ut, preferred_element_type=F32))
        pst_ref[...] = pre[0:1] * float(ROW_BLOCK)
        carry_ref[...] = jnp.zeros_like(carry_ref)

    hit1, hit2, w1, w2 = _route_topk(lg_ref[...] + bias_ref[...], tm)
    oh = jnp.where(hit1 | hit2, 1.0, 0.0)
    cum = jnp.dot(ls_ref[...], oh.astype(BF16), preferred_element_type=F32)
    slot = cum + carry_ref[...] + pst_ref[...]
    carry_ref[...] += jnp.sum(oh, axis=0, keepdims=True)
    p1 = jnp.sum(jnp.where(hit1, slot, 0.0), axis=-1, keepdims=True)
    p2 = jnp.sum(jnp.where(hit2, slot, 0.0), axis=-1, keepdims=True)
    pos_ref[...] = jnp.where(lane == 0, p1, jnp.where(lane == 1, p2, 0.0)).astype(I32)
    wt_ref[...] = jnp.where(lane == 0, w1, jnp.where(lane == 1, w2, 0.0))


def _route(logits, bias):
    t = logits.shape[0]
    tm = min(TM_ROUTE, t)
    r = jnp.arange(tm)
    ls = jnp.where(r[:, None] > r[None, :], 1.0, 0.0).astype(BF16)
    e = jnp.arange(LANES)
    ut = jnp.where(e[:, None] < e[None, :], 1.0, 0.0).astype(BF16)
    tok = lambda i: (i, 0)
    const = lambda i: (0, 0)
    cnt = pl.pallas_call(
        functools.partial(_count_body, tm=tm),
        grid=(t // tm,),
        in_specs=[pl.BlockSpec((tm, LANES), tok), pl.BlockSpec((1, LANES), const)],
        out_specs=pl.BlockSpec((8, LANES), const),
        out_shape=jax.ShapeDtypeStruct((8, LANES), F32),
        compiler_params=_params(("arbitrary",)),
        name="route_count",
    )(logits, bias)
    pos, wts = pl.pallas_call(
        functools.partial(_assign_body, tm=tm),
        grid=(t // tm,),
        in_specs=[pl.BlockSpec((tm, LANES), tok),
                  pl.BlockSpec((1, LANES), const),
                  pl.BlockSpec((8, LANES), const),
                  pl.BlockSpec((tm, tm), const),
                  pl.BlockSpec((LANES, LANES), const)],
        out_specs=[pl.BlockSpec((tm, LANES), tok), pl.BlockSpec((tm, LANES), tok)],
        out_shape=[jax.ShapeDtypeStruct((t, LANES), I32), jax.ShapeDtypeStruct((t, LANES), F32)],
        scratch_shapes=[pltpu.VMEM((1, LANES), F32), pltpu.VMEM((1, LANES), F32)],
        compiler_params=_params(("arbitrary",)),
        name="route_assign",
    )(logits, bias, cnt, ls, ut)
    return pos, wts, cnt


def _row_copy(src, s, dst, d, sem):
    return pltpu.make_async_copy(src.at[s], dst.at[d], sem)


def _list_copy(src, b, dst, s, sem):
    return pltpu.make_async_copy(src.at[b], dst.at[s], sem.at[s])


def _expert_body(be_ref, nr_ref, tok_ref, u_ref, wg_ref, wu_ref, wd_ref, y_ref, xbuf, tokbuf, sem, tsem, *, n_blocks):
    i = pl.program_id(0)
    slot = i % 2
    nrows = nr_ref[i]
    used = nrows > 0
    nrows_next = nr_ref[jnp.minimum(i + 1, n_blocks - 1)]
    next_used = (i + 1 < n_blocks) & (nrows_next > 0)
    next2_used = (i + 2 < n_blocks) & (nr_ref[jnp.minimum(i + 2, n_blocks - 1)] > 0)

    def pairs(n):
        return (n + 1) // 2

    def gather(s, n):
        def issue(p, carry):
            for k in range(2):
                r = 2 * p + k
                _row_copy(u_ref, tokbuf[s, 0, r], xbuf.at[s], r, sem.at[s]).start(priority=k)
            return carry
        lax.fori_loop(0, pairs(n), issue, 0)

    @pl.when(i == 0)
    def _():
        xbuf[...] = jnp.zeros_like(xbuf)

        @pl.when(used)
        def _():
            _list_copy(tok_ref, 0, tokbuf, 0, tsem).start()
            _list_copy(tok_ref, 0, tokbuf, 0, tsem).wait()
            gather(0, nrows)

        @pl.when(next_used)
        def _():
            _list_copy(tok_ref, 1, tokbuf, 1, tsem).start()

    @pl.when(used)
    def _():
        @pl.when(next2_used)
        def _():
            _list_copy(tok_ref, i + 2, tokbuf, slot, tsem).start()

        @pl.when(next_used)
        def _():
            _list_copy(tok_ref, i + 1, tokbuf, 1 - slot, tsem).wait()
            gather(1 - slot, nrows_next)

        def drain(p, carry):
            for k in range(2):
                _row_copy(u_ref, 0, xbuf.at[slot], 0, sem.at[slot]).wait()
            return carry
        lax.fori_loop(0, pairs(nrows), drain, 0)

    nsub = (nrows + SUB_ROWS - 1) // SUB_ROWS
    for k in range(1, ROW_BLOCK // SUB_ROWS + 1):
        @pl.when(nsub == k)
        def _(k=k):
            m = k * SUB_ROWS
            x = xbuf[slot, 0:m].reshape(m, D)
            a = jnp.dot(x, wg_ref[0], preferred_element_type=F32)
            b = jnp.dot(x, wu_ref[0], preferred_element_type=F32)
            hdn = (a * _sigmoid(a) * b).astype(BF16)
            y = jnp.dot(hdn, wd_ref[0], preferred_element_type=F32)
            y_ref[0:m] = y.astype(BF16).reshape(m, ROW_SUB, LANES)
            if m < ROW_BLOCK:
                y_ref[m:] = jnp.zeros((ROW_BLOCK - m, ROW_SUB, LANES), BF16)

    @pl.when(jnp.logical_not(used))
    def _():
        y_ref[...] = jnp.zeros_like(y_ref)


def _experts(u3, slot_tok, blk_e, blk_rows, w_gate, w_up, w_down):
    n_rows = slot_tok.shape[0]
    nb = n_rows // ROW_BLOCK
    rows = lambda i, be, nr: (i, 0, 0)
    wsel = lambda i, be, nr: (be[i], 0, 0)
    grid_spec = pltpu.PrefetchScalarGridSpec(
        num_scalar_prefetch=2,
        grid=(nb,),
        in_specs=[pl.BlockSpec(memory_space=pl.ANY),
                  pl.BlockSpec(memory_space=pl.ANY),
                  pl.BlockSpec((1, D, EXPERT_HIDDEN), wsel),
                  pl.BlockSpec((1, D, EXPERT_HIDDEN), wsel),
                  pl.BlockSpec((1, EXPERT_HIDDEN, D), wsel)],
        out_specs=pl.BlockSpec((ROW_BLOCK, ROW_SUB, LANES), rows),
        scratch_shapes=[pltpu.VMEM((2, ROW_BLOCK, ROW_SUB, LANES), BF16),
                        pltpu.SMEM((2, 1, ROW_BLOCK), I32),
                        pltpu.SemaphoreType.DMA((2,)),
                        pltpu.SemaphoreType.DMA((2,))],
    )
    return pl.pallas_call(
        functools.partial(_expert_body, n_blocks=nb),
        grid_spec=grid_spec,
        out_shape=jax.ShapeDtypeStruct((n_rows, ROW_SUB, LANES), BF16),
        compiler_params=_params(("arbitrary",)),
        name="experts",
    )(blk_e, blk_rows, slot_tok.reshape(nb, 1, ROW_BLOCK), u3, w_gate, w_up, w_down)


def _combine_body(cur_ref, nxt_ref, yb_ref, wt_ref, h_ref, g2_ref, nf_ref, o_ref, gbuf, sem, *, tm, n_steps):
    i = pl.program_id(0)
    slot = i % 2

    def gather(pos_ref, s):
        def issue(r, carry):
            for k in range(2):
                _row_copy(yb_ref, pos_ref[0, 0, k * tm + r], gbuf.at[s, k], r, sem.at[s]).start(priority=k)
            return carry
        lax.fori_loop(0, tm, issue, 0, unroll=4)

    @pl.when(i == 0)
    def _():
        gather(cur_ref, 0)

    @pl.when(i + 1 < n_steps)
    def _():
        gather(nxt_ref, 1 - slot)

    def drain(r, carry):
        for k in range(2):
            _row_copy(yb_ref, 0, gbuf.at[slot, k], 0, sem.at[slot]).wait()
        return carry
    lax.fori_loop(0, tm, drain, 0, unroll=4)

    wt = wt_ref[...]
    ya = gbuf[slot, 0].reshape(tm, D).astype(F32)
    yb = gbuf[slot, 1].reshape(tm, D).astype(F32)
    y = wt[:, 0:1] * ya + wt[:, 1:2] * yb
    h = h_ref[...] + g2_ref[0] * y
    ms = jnp.mean(h * h, axis=-1, keepdims=True)
    o_ref[...] = h * lax.rsqrt(ms + EPS) * nf_ref[...]


def _combine(pos, yb, wts, h1, gate2, norm_final, seq):
    t = h1.shape[0]
    tm = min(TM_ROWS, seq)
    per_seq = seq // tm
    tok = lambda i: (i, 0)
    n_steps = t // tm
    pos3 = jnp.concatenate([pos[:, 0].reshape(n_steps, 1, tm), pos[:, 1].reshape(n_steps, 1, tm)], axis=2)
    return pl.pallas_call(
        functools.partial(_combine_body, tm=tm, n_steps=n_steps),
        grid=(n_steps,),
        in_specs=[pl.BlockSpec((1, 1, 2 * tm), lambda i: (i, 0, 0), memory_space=pltpu.SMEM),
                  pl.BlockSpec((1, 1, 2 * tm), lambda i: (jnp.minimum(i + 1, n_steps - 1), 0, 0),
                               memory_space=pltpu.SMEM),
                  pl.BlockSpec(memory_space=pl.ANY),
                  pl.BlockSpec((tm, LANES), tok),
                  pl.BlockSpec((tm, D), tok),
                  pl.BlockSpec((1, 1, D), lambda i: (i // per_seq, 0, 0)),
                  pl.BlockSpec((1, D), lambda i: (0, 0))],
        out_specs=pl.BlockSpec((tm, D), tok),
        out_shape=jax.ShapeDtypeStruct((t, D), F32),
        scratch_shapes=[pltpu.VMEM((2, 2, tm, ROW_SUB, LANES), BF16), pltpu.SemaphoreType.DMA((2,))],
        compiler_params=_params(("arbitrary",)),
        name="combine",
    )(pos3, pos3, yb, wts, h1, gate2, norm_final)


def _layer(h2d, c, bsz, seq, w_ada, b_ada, norm_mix, w_in, w_gk2, b_gk2, gla_norm, w_a, w_pool, pool_scale, w_b,
           w_out, norm_ffn, w_rg, b_rg, w_re, b_re, w_gate, w_up, w_down):
    t = bsz * seq
    mod = _ada(c, w_ada, b_ada)
    shift1, scale1, gate1, shift2, scale2, gate2 = [m.reshape(bsz, 1, D) for m in jnp.split(mod, 6, axis=-1)]

    c_gk = 2 * KEY_DIM + 2 * VAL_DIM
    c_p = c_gk + GATE_RANK
    c_ga = c_p + POOL_DIM
    w_main = jnp.concatenate([w_in[:, :c_gk], w_in[:, c_ga:], w_in[:, c_p:c_ga]], axis=1).astype(BF16)
    w_gk = jnp.zeros((D, LANES), F32).at[:, :GATE_RANK].set(w_in[:, c_gk:c_p]).astype(BF16)

    z, gk, wg16 = _inproj(h2d, norm_mix.reshape(1, D), scale1, shift1, w_main, w_gk, w_gate, seq)
    o_gla, wu16, wd16 = _gla(z, gk, w_gk2, b_gk2, gla_norm, w_up, w_down, bsz, seq)
    cb = _pool(z, w_pool, pool_scale, w_b, seq)

    w_router = jnp.zeros((D, LANES), F32).at[:, :N_GROUPS].set(w_rg)
    w_router = w_router.at[:, N_GROUPS:N_GROUPS + N_EXPERTS].set(w_re).astype(BF16)
    h1, u2, logits = _mix(o_gla, z, cb, h2d, gate1, norm_ffn.reshape(1, D), scale2, shift2,
                          w_a.astype(BF16), w_out.astype(BF16), w_router, seq)

    bias = jnp.zeros((1, LANES), F32).at[0, :N_GROUPS].set(b_rg).at[0, N_GROUPS:N_GROUPS + N_EXPERTS].set(b_re)
    pos, wts, cnt = _route(logits, bias)
    counts = cnt[0, :N_EXPERTS].astype(I32)
    nblk = (counts + ROW_BLOCK - 1) // ROW_BLOCK
    bends = jnp.cumsum(nblk)
    n_rows = 2 * t + N_EXPERTS * ROW_BLOCK
    n_blocks = n_rows // ROW_BLOCK
    blk = jnp.arange(n_blocks, dtype=I32)
    blk_e = jnp.sum(bends[None, :] <= blk[:, None], axis=1)
    blk_e = jnp.minimum(blk_e, N_EXPERTS - 1).astype(I32)
    blk_rows = jnp.clip(counts[blk_e] - (blk - (bends - nblk)[blk_e]) * ROW_BLOCK, 0, ROW_BLOCK)
    blk_rows = jnp.where(blk < bends[-1], blk_rows, 0).astype(I32)
    pstart = (bends - nblk) * ROW_BLOCK
    n_fill = nblk * ROW_BLOCK - counts
    m = jnp.arange(ROW_BLOCK, dtype=I32)[None, :]
    e_col = jnp.arange(N_EXPERTS, dtype=I32)[:, None]
    fill_key = jnp.where(m < n_fill[:, None], (pstart + counts)[:, None] + m, n_rows + e_col * ROW_BLOCK + m)
    tok = jnp.arange(t, dtype=I32)
    keys = jnp.concatenate([pos[:, 0], pos[:, 1], fill_key.reshape(-1)])
    vals = jnp.concatenate([tok, tok, jnp.zeros((N_EXPERTS * ROW_BLOCK,), I32)])
    slot_tok = lax.sort((keys, vals), num_keys=1)[1]

    yb = _experts(u2, slot_tok, blk_e, blk_rows, wg16, wu16, wd16)
    return pos, yb, wts, h1, gate2


def kernel(x, c, w_ada, b_ada, norm_mix, w_in, w_gk2, b_gk2, gla_norm, w_a, w_pool, pool_scale, w_b, w_out, norm_ffn, w_rg, b_rg, w_re, b_re, w_gate, w_up, w_down, norm_final):
    bsz, seq, _ = x.shape
    depth = w_ada.shape[0]
    assert depth == 1
    h2d = x.reshape(bsz * seq, D)
    pos, yb, wts, h1, gate2 = _layer(
        h2d, c, bsz, seq, w_ada[0], b_ada[0], norm_mix[0], w_in[0], w_gk2[0], b_gk2[0], gla_norm[0], w_a[0],
        w_pool[0], pool_scale[0], w_b[0], w_out[0], norm_ffn[0], w_rg[0], b_rg[0], w_re[0], b_re[0],
        w_gate[0], w_up[0], w_down[0])
    out = _combine(pos, yb, wts, h1, gate2, norm_final.reshape(1, D), seq)
    return out.reshape(bsz, seq, D)
```
